```python
import jax
import jax.numpy as jnp
from jax import lax
import numpy as np

D_MODEL = 1024
BATCH = 8
SEQ = 8192
DEPTH = 4


HEAD_DIM = 64
N_MIX_HEADS = D_MODEL // HEAD_DIM
HEADS_A = (3 * N_MIX_HEADS) // 8
HEADS_B = (3 * N_MIX_HEADS) // 8
HEADS_C = N_MIX_HEADS - HEADS_A - HEADS_B
Q_LORA = D_MODEL // 4
KV_LORA = D_MODEL // 8
QK_NOPE = HEAD_DIM
QK_ROPE = HEAD_DIM // 2
V_DIM_A = HEAD_DIM
DILATED_PAIRS = ((128, 1), (512, 4), (2048, 16))
GRID_W = 64
NA_ROWS = 8
NA_COLS = 16
NA_QCOLS = 16
NA_KCOLS = 2 * NA_COLS
D_FF = 4 * D_MODEL
ROPE_THETA = 10000.0
Q_BLOCK = 128
NORM_EPS = 1e-6
NEG_INF = -1e30

COLS_A = Q_LORA + KV_LORA + QK_ROPE
COLS_B = 3 * HEADS_B * HEAD_DIM
COLS_C = 3 * HEADS_C * HEAD_DIM
IN_COLS = COLS_A + COLS_B + COLS_C
WIDTH_A = HEADS_A * V_DIM_A
WIDTH_B = HEADS_B * HEAD_DIM
WIDTH_C = HEADS_C * HEAD_DIM
MIX_WIDTH = WIDTH_A + WIDTH_B + WIDTH_C

kernel_name = "hybrid_mla_dilated_natten_encoder"


def rms_norm(x, g):
    xf = x.astype(jnp.float32)
    y = xf * lax.rsqrt(jnp.mean(xf * xf, axis=-1, keepdims=True) + NORM_EPS)
    return (y * g.astype(jnp.float32)).astype(x.dtype)


def rope(x, pos):
    half = x.shape[-1] // 2
    inv_freq = ROPE_THETA ** (-jnp.arange(half, dtype=jnp.float32) / half)
    ang = pos[:, None] * inv_freq[None, :]
    cos = jnp.cos(ang)[None, :, None, :]
    sin = jnp.sin(ang)[None, :, None, :]
    x1 = x[..., :half].astype(jnp.float32)
    x2 = x[..., half:].astype(jnp.float32)
    return jnp.concatenate([x1 * cos - x2 * sin, x1 * sin + x2 * cos], axis=-1).astype(x.dtype)


def dense_attention(q, k, v):
    b, s, h, dq = q.shape
    scale = dq ** -0.5
    qb = q.reshape(b, s // Q_BLOCK, Q_BLOCK, h, dq).transpose(1, 0, 2, 3, 4)

    def one_block(q_blk):
        sc = jnp.einsum('bqhd,bkhd->bhqk', q_blk, k, preferred_element_type=jnp.float32) * scale
        p = jax.nn.softmax(sc, axis=-1)
        return jnp.einsum('bhqk,bkhd->bqhd', p.astype(v.dtype), v)

    o = lax.map(one_block, qb)
    return o.transpose(1, 0, 2, 3, 4).reshape(b, s, h, v.shape[-1])


def banded_attention(q, k, v, half):
    bq, n, h, d = q.shape
    blk = half
    nb = -(-n // blk)
    n_pad = nb * blk
    qp = jnp.pad(q, ((0, 0), (0, n_pad - n), (0, 0), (0, 0))).reshape(bq, nb, blk, h, d)
    pad_kv = ((0, 0), (blk, n_pad - n + blk), (0, 0), (0, 0))
    kp = jnp.pad(k, pad_kv).reshape(bq, nb + 2, blk, h, d)
    vp = jnp.pad(v, pad_kv).reshape(bq, nb + 2, blk, h, d)
    kw = jnp.concatenate([kp[:, :-2], kp[:, 1:-1], kp[:, 2:]], axis=2)
    vw = jnp.concatenate([vp[:, :-2], vp[:, 1:-1], vp[:, 2:]], axis=2)
    q_idx = jnp.arange(n_pad).reshape(nb, blk)
    k_idx = jnp.arange(nb)[:, None] * blk - blk + jnp.arange(3 * blk)[None, :]
    mask = ((jnp.abs(q_idx[:, :, None] - k_idx[:, None, :]) <= half)
            & (k_idx[:, None, :] >= 0) & (k_idx[:, None, :] < n))
    sc = jnp.einsum('bnqhd,bnkhd->bnhqk', qp, kw, preferred_element_type=jnp.float32) * (d ** -0.5)
    sc = jnp.where(mask[None, :, None], sc, NEG_INF)
    m = jnp.max(sc, axis=-1, keepdims=True)
    p = jnp.exp(sc - m)
    den = jnp.sum(p, axis=-1)
    o = jnp.einsum('bnhqk,bnkhd->bnqhd', p.astype(v.dtype), vw).astype(jnp.float32)
    o = o / den.transpose(0, 1, 3, 2)[..., None]
    lse = (m[..., 0] + jnp.log(den)).transpose(0, 1, 3, 2)
    o = o.reshape(bq, n_pad, h, d)[:, :n]
    lse = lse.reshape(bq, n_pad, h)[:, :n]
    return o, lse


def dilated_sliding_attention(q, k, v):
    b, s, h, d = q.shape
    outs, lses = [], []
    for window, dil in DILATED_PAIRS:
        n = s // dil
        qc = q.reshape(b, n, dil, h, d).transpose(0, 2, 1, 3, 4).reshape(b * dil, n, h, d)
        kc = k.reshape(b, n, dil, h, d).transpose(0, 2, 1, 3, 4).reshape(b * dil, n, h, d)
        vc = v.reshape(b, n, dil, h, d).transpose(0, 2, 1, 3, 4).reshape(b * dil, n, h, d)
        o, lse = banded_attention(qc, kc, vc, window // (2 * dil))
        outs.append(o.reshape(b, dil, n, h, d).transpose(0, 2, 1, 3, 4).reshape(b, s, h, d))
        lses.append(lse.reshape(b, dil, n, h).transpose(0, 2, 1, 3).reshape(b, s, h))
    w = jax.nn.softmax(jnp.stack(lses, axis=-1), axis=-1)
    o = jnp.sum(jnp.stack(outs, axis=-1) * w[:, :, :, None, :], axis=-1)
    return o.astype(q.dtype)


def neighbourhood_attention(q, k, v, rpb):
    b, s, h, d = q.shape
    rows = s // GRID_W
    kr_win = min(NA_ROWS, rows)
    q_rows = kr_win
    k_rows = min(2 * kr_win, rows)
    nrb = -(-rows // q_rows)
    rows_pad = nrb * q_rows
    ncb = GRID_W // NA_QCOLS
    qg = jnp.pad(q.reshape(b, rows, GRID_W, h, d), ((0, 0), (0, rows_pad - rows), (0, 0), (0, 0), (0, 0)))
    qg = qg.reshape(b, nrb, q_rows, ncb, NA_QCOLS, h, d).transpose(0, 1, 3, 2, 4, 5, 6)
    r_q = jnp.arange(rows_pad).reshape(nrb, q_rows)
    r_start = jnp.clip(r_q - kr_win // 2, 0, rows - kr_win)
    c_q = jnp.arange(GRID_W).reshape(ncb, NA_QCOLS)
    c_start = jnp.clip(c_q - NA_COLS // 2, 0, GRID_W - NA_COLS)
    kr_idx = (jnp.clip(jnp.arange(nrb) * q_rows - kr_win // 2, 0, rows - k_rows)[:, None]
              + jnp.arange(k_rows)[None, :])
    kc_idx = (jnp.clip(jnp.arange(ncb) * NA_QCOLS - NA_COLS // 2, 0, GRID_W - NA_KCOLS)[:, None]
              + jnp.arange(NA_KCOLS)[None, :])
    kgrid = k.reshape(b, rows, GRID_W, h, d)
    vgrid = v.reshape(b, rows, GRID_W, h, d)
    ri = kr_idx[:, None, :, None]
    ci = kc_idx[None, :, None, :]
    kg = kgrid[:, ri, ci]
    vg = vgrid[:, ri, ci]
    row_ok = (kr_idx[:, None, :] >= r_start[:, :, None]) & (kr_idx[:, None, :] < r_start[:, :, None] + kr_win)
    col_ok = (kc_idx[:, None, :] >= c_start[:, :, None]) & (kc_idx[:, None, :] < c_start[:, :, None] + NA_COLS)
    dr = jnp.clip(kr_idx[:, None, :] - r_q[:, :, None], -(NA_ROWS - 1), NA_ROWS - 1) + (NA_ROWS - 1)
    dc = jnp.clip(kc_idx[:, None, :] - c_q[:, :, None], -(NA_COLS - 1), NA_COLS - 1) + (NA_COLS - 1)
    bias = rpb[:, dr[:, None, :, None, :, None], dc[None, :, None, :, None, :]]
    mask = row_ok[:, None, :, None, :, None] & col_ok[None, :, None, :, None, :]
    sc = jnp.einsum('bnmiphd,bnmjqhd->bnmhipjq', qg, kg, preferred_element_type=jnp.float32) * (d ** -0.5)
    sc = sc + bias.transpose(1, 2, 0, 3, 4, 5, 6).astype(jnp.float32)[None]
    sc = jnp.where(mask[:, :, None][None], sc, NEG_INF)
    shp = sc.shape
    p = jax.nn.softmax(sc.reshape(shp[:-2] + (k_rows * NA_KCOLS,)), axis=-1).reshape(shp)
    o = jnp.einsum('bnmhipjq,bnmjqhd->bnmiphd', p.astype(v.dtype), vg)
    o = o.transpose(0, 1, 3, 2, 4, 5, 6).reshape(b, rows_pad, GRID_W, h, d)[:, :rows]
    return o.reshape(b, s, h, d)


def _fwd_setup_inputs(seed: int = 0) -> dict:
    key = jax.random.key(seed)
    ks = jax.random.split(key, 16)

    def normal(k, shape, scale):
        return jax.random.normal(k, shape, dtype=jnp.float32) * scale

    def gain(k, shape):
        return 1.0 + 0.02 * jax.random.normal(k, shape, dtype=jnp.float32)

    return {
        "x": normal(ks[0], (BATCH, SEQ, D_MODEL), 1.0),
        "g_mix": gain(ks[1], (DEPTH, D_MODEL)),
        "w_in": normal(ks[2], (DEPTH, D_MODEL, IN_COLS), D_MODEL ** -0.5),
        "q_norm": gain(ks[3], (DEPTH, Q_LORA)),
        "w_uq": normal(ks[4], (DEPTH, Q_LORA, HEADS_A * (QK_NOPE + QK_ROPE)), Q_LORA ** -0.5),
        "kv_norm": gain(ks[5], (DEPTH, KV_LORA)),
        "w_ukv": normal(ks[6], (DEPTH, KV_LORA, HEADS_A * (QK_NOPE + V_DIM_A)), KV_LORA ** -0.5),
        "rpb": normal(ks[7], (DEPTH, HEADS_C, 2 * NA_ROWS - 1, 2 * NA_COLS - 1), 0.1),
        "out_norm_a": gain(ks[8], (DEPTH, WIDTH_A)),
        "out_norm_b": gain(ks[9], (DEPTH, WIDTH_B)),
        "out_norm_c": gain(ks[10], (DEPTH, WIDTH_C)),
        "w_out": normal(ks[11], (DEPTH, MIX_WIDTH, D_MODEL), MIX_WIDTH ** -0.5),
        "g_mlp": gain(ks[12], (DEPTH, D_MODEL)),
        "w_mlp_in": normal(ks[13], (DEPTH, D_MODEL, D_FF), D_MODEL ** -0.5),
        "w_mlp_out": normal(ks[14], (DEPTH, D_FF, D_MODEL), D_FF ** -0.5),
        "g_final": gain(ks[15], (D_MODEL,)),
    }


def _fwd_reference(x, g_mix, w_in, q_norm, w_uq, kv_norm, w_ukv, rpb, out_norm_a, out_norm_b, out_norm_c,
              w_out, g_mlp, w_mlp_in, w_mlp_out, g_final):
    b, s, _ = x.shape
    pos = jnp.arange(s, dtype=jnp.float32)
    for l in range(DEPTH):
        h = rms_norm(x, g_mix[l])
        proj = h @ w_in[l]
        p_a = proj[..., :COLS_A]
        p_b = proj[..., COLS_A:COLS_A + COLS_B]
        p_c = proj[..., COLS_A + COLS_B:]
        c_q = p_a[..., :Q_LORA]
        c_kv = p_a[..., Q_LORA:Q_LORA + KV_LORA]
        k_pe = p_a[..., Q_LORA + KV_LORA:]
        qa = (rms_norm(c_q, q_norm[l]) @ w_uq[l]).reshape(b, s, HEADS_A, QK_NOPE + QK_ROPE)
        kva = (rms_norm(c_kv, kv_norm[l]) @ w_ukv[l]).reshape(b, s, HEADS_A, QK_NOPE + V_DIM_A)
        k_pe = jnp.broadcast_to(rope(k_pe[:, :, None, :], pos), (b, s, HEADS_A, QK_ROPE))
        qa = jnp.concatenate([qa[..., :QK_NOPE], rope(qa[..., QK_NOPE:], pos)], axis=-1)
        ka = jnp.concatenate([kva[..., :QK_NOPE], k_pe], axis=-1)
        o_a = dense_attention(qa, ka, kva[..., QK_NOPE:])
        pb = p_b.reshape(b, s, 3, HEADS_B, HEAD_DIM)
        o_b = dilated_sliding_attention(rope(pb[:, :, 0], pos), rope(pb[:, :, 1], pos), pb[:, :, 2])
        pc = p_c.reshape(b, s, 3, HEADS_C, HEAD_DIM)
        o_c = neighbourhood_attention(pc[:, :, 0], pc[:, :, 1], pc[:, :, 2], rpb[l])
        mixed = jnp.concatenate([
            rms_norm(o_a.reshape(b, s, WIDTH_A), out_norm_a[l]),
            rms_norm(o_b.reshape(b, s, WIDTH_B), out_norm_b[l]),
            rms_norm(o_c.reshape(b, s, WIDTH_C), out_norm_c[l]),
        ], axis=-1)
        x = x + mixed @ w_out[l]
        h2 = rms_norm(x, g_mlp[l])
        x = x + jnp.square(jax.nn.relu(h2 @ w_mlp_in[l])) @ w_mlp_out[l]
    return rms_norm(x, g_final)


import jax as _jax
import jax.numpy as _jnp

TWIN_FORMAT = 'train_step'
FWD_PARAMS = ['x', 'g_mix', 'w_in', 'q_norm', 'w_uq', 'kv_norm', 'w_ukv', 'rpb', 'out_norm_a', 'out_norm_b', 'out_norm_c', 'w_out', 'g_mlp', 'w_mlp_in', 'w_mlp_out', 'g_final']
TWIN_WEIGHTS = ['g_mix', 'w_in', 'q_norm', 'w_uq', 'kv_norm', 'w_ukv', 'rpb', 'out_norm_a', 'out_norm_b', 'out_norm_c', 'w_out', 'g_mlp', 'w_mlp_in', 'w_mlp_out', 'g_final']
TWIN_DIFF_INPUT = 'x'
TWIN_INPUTS = ['x', 'g_mix', 'w_in', 'q_norm', 'w_uq', 'kv_norm', 'w_ukv', 'rpb', 'out_norm_a', 'out_norm_b', 'out_norm_c', 'w_out', 'g_mlp', 'w_mlp_in', 'w_mlp_out', 'g_final', 'loss_target', 'm_g_mix', 'm_w_in', 'm_q_norm', 'm_w_uq', 'm_kv_norm', 'm_w_ukv', 'm_rpb', 'm_out_norm_a', 'm_out_norm_b', 'm_out_norm_c', 'm_w_out', 'm_g_mlp', 'm_w_mlp_in', 'm_w_mlp_out', 'm_g_final', 'v_g_mix', 'v_w_in', 'v_q_norm', 'v_w_uq', 'v_kv_norm', 'v_w_ukv', 'v_rpb', 'v_out_norm_a', 'v_out_norm_b', 'v_out_norm_c', 'v_w_out', 'v_g_mlp', 'v_w_mlp_in', 'v_w_mlp_out', 'v_g_final']
TWIN_OUTPUTS = ['loss', 'grad_x', 'grad_g_mix', 'grad_w_in', 'grad_q_norm', 'grad_w_uq', 'grad_kv_norm', 'grad_w_ukv', 'grad_rpb', 'grad_out_norm_a', 'grad_out_norm_b', 'grad_out_norm_c', 'grad_w_out', 'grad_g_mlp', 'grad_w_mlp_in', 'grad_w_mlp_out', 'grad_g_final', 'delta_g_mix', 'delta_w_in', 'delta_q_norm', 'delta_w_uq', 'delta_kv_norm', 'delta_w_ukv', 'delta_rpb', 'delta_out_norm_a', 'delta_out_norm_b', 'delta_out_norm_c', 'delta_w_out', 'delta_g_mlp', 'delta_w_mlp_in', 'delta_w_mlp_out', 'delta_g_final', 'new_m_g_mix', 'new_m_w_in', 'new_m_q_norm', 'new_m_w_uq', 'new_m_kv_norm', 'new_m_w_ukv', 'new_m_rpb', 'new_m_out_norm_a', 'new_m_out_norm_b', 'new_m_out_norm_c', 'new_m_w_out', 'new_m_g_mlp', 'new_m_w_mlp_in', 'new_m_w_mlp_out', 'new_m_g_final', 'new_v_g_mix', 'new_v_w_in', 'new_v_q_norm', 'new_v_w_uq', 'new_v_kv_norm', 'new_v_w_ukv', 'new_v_rpb', 'new_v_out_norm_a', 'new_v_out_norm_b', 'new_v_out_norm_c', 'new_v_w_out', 'new_v_g_mlp', 'new_v_w_mlp_in', 'new_v_w_mlp_out', 'new_v_g_final']
TWIN_LEAF_KINDS = {'loss': 'loss', 'grad_x': 'grad_x', 'grad_g_mix': 'grad_w', 'grad_w_in': 'grad_w', 'grad_q_norm': 'grad_w', 'grad_w_uq': 'grad_w', 'grad_kv_norm': 'grad_w', 'grad_w_ukv': 'grad_w', 'grad_rpb': 'grad_w', 'grad_out_norm_a': 'grad_w', 'grad_out_norm_b': 'grad_w', 'grad_out_norm_c': 'grad_w', 'grad_w_out': 'grad_w', 'grad_g_mlp': 'grad_w', 'grad_w_mlp_in': 'grad_w', 'grad_w_mlp_out': 'grad_w', 'grad_g_final': 'grad_w', 'delta_g_mix': 'delta_w', 'delta_w_in': 'delta_w', 'delta_q_norm': 'delta_w', 'delta_w_uq': 'delta_w', 'delta_kv_norm': 'delta_w', 'delta_w_ukv': 'delta_w', 'delta_rpb': 'delta_w', 'delta_out_norm_a': 'delta_w', 'delta_out_norm_b': 'delta_w', 'delta_out_norm_c': 'delta_w', 'delta_w_out': 'delta_w', 'delta_g_mlp': 'delta_w', 'delta_w_mlp_in': 'delta_w', 'delta_w_mlp_out': 'delta_w', 'delta_g_final': 'delta_w', 'new_m_g_mix': 'new_m', 'new_m_w_in': 'new_m', 'new_m_q_norm': 'new_m', 'new_m_w_uq': 'new_m', 'new_m_kv_norm': 'new_m', 'new_m_w_ukv': 'new_m', 'new_m_rpb': 'new_m', 'new_m_out_norm_a': 'new_m', 'new_m_out_norm_b': 'new_m', 'new_m_out_norm_c': 'new_m', 'new_m_w_out': 'new_m', 'new_m_g_mlp': 'new_m', 'new_m_w_mlp_in': 'new_m', 'new_m_w_mlp_out': 'new_m', 'new_m_g_final': 'new_m', 'new_v_g_mix': 'new_v', 'new_v_w_in': 'new_v', 'new_v_q_norm': 'new_v', 'new_v_w_uq': 'new_v', 'new_v_kv_norm': 'new_v', 'new_v_w_ukv': 'new_v', 'new_v_rpb': 'new_v', 'new_v_out_norm_a': 'new_v', 'new_v_out_norm_b': 'new_v', 'new_v_out_norm_c': 'new_v', 'new_v_w_out': 'new_v', 'new_v_g_mlp': 'new_v', 'new_v_w_mlp_in': 'new_v', 'new_v_w_mlp_out': 'new_v', 'new_v_g_final': 'new_v'}


def _forward(args):
    return _fwd_reference(*[args[k] for k in FWD_PARAMS])


def _output_shape():
    def fwd():
        inp = _fwd_setup_inputs(0)
        return _fwd_reference(*[inp[k] for k in FWD_PARAMS])
    out = _jax.eval_shape(fwd)
    return out.shape, out.dtype

N_MICROBATCH = 1
ADAM_LR = 0.001
ADAM_B1 = 0.9
ADAM_B2 = 0.999
ADAM_EPS = 1e-08
ADAM_WD = 0.01
ADAM_STEP = 10
PER_EXAMPLE_BATCH_AXIS = {'x': 0, 'loss_target': 0}
SHARED_INPUTS = []
_WEIGHT_DTYPES = {'g_mix': _jnp.float32, 'w_in': _jnp.float32, 'q_norm': _jnp.float32, 'w_uq': _jnp.float32, 'kv_norm': _jnp.float32, 'w_ukv': _jnp.float32, 'rpb': _jnp.float32, 'out_norm_a': _jnp.float32, 'out_norm_b': _jnp.float32, 'out_norm_c': _jnp.float32, 'w_out': _jnp.float32, 'g_mlp': _jnp.float32, 'w_mlp_in': _jnp.float32, 'w_mlp_out': _jnp.float32, 'g_final': _jnp.float32}
MOMENT_SCALE = {'g_mix': 2.936896e-01, 'w_in': 1.985772e-01, 'q_norm': 1.083034e-01, 'w_uq': 7.179866e-02, 'kv_norm': 7.051107e-01, 'w_ukv': 2.313511e-01, 'rpb': 3.355803e-02, 'out_norm_a': 3.351840e-01, 'out_norm_b': 2.982278e-01, 'out_norm_c': 2.873802e-01, 'w_out': 2.978677e-01, 'g_mlp': 1.954255e-01, 'w_mlp_in': 9.449431e-02, 'w_mlp_out': 3.574104e-01, 'g_final': 6.854994e+01}


def _to_microbatches(a, axis):
    t = _jnp.moveaxis(a, axis, 0)
    t = t.reshape((N_MICROBATCH, t.shape[0] // N_MICROBATCH) + t.shape[1:])
    return _jnp.moveaxis(t, 1, axis + 1)


def setup_inputs(seed: int = 0) -> dict:
    inp = _fwd_setup_inputs(seed)
    key = _jax.random.fold_in(_jax.random.key(seed), 7919)
    shape, _ = _output_shape()
    out = dict(inp)
    out["loss_target"] = _jax.random.normal(_jax.random.fold_in(key, 0), shape, _jnp.float32)
    for i, name in enumerate(TWIN_WEIGHTS):
        w = inp[name].astype(_jnp.float32)
        if MOMENT_SCALE is None:
            s = _jnp.sqrt(_jnp.mean(_jnp.square(w)) + 1e-30)
        else:
            s = MOMENT_SCALE[name]
        km, kv = _jax.random.split(_jax.random.fold_in(key, i + 1))
        out[name] = w
        out["m_" + name] = s * _jax.random.normal(km, w.shape, _jnp.float32)
        out["v_" + name] = (s * s) * _jax.random.uniform(kv, w.shape, _jnp.float32, 0.5, 1.5)
    if N_MICROBATCH > 1:
        for name, axis in PER_EXAMPLE_BATCH_AXIS.items():
            out[name] = _to_microbatches(out[name], axis)
    return {'x': out['x'], 'g_mix': out['g_mix'], 'w_in': out['w_in'], 'q_norm': out['q_norm'], 'w_uq': out['w_uq'], 'kv_norm': out['kv_norm'], 'w_ukv': out['w_ukv'], 'rpb': out['rpb'], 'out_norm_a': out['out_norm_a'], 'out_norm_b': out['out_norm_b'], 'out_norm_c': out['out_norm_c'], 'w_out': out['w_out'], 'g_mlp': out['g_mlp'], 'w_mlp_in': out['w_mlp_in'], 'w_mlp_out': out['w_mlp_out'], 'g_final': out['g_final'], 'loss_target': out['loss_target'], 'm_g_mix': out['m_g_mix'], 'm_w_in': out['m_w_in'], 'm_q_norm': out['m_q_norm'], 'm_w_uq': out['m_w_uq'], 'm_kv_norm': out['m_kv_norm'], 'm_w_ukv': out['m_w_ukv'], 'm_rpb': out['m_rpb'], 'm_out_norm_a': out['m_out_norm_a'], 'm_out_norm_b': out['m_out_norm_b'], 'm_out_norm_c': out['m_out_norm_c'], 'm_w_out': out['m_w_out'], 'm_g_mlp': out['m_g_mlp'], 'm_w_mlp_in': out['m_w_mlp_in'], 'm_w_mlp_out': out['m_w_mlp_out'], 'm_g_final': out['m_g_final'], 'v_g_mix': out['v_g_mix'], 'v_w_in': out['v_w_in'], 'v_q_norm': out['v_q_norm'], 'v_w_uq': out['v_w_uq'], 'v_kv_norm': out['v_kv_norm'], 'v_w_ukv': out['v_w_ukv'], 'v_rpb': out['v_rpb'], 'v_out_norm_a': out['v_out_norm_a'], 'v_out_norm_b': out['v_out_norm_b'], 'v_out_norm_c': out['v_out_norm_c'], 'v_w_out': out['v_w_out'], 'v_g_mlp': out['v_g_mlp'], 'v_w_mlp_in': out['v_w_mlp_in'], 'v_w_mlp_out': out['v_w_mlp_out'], 'v_g_final': out['v_g_final']}


def _loss(weights, diff, rest, loss_target):
    with _jax.named_scope("forward"):
        args = {**rest, TWIN_DIFF_INPUT: diff, **{k: w.astype(_WEIGHT_DTYPES[k]) for k, w in weights.items()}}
        y = _forward(args)
    with _jax.named_scope("loss_head"):
        err = _jnp.square(y.astype(_jnp.float32) - loss_target)
        return 0.5 * _jnp.sum(_jnp.mean(err, axis=-1)) if err.ndim else 0.5 * err


def _adamw(w, g, m, v):
    m = ADAM_B1 * m + (1.0 - ADAM_B1) * g
    v = ADAM_B2 * v + (1.0 - ADAM_B2) * _jnp.square(g)
    m_hat = m / (1.0 - ADAM_B1 ** ADAM_STEP)
    v_hat = v / (1.0 - ADAM_B2 ** ADAM_STEP)
    delta = -ADAM_LR * (m_hat / (_jnp.sqrt(v_hat) + ADAM_EPS) + ADAM_WD * w)
    return delta, m, v


def reference(x, g_mix, w_in, q_norm, w_uq, kv_norm, w_ukv, rpb, out_norm_a, out_norm_b, out_norm_c, w_out, g_mlp, w_mlp_in, w_mlp_out, g_final, loss_target, m_g_mix, m_w_in, m_q_norm, m_w_uq, m_kv_norm, m_w_ukv, m_rpb, m_out_norm_a, m_out_norm_b, m_out_norm_c, m_w_out, m_g_mlp, m_w_mlp_in, m_w_mlp_out, m_g_final, v_g_mix, v_w_in, v_q_norm, v_w_uq, v_kv_norm, v_w_ukv, v_rpb, v_out_norm_a, v_out_norm_b, v_out_norm_c, v_w_out, v_g_mlp, v_w_mlp_in, v_w_mlp_out, v_g_final):
    given = dict(x=x, g_mix=g_mix, w_in=w_in, q_norm=q_norm, w_uq=w_uq, kv_norm=kv_norm, w_ukv=w_ukv, rpb=rpb, out_norm_a=out_norm_a, out_norm_b=out_norm_b, out_norm_c=out_norm_c, w_out=w_out, g_mlp=g_mlp, w_mlp_in=w_mlp_in, w_mlp_out=w_mlp_out, g_final=g_final, loss_target=loss_target, m_g_mix=m_g_mix, m_w_in=m_w_in, m_q_norm=m_q_norm, m_w_uq=m_w_uq, m_kv_norm=m_kv_norm, m_w_ukv=m_w_ukv, m_rpb=m_rpb, m_out_norm_a=m_out_norm_a, m_out_norm_b=m_out_norm_b, m_out_norm_c=m_out_norm_c, m_w_out=m_w_out, m_g_mlp=m_g_mlp, m_w_mlp_in=m_w_mlp_in, m_w_mlp_out=m_w_mlp_out, m_g_final=m_g_final, v_g_mix=v_g_mix, v_w_in=v_w_in, v_q_norm=v_q_norm, v_w_uq=v_w_uq, v_kv_norm=v_kv_norm, v_w_ukv=v_w_ukv, v_rpb=v_rpb, v_out_norm_a=v_out_norm_a, v_out_norm_b=v_out_norm_b, v_out_norm_c=v_out_norm_c, v_w_out=v_w_out, v_g_mlp=v_g_mlp, v_w_mlp_in=v_w_mlp_in, v_w_mlp_out=v_w_mlp_out, v_g_final=v_g_final)
    weights = {n: given[n] for n in TWIN_WEIGHTS}
    shared = {n: given[n] for n in SHARED_INPUTS}
    per_example = {n: given[n] for n in ['x']}
    grad_fn = _jax.value_and_grad(_loss, argnums=(0, 1))

    def one_microbatch(ex, loss_target):
        ex = dict(ex)
        diff = ex.pop(TWIN_DIFF_INPUT)
        return grad_fn(weights, diff, {**shared, **ex}, loss_target)

    if N_MICROBATCH == 1:
        loss, (grad_w, grad_x) = one_microbatch(per_example, given["loss_target"])
    else:
        def body(carry, xs):
            loss_sum, grad_sum = carry
            l_k, (gw_k, gx_k) = one_microbatch(xs[0], xs[1])
            with _jax.named_scope("update"):
                return (loss_sum + l_k, _jax.tree.map(_jnp.add, grad_sum, gw_k)), gx_k

        init = (_jnp.zeros((), _jnp.float32), _jax.tree.map(_jnp.zeros_like, weights))
        (loss, grad_w), grad_x = _jax.lax.scan(body, init, (per_example, given["loss_target"]))
    with _jax.named_scope("update"):
        delta_w, new_m, new_v = {}, {}, {}
        for n in TWIN_WEIGHTS:
            delta_w[n], new_m[n], new_v[n] = _adamw(weights[n], grad_w[n], given["m_" + n], given["v_" + n])
    return (loss, grad_x, *[grad_w[n] for n in TWIN_WEIGHTS], *[delta_w[n] for n in TWIN_WEIGHTS],
            *[new_m[n] for n in TWIN_WEIGHTS], *[new_v[n] for n in TWIN_WEIGHTS])
```

```python
import functools

import numpy as np
import jax
import jax.numpy as jnp
from jax import lax
from jax.experimental import pallas as pl
from jax.experimental.pallas import tpu as pltpu

F32 = jnp.float32
BF16 = jnp.bfloat16

D_MODEL = 1024
DEPTH = 4
HEAD_DIM = 64
HEADS_A = 6
HEADS_B = 6
HEADS_C = 4
Q_LORA = 256
KV_LORA = 128
QK_NOPE = 64
QK_ROPE = 32
V_DIM_A = 64
DILATED_PAIRS = ((128, 1), (512, 4), (2048, 16))
BAND_HALF = 64
GRID_W = 64
NA_ROWS = 8
NA_COLS = 16
D_FF = 4096
ROPE_THETA = 10000.0
NORM_EPS = 1e-6
NEG_INF = -1e30

COLS_A = Q_LORA + KV_LORA + QK_ROPE
COLS_B = 3 * HEADS_B * HEAD_DIM
COLS_C = 3 * HEADS_C * HEAD_DIM
WIDTH_A = HEADS_A * V_DIM_A
WIDTH_B = HEADS_B * HEAD_DIM
WIDTH_C = HEADS_C * HEAD_DIM

ADAM_LR = 0.001
ADAM_B1 = 0.9
ADAM_B2 = 0.999
ADAM_EPS = 1e-08
ADAM_WD = 0.01
ADAM_STEP = 10

N_DEV = 8
AXES = ("x", "y", "c")
MESH = pl.DeviceIdType.MESH

V7X_VMEM_LIMIT = 48 * 1024 * 1024
MM_VMEM_BUDGET = 32 * 1024 * 1024
LANES = 128
PACK_COLS = 1024
ATT_TILE = 128
NA_SPAN = 3


def _params(sem=None):
    return pltpu.CompilerParams(dimension_semantics=sem, vmem_limit_bytes=V7X_VMEM_LIMIT)


def _tile(dim, pref):
    if dim <= pref:
        return dim
    for t in range(pref - pref % LANES, LANES - 1, -LANES):
        if dim % t == 0:
            return t
    return dim


def _mm(a, b, *, ta=False, tb=False, out_dtype=F32, epilogue=None, extra=None, name):
    if ta:
        k_dim, m_dim = a.shape
    else:
        m_dim, k_dim = a.shape
    n_dim = b.shape[0] if tb else b.shape[1]
    tn = _tile(n_dim, 1024) if n_dim % LANES == 0 else n_dim
    tk = _tile(k_dim, 1024 if not ta else 512) if k_dim % LANES == 0 else k_dim
    n_out = 2 if epilogue == "relu2" else 1
    for tm in (_tile(m_dim, 1024), _tile(m_dim, 512), _tile(m_dim, 256)):
        blocks = (tm * tk * a.dtype.itemsize + tk * tn * b.dtype.itemsize
                  + (tm * tn * 4 if extra is not None else 0) + n_out * tm * tn * 4)
        if 2 * blocks + tm * tn * 4 <= MM_VMEM_BUDGET:
            break
    nk = k_dim // tk
    a_spec = (pl.BlockSpec((tk, tm), lambda i, j, k: (k, i)) if ta
              else pl.BlockSpec((tm, tk), lambda i, j, k: (i, k)))
    b_spec = (pl.BlockSpec((tn, tk), lambda i, j, k: (j, k)) if tb
              else pl.BlockSpec((tk, tn), lambda i, j, k: (k, j)))
    o_spec = pl.BlockSpec((tm, tn), lambda i, j, k: (i, j))
    dims = (((0 if ta else 1,), (1 if tb else 0,)), ((), ()))
    in_specs = [a_spec, b_spec]
    operands = [a, b]
    if epilogue in ("drelu2", "add"):
        in_specs.append(o_spec)
        operands.append(extra)
    if epilogue == "relu2":
        out_shape = (jax.ShapeDtypeStruct((m_dim, n_dim), F32), jax.ShapeDtypeStruct((m_dim, n_dim), BF16))
        out_specs = (o_spec, o_spec)
    else:
        out_shape = jax.ShapeDtypeStruct((m_dim, n_dim), out_dtype)
        out_specs = o_spec

    def kern(*refs):
        acc_ref = refs[-1]
        a_ref, b_ref = refs[0], refs[1]
        k = pl.program_id(2)

        @pl.when(k == 0)
        def _():
            acc_ref[...] = jnp.zeros_like(acc_ref)

        acc_ref[...] += lax.dot_general(a_ref[...].astype(BF16), b_ref[...].astype(BF16), dims,
                                        preferred_element_type=F32)

        @pl.when(k == nk - 1)
        def _():
            acc = acc_ref[...]
            if epilogue == "relu2":
                refs[2][...] = acc
                r = jnp.maximum(acc, 0.0)
                refs[3][...] = (r * r).astype(BF16)
            elif epilogue == "drelu2":
                refs[3][...] = (acc * (2.0 * jnp.maximum(refs[2][...], 0.0))).astype(out_dtype)
            elif epilogue == "add":
                refs[3][...] = (acc + refs[2][...]).astype(out_dtype)
            else:
                refs[2][...] = acc.astype(out_dtype)

    return pl.pallas_call(
        kern, name=name, out_shape=out_shape, grid=(m_dim // tm, n_dim // tn, nk),
        in_specs=in_specs, out_specs=out_specs, scratch_shapes=[pltpu.VMEM((tm, tn), F32)],
        compiler_params=_params(("parallel", "parallel", "arbitrary")),
    )(*operands)


def _norm_fwd(x, g, *, out_dtype, name):
    s, w = x.shape
    ts = _tile(s, 512)

    def kern(x_ref, g_ref, y_ref):
        xv = x_ref[...]
        r = lax.rsqrt(jnp.mean(xv * xv, axis=-1, keepdims=True) + NORM_EPS)
        y_ref[...] = (xv * r * g_ref[...]).astype(out_dtype)

    return pl.pallas_call(
        kern, name=name, out_shape=jax.ShapeDtypeStruct((s, w), out_dtype), grid=(s // ts,),
        in_specs=[pl.BlockSpec((ts, w), lambda i: (i, 0)), pl.BlockSpec((1, w), lambda i: (0, 0))],
        out_specs=pl.BlockSpec((ts, w), lambda i: (i, 0)),
        compiler_params=_params(("parallel",)),
    )(x, g.reshape(1, w))


def _norm_bwd(x, g, dy, *, name):
    s, w = x.shape
    ts = _tile(s, 512)

    def kern(x_ref, g_ref, dy_ref, dx_ref, dg_ref):
        @pl.when(pl.program_id(0) == 0)
        def _():
            dg_ref[...] = jnp.zeros_like(dg_ref)

        xv = x_ref[...]
        dyv = dy_ref[...]
        r = lax.rsqrt(jnp.mean(xv * xv, axis=-1, keepdims=True) + NORM_EPS)
        xhat = xv * r
        dg_ref[...] += jnp.sum(dyv * xhat, axis=0, keepdims=True)
        dxhat = dyv * g_ref[...]
        dx_ref[...] = r * (dxhat - xhat * jnp.mean(dxhat * xhat, axis=-1, keepdims=True))

    dx, dg = pl.pallas_call(
        kern, name=name,
        out_shape=(jax.ShapeDtypeStruct((s, w), F32), jax.ShapeDtypeStruct((1, w), F32)), grid=(s // ts,),
        in_specs=[pl.BlockSpec((ts, w), lambda i: (i, 0)), pl.BlockSpec((1, w), lambda i: (0, 0)),
                  pl.BlockSpec((ts, w), lambda i: (i, 0))],
        out_specs=(pl.BlockSpec((ts, w), lambda i: (i, 0)), pl.BlockSpec((1, w), lambda i: (0, 0))),
        compiler_params=_params(("arbitrary",)),
    )(x, g.reshape(1, w), dy)
    return dx, dg.reshape(w)


def _loss_head(x, g, target, *, name):
    s, w = x.shape
    ts = _tile(s, 512)

    def kern(x_ref, g_ref, t_ref, loss_ref, dx_ref, dg_ref):
        @pl.when(pl.program_id(0) == 0)
        def _():
            dg_ref[...] = jnp.zeros_like(dg_ref)
            loss_ref[...] = jnp.zeros_like(loss_ref)

        xv = x_ref[...]
        gv = g_ref[...]
        r = lax.rsqrt(jnp.mean(xv * xv, axis=-1, keepdims=True) + NORM_EPS)
        xhat = xv * r
        err = xhat * gv - t_ref[...]
        loss_ref[...] += 0.5 * jnp.sum(jnp.mean(err * err, axis=-1, keepdims=True))
        dyv = err * (1.0 / w)
        dg_ref[...] += jnp.sum(dyv * xhat, axis=0, keepdims=True)
        dxhat = dyv * gv
        dx_ref[...] = r * (dxhat - xhat * jnp.mean(dxhat * xhat, axis=-1, keepdims=True))

    loss, dx, dg = pl.pallas_call(
        kern, name=name,
        out_shape=(jax.ShapeDtypeStruct((1, LANES), F32), jax.ShapeDtypeStruct((s, w), F32),
                   jax.ShapeDtypeStruct((1, w), F32)),
        grid=(s // ts,),
        in_specs=[pl.BlockSpec((ts, w), lambda i: (i, 0)), pl.BlockSpec((1, w), lambda i: (0, 0)),
                  pl.BlockSpec((ts, w), lambda i: (i, 0))],
        out_specs=(pl.BlockSpec((1, LANES), lambda i: (0, 0)), pl.BlockSpec((ts, w), lambda i: (i, 0)),
                   pl.BlockSpec((1, w), lambda i: (0, 0))),
        compiler_params=_params(("arbitrary",)),
    )(x, g.reshape(1, w), target)
    return loss[0, 0], dx, dg.reshape(w)


def _kv_range(mode, qi, nk):
    if mode == "dense":
        return 0, nk
    span = 1 if mode == "band" else NA_SPAN
    return jnp.maximum(qi - span, 0), jnp.minimum(qi + span + 1, nk)


def _att_mask(mode, qi, kj, tq, tk, n):
    if mode == "dense":
        return None
    tq_pos = qi * tq + lax.broadcasted_iota(jnp.int32, (tq, 1), 0)
    tk_pos = kj * tk + lax.broadcasted_iota(jnp.int32, (1, tk), 1)
    if mode == "band":
        diff = tq_pos - tk_pos
        return (diff <= BAND_HALF) & (diff >= -BAND_HALF)
    rows = n // GRID_W
    shift = GRID_W.bit_length() - 1
    r_start = jnp.clip((tq_pos >> shift) - NA_ROWS // 2, 0, rows - NA_ROWS)
    c_start = jnp.clip((tq_pos & (GRID_W - 1)) - NA_COLS // 2, 0, GRID_W - NA_COLS)
    kr = tk_pos >> shift
    kc = tk_pos & (GRID_W - 1)
    return (kr >= r_start) & (kr < r_start + NA_ROWS) & (kc >= c_start) & (kc < c_start + NA_COLS)


def _att_fwd(q, k, v, bias, *, mode, scale, name):
    g_dim, n, dq = q.shape
    dv = v.shape[-1]
    tq = tk = _tile(n, 512) if mode == "dense" else ATT_TILE
    nk = n // tk

    def kern(*refs):
        if mode == "natten":
            q_ref, k_ref, v_ref, b_ref, o_ref, lse_ref = refs
        else:
            q_ref, k_ref, v_ref, o_ref, lse_ref = refs
        qi = pl.program_id(1)
        qb = q_ref[0]
        lo, hi = _kv_range(mode, qi, nk)

        def body(j, carry):
            m, l, acc = carry
            start = pl.multiple_of(j * tk, tk)
            kb = k_ref[0, pl.ds(start, tk), :]
            vb = v_ref[0, pl.ds(start, tk), :]
            s = lax.dot_general(qb, kb, (((1,), (1,)), ((), ())), preferred_element_type=F32) * scale
            if mode == "natten":
                s = s + b_ref[0, j - qi + NA_SPAN]
            mask = _att_mask(mode, qi, j, tq, tk, n)
            if mask is not None:
                s = jnp.where(mask, s, NEG_INF)
            m_new = jnp.maximum(m, jnp.max(s, axis=-1, keepdims=True))
            p = jnp.exp(s - m_new)
            if mask is not None:
                p = jnp.where(mask, p, 0.0)
            alpha = jnp.exp(m - m_new)
            l = alpha * l + jnp.sum(p, axis=-1, keepdims=True)
            acc = alpha * acc + jnp.dot(p.astype(BF16), vb, preferred_element_type=F32)
            return m_new, l, acc

        init = (jnp.full((tq, 1), NEG_INF, F32), jnp.zeros((tq, 1), F32), jnp.zeros((tq, dv), F32))
        m, l, acc = lax.fori_loop(lo, hi, body, init)
        o_ref[0] = acc / l
        lse_ref[0] = m + jnp.log(l)

    in_specs = [pl.BlockSpec((1, tq, dq), lambda g, i: (g, i, 0)),
                pl.BlockSpec((1, n, dq), lambda g, i: (g, 0, 0)),
                pl.BlockSpec((1, n, dv), lambda g, i: (g, 0, 0))]
    operands = [q, k, v]
    if mode == "natten":
        in_specs.append(pl.BlockSpec((1,) + bias.shape[1:], lambda g, i: (g, 0, 0, 0)))
        operands.append(bias)
    return pl.pallas_call(
        kern, name=name,
        out_shape=(jax.ShapeDtypeStruct((g_dim, n, dv), F32), jax.ShapeDtypeStruct((g_dim, n, 1), F32)),
        grid=(g_dim, n // tq), in_specs=in_specs,
        out_specs=(pl.BlockSpec((1, tq, dv), lambda g, i: (g, i, 0)), pl.BlockSpec((1, tq, 1), lambda g, i: (g, i, 0))),
        compiler_params=_params(("parallel", "arbitrary")),
    )(*operands)


def _att_bwd(q, k, v, bias, o, lse, do, dlse, *, mode, scale, name):
    g_dim, n, dq = q.shape
    dv = v.shape[-1]
    tq = tk = _tile(n, 512) if mode == "dense" else ATT_TILE
    nk = n // tk

    def kern(*refs):
        if mode == "natten":
            q_ref, k_ref, v_ref, b_ref, o_ref, lse_ref, do_ref, dlse_ref, dq_ref, dk_ref, dv_ref, db_ref = refs
        else:
            q_ref, k_ref, v_ref, o_ref, lse_ref, do_ref, dlse_ref, dq_ref, dk_ref, dv_ref = refs
        qi = pl.program_id(1)

        @pl.when(qi == 0)
        def _():
            dk_ref[...] = jnp.zeros_like(dk_ref)
            dv_ref[...] = jnp.zeros_like(dv_ref)
            if mode == "natten":
                db_ref[...] = jnp.zeros_like(db_ref)

        qb = q_ref[0]
        dob = do_ref[0]
        delta = jnp.sum(dob * o_ref[0], axis=-1, keepdims=True) - dlse_ref[0]
        dob16 = dob.astype(BF16)
        lse_v = lse_ref[0]
        lo, hi = _kv_range(mode, qi, nk)

        def body(j, dq_acc):
            start = pl.multiple_of(j * tk, tk)
            kb = k_ref[0, pl.ds(start, tk), :]
            vb = v_ref[0, pl.ds(start, tk), :]
            s = lax.dot_general(qb, kb, (((1,), (1,)), ((), ())), preferred_element_type=F32) * scale
            if mode == "natten":
                s = s + b_ref[0, j - qi + NA_SPAN]
            p = jnp.exp(s - lse_v)
            mask = _att_mask(mode, qi, j, tq, tk, n)
            if mask is not None:
                p = jnp.where(mask, p, 0.0)
            dp = lax.dot_general(dob16, vb, (((1,), (1,)), ((), ())), preferred_element_type=F32)
            ds = p * (dp - delta)
            if mode == "natten":
                db_ref[0, j - qi + NA_SPAN] += ds
            ds16 = (ds * scale).astype(BF16)
            dv_ref[0, pl.ds(start, tk), :] += lax.dot_general(
                p.astype(BF16), dob16, (((0,), (0,)), ((), ())), preferred_element_type=F32)
            dk_ref[0, pl.ds(start, tk), :] += lax.dot_general(
                ds16, qb, (((0,), (0,)), ((), ())), preferred_element_type=F32)
            return dq_acc + jnp.dot(ds16, kb, preferred_element_type=F32)

        dq_ref[0] = lax.fori_loop(lo, hi, body, jnp.zeros((tq, dq), F32))

    q_spec = pl.BlockSpec((1, tq, dq), lambda g, i: (g, i, 0))
    k_spec = pl.BlockSpec((1, n, dq), lambda g, i: (g, 0, 0))
    v_spec = pl.BlockSpec((1, n, dv), lambda g, i: (g, 0, 0))
    o_spec = pl.BlockSpec((1, tq, dv), lambda g, i: (g, i, 0))
    l_spec = pl.BlockSpec((1, tq, 1), lambda g, i: (g, i, 0))
    in_specs = [q_spec, k_spec, v_spec]
    operands = [q, k, v]
    out_shape = [jax.ShapeDtypeStruct((g_dim, n, dq), F32), jax.ShapeDtypeStruct((g_dim, n, dq), F32),
                 jax.ShapeDtypeStruct((g_dim, n, dv), F32)]
    out_specs = [q_spec, k_spec, v_spec]
    if mode == "natten":
        b_spec = pl.BlockSpec((1,) + bias.shape[1:], lambda g, i: (g, 0, 0, 0))
        in_specs.append(b_spec)
        operands.append(bias)
        out_shape.append(jax.ShapeDtypeStruct(bias.shape, F32))
        out_specs.append(b_spec)
    in_specs += [o_spec, l_spec, o_spec, l_spec]
    operands += [o, lse, do, dlse]
    return pl.pallas_call(
        kern, name=name, out_shape=tuple(out_shape), grid=(g_dim, n // tq), in_specs=in_specs,
        out_specs=tuple(out_specs), compiler_params=_params(("parallel", "arbitrary")),
    )(*operands)


def _make_attention(mode, scale, name):
    @jax.custom_vjp
    def att(q, k, v, bias):
        return att_fwd(q, k, v, bias)[0]

    def att_fwd(q, k, v, bias):
        q16, k16, v16 = q.astype(BF16), k.astype(BF16), v.astype(BF16)
        o, lse = _att_fwd(q16, k16, v16, bias, mode=mode, scale=scale, name=name + "_fwd")
        return (o, lse[..., 0]), (q16, k16, v16, bias, o, lse)

    def att_bwd(res, cts):
        q16, k16, v16, bias, o, lse = res
        do, dlse = cts
        outs = _att_bwd(q16, k16, v16, bias, o, lse, do, dlse[..., None], mode=mode, scale=scale, name=name + "_bwd")
        dbias = outs[3] if mode == "natten" else None
        return outs[0], outs[1], outs[2], dbias

    att.defvjp(att_fwd, att_bwd)
    return att


def _na_onehots():
    span = 2 * NA_SPAN + 1
    rows_per_tile = ATT_TILE // GRID_W
    e_r = np.zeros((span, rows_per_tile, rows_per_tile, 2 * NA_ROWS - 1), np.float32)
    for d in range(span):
        for qr in range(rows_per_tile):
            for kr in range(rows_per_tile):
                a = rows_per_tile * (d - NA_SPAN) + kr - qr + NA_ROWS - 1
                if 0 <= a < 2 * NA_ROWS - 1:
                    e_r[d, qr, kr, a] = 1.0
    e_c = np.zeros((GRID_W, GRID_W, 2 * NA_COLS - 1), np.float32)
    for qc in range(GRID_W):
        for kc in range(GRID_W):
            b = kc - qc + NA_COLS - 1
            if 0 <= b < 2 * NA_COLS - 1:
                e_c[qc, kc, b] = 1.0
    return e_r, e_c


def _na_bias_tiles(rpb):
    e_r, e_c = _na_onehots()
    t = jnp.einsum("dikA,xyB,hAB->hdixky", e_r, e_c, rpb, precision=lax.Precision.HIGHEST)
    return t.reshape(rpb.shape[0], 2 * NA_SPAN + 1, ATT_TILE, ATT_TILE)


def _make_norm_linear(name):
    @jax.custom_vjp
    def f(x, g, w):
        return f_fwd(x, g, w)[0]

    def f_fwd(x, g, w):
        w = w.astype(BF16)
        h = _norm_fwd(x, g, out_dtype=BF16, name=name + "_norm")
        return _mm(h, w, name=name + "_mm"), (x, g, w, h)

    def f_bwd(res, dy):
        x, g, w, h = res
        dh = _mm(dy, w, tb=True, name=name + "_dh")
        dw = _mm(h, dy, ta=True, name=name + "_dw")
        dx, dg = _norm_bwd(x, g, dh, name=name + "_dnorm")
        return dx, dg, dw

    f.defvjp(f_fwd, f_bwd)
    return f


def _make_mix_out(name):
    @jax.custom_vjp
    def f(x, oa, ob, oc, ga, gb, gc, w):
        return f_fwd(x, oa, ob, oc, ga, gb, gc, w)[0]

    def f_fwd(x, oa, ob, oc, ga, gb, gc, w):
        mixed = jnp.concatenate([
            _norm_fwd(oa, ga, out_dtype=BF16, name=name + "_norm_a"),
            _norm_fwd(ob, gb, out_dtype=BF16, name=name + "_norm_b"),
            _norm_fwd(oc, gc, out_dtype=BF16, name=name + "_norm_c")], axis=-1)
        w = w.astype(BF16)
        y = _mm(mixed, w, epilogue="add", extra=x, name=name + "_mm")
        return y, (oa, ob, oc, ga, gb, gc, w, mixed)

    def f_bwd(res, dy):
        oa, ob, oc, ga, gb, gc, w, mixed = res
        dmixed = _mm(dy, w, tb=True, name=name + "_dmixed")
        dw = _mm(mixed, dy, ta=True, name=name + "_dw")
        doa, dga = _norm_bwd(oa, ga, dmixed[:, :WIDTH_A], name=name + "_dnorm_a")
        dob, dgb = _norm_bwd(ob, gb, dmixed[:, WIDTH_A:WIDTH_A + WIDTH_B], name=name + "_dnorm_b")
        doc, dgc = _norm_bwd(oc, gc, dmixed[:, WIDTH_A + WIDTH_B:], name=name + "_dnorm_c")
        return dy, doa, dob, doc, dga, dgb, dgc, dw

    f.defvjp(f_fwd, f_bwd)
    return f


def _make_mlp(name):
    @jax.custom_vjp
    def f(x, g, w1, w2):
        return f_fwd(x, g, w1, w2)[0]

    def f_fwd(x, g, w1, w2):
        w1, w2 = w1.astype(BF16), w2.astype(BF16)
        h = _norm_fwd(x, g, out_dtype=BF16, name=name + "_norm")
        u, a = _mm(h, w1, epilogue="relu2", name=name + "_up")
        y = _mm(a, w2, epilogue="add", extra=x, name=name + "_down")
        return y, (x, g, w1, w2, h, u, a)

    def f_bwd(res, dy):
        x, g, w1, w2, h, u, a = res
        du = _mm(dy, w2, tb=True, epilogue="drelu2", extra=u, out_dtype=BF16, name=name + "_du")
        dw2 = _mm(a, dy, ta=True, name=name + "_dw2")
        dw1 = _mm(h, du, ta=True, name=name + "_dw1")
        dh = _mm(du, w1, tb=True, name=name + "_dh")
        dx, dg = _norm_bwd(x, g, dh, name=name + "_dnorm")
        return dx + dy, dg, dw1, dw2

    f.defvjp(f_fwd, f_bwd)
    return f


def _rope_tables(s, dim):
    half = dim // 2
    inv_freq = ROPE_THETA ** (-jnp.arange(half, dtype=F32) / half)
    ang = jnp.arange(s, dtype=F32)[:, None] * inv_freq[None, :]
    return jnp.cos(ang)[:, None, :], jnp.sin(ang)[:, None, :]


def _rope(x, cos, sin):
    half = x.shape[-1] // 2
    x1, x2 = x[..., :half], x[..., half:]
    return jnp.concatenate([x1 * cos - x2 * sin, x1 * sin + x2 * cos], axis=-1)


def _heads_major(t):
    return jnp.transpose(t, (1, 0, 2))


def _dilate(t, dil):
    s, h, d = t.shape
    return jnp.transpose(t.reshape(s // dil, dil, h, d), (1, 2, 0, 3)).reshape(dil * h, s // dil, d)


def _undilate(t, dil, h):
    gh, n, d = t.shape
    return jnp.transpose(t.reshape(dil, h, n, d), (2, 0, 1, 3)).reshape(n * dil, h, d)


def _layer(x, lw, rope32, rope64):
    s = x.shape[0]
    proj = _make_norm_linear("in")(x, lw["g_mix"], lw["w_in"])
    c_q = proj[:, :Q_LORA]
    c_kv = proj[:, Q_LORA:Q_LORA + KV_LORA]
    k_pe = proj[:, Q_LORA + KV_LORA:COLS_A]
    p_b = proj[:, COLS_A:COLS_A + COLS_B].reshape(s, 3, HEADS_B, HEAD_DIM)
    p_c = proj[:, COLS_A + COLS_B:].reshape(s, 3, HEADS_C, HEAD_DIM)

    qa = _make_norm_linear("uq")(c_q, lw["q_norm"], lw["w_uq"]).reshape(s, HEADS_A, QK_NOPE + QK_ROPE)
    kva = _make_norm_linear("ukv")(c_kv, lw["kv_norm"], lw["w_ukv"]).reshape(s, HEADS_A, QK_NOPE + V_DIM_A)
    k_pe = jnp.broadcast_to(_rope(k_pe[:, None, :], *rope32), (s, HEADS_A, QK_ROPE))
    qa = jnp.concatenate([qa[..., :QK_NOPE], _rope(qa[..., QK_NOPE:], *rope32)], axis=-1)
    ka = jnp.concatenate([kva[..., :QK_NOPE], k_pe], axis=-1)
    att_a = _make_attention("dense", (QK_NOPE + QK_ROPE) ** -0.5, "att_a")
    o_a, _ = att_a(_heads_major(qa), _heads_major(ka), _heads_major(kva[..., QK_NOPE:]), None)
    o_a = _heads_major(o_a).reshape(s, WIDTH_A)

    qb = _rope(p_b[:, 0], *rope64)
    kb = _rope(p_b[:, 1], *rope64)
    vb = p_b[:, 2]
    outs, lses = [], []
    for _, dil in DILATED_PAIRS:
        att_b = _make_attention("band", HEAD_DIM ** -0.5, "att_b%d" % dil)
        o, lse = att_b(_dilate(qb, dil), _dilate(kb, dil), _dilate(vb, dil), None)
        outs.append(_undilate(o, dil, HEADS_B))
        lses.append(_undilate(lse[..., None], dil, HEADS_B)[..., 0])
    wgt = jax.nn.softmax(jnp.stack(lses, axis=-1), axis=-1)
    o_b = jnp.sum(jnp.stack(outs, axis=-1) * wgt[:, :, None, :], axis=-1).reshape(s, WIDTH_B)

    att_c = _make_attention("natten", HEAD_DIM ** -0.5, "att_c")
    o_c, _ = att_c(_heads_major(p_c[:, 0]), _heads_major(p_c[:, 1]), _heads_major(p_c[:, 2]),
                   _na_bias_tiles(lw["rpb"]))
    o_c = _heads_major(o_c).reshape(s, WIDTH_C)

    x = _make_mix_out("out")(x, o_a, o_b, o_c, lw["out_norm_a"], lw["out_norm_b"], lw["out_norm_c"], lw["w_out"])
    return _make_mlp("mlp")(x, lw["g_mlp"], lw["w_mlp_in"], lw["w_mlp_out"])


def _place():
    return lax.axis_index("x"), lax.axis_index("y"), lax.axis_index("c")


def _all_gather(block, *, name):
    r, c_dim = block.shape

    def body(x_ref, out_ref, send_sems, recv_sems, local_sem):
        x, y, c = _place()
        me, sibling = (x, y, c), (x, y, 1 - c)
        chips = [(1 - x, y), (x, 1 - y), (1 - x, 1 - y)]

        def slot(px, py, pc):
            return out_ref.at[4 * px + 2 * py + pc]

        def copy(k, blk, to, src=None):
            return pltpu.make_async_remote_copy(
                src_ref=slot(*blk) if src is None else src, dst_ref=slot(*blk),
                send_sem=send_sems.at[k], recv_sem=recv_sems.at[k], device_id=to, device_id_type=MESH)

        mine = pltpu.make_async_copy(x_ref, slot(*me), local_sem)
        mine.start()
        first = [copy(0, me, sibling, src=x_ref)]
        first += [copy(1 + j, me, (*chip, c), src=x_ref) for j, chip in enumerate(chips)]
        for cp in first:
            cp.start()
        passed = [copy(4 + j, (*chip, c), sibling) for j, chip in enumerate(chips)]
        for j, chip in enumerate(chips):
            copy(1 + j, (*chip, c), me).wait_recv()
            passed[j].start()
        copy(0, sibling, me).wait_recv()
        for j, chip in enumerate(chips):
            copy(4 + j, (*chip, 1 - c), me).wait_recv()
        for cp in first + passed:
            cp.wait_send()
        mine.wait()

    return pl.pallas_call(
        body, name=name, out_shape=jax.ShapeDtypeStruct((N_DEV, r, c_dim), block.dtype),
        in_specs=[pl.BlockSpec(memory_space=pl.ANY)], out_specs=pl.BlockSpec(memory_space=pl.ANY),
        scratch_shapes=[pltpu.SemaphoreType.DMA((7,)), pltpu.SemaphoreType.DMA((7,)), pltpu.SemaphoreType.DMA(())],
    )(block)


def _all_to_all(chunks, *, name):
    def body(in_ref, out_ref, send_sems, recv_sems, local_sem):
        x, y, c = _place()
        me = 4 * x + 2 * y + c
        mine = pltpu.make_async_copy(in_ref.at[me], out_ref.at[me], local_sem)
        mine.start()
        peers = []
        for k in range(1, N_DEV):
            px = 1 - x if k & 4 else x
            py = 1 - y if k & 2 else y
            pc = 1 - c if k & 1 else c
            peers.append((px, py, pc))

        def copy(k, peer):
            pid = 4 * peer[0] + 2 * peer[1] + peer[2]
            return pltpu.make_async_remote_copy(
                src_ref=in_ref.at[pid], dst_ref=out_ref.at[me], send_sem=send_sems.at[k], recv_sem=recv_sems.at[k],
                device_id=peer, device_id_type=MESH)

        def landing(k, peer):
            pid = 4 * peer[0] + 2 * peer[1] + peer[2]
            return pltpu.make_async_remote_copy(
                src_ref=in_ref.at[pid], dst_ref=out_ref.at[pid], send_sem=send_sems.at[k], recv_sem=recv_sems.at[k],
                device_id=peer, device_id_type=MESH)

        sends = [copy(k, peer) for k, peer in enumerate(peers)]
        for cp in sends:
            cp.start()
        for k, peer in enumerate(peers):
            landing(k, peer).wait_recv()
        for cp in sends:
            cp.wait_send()
        mine.wait()

    return pl.pallas_call(
        body, name=name, out_shape=jax.ShapeDtypeStruct(chunks.shape, chunks.dtype),
        in_specs=[pl.BlockSpec(memory_space=pl.ANY)], out_specs=pl.BlockSpec(memory_space=pl.ANY),
        scratch_shapes=[pltpu.SemaphoreType.DMA((7,)), pltpu.SemaphoreType.DMA((7,)), pltpu.SemaphoreType.DMA(())],
    )(chunks)


def _adamw(parts, w, m, v, *, name):
    r, c_dim = w.shape
    tr = r
    for cand in range(min(r, 128), 7, -8):
        if r % cand == 0:
            tr = cand
            break

    def kern(p_ref, w_ref, m_ref, v_ref, g_ref, d_ref, nm_ref, nv_ref):
        g = p_ref[0].astype(F32)
        for i in range(1, N_DEV):
            g = g + p_ref[i].astype(F32)
        nm = ADAM_B1 * m_ref[...] + (1.0 - ADAM_B1) * g
        nv = ADAM_B2 * v_ref[...] + (1.0 - ADAM_B2) * (g * g)
        m_hat = nm / (1.0 - ADAM_B1 ** ADAM_STEP)
        v_hat = nv / (1.0 - ADAM_B2 ** ADAM_STEP)
        g_ref[...] = g
        d_ref[...] = -ADAM_LR * (m_hat / (jnp.sqrt(v_hat) + ADAM_EPS) + ADAM_WD * w_ref[...])
        nm_ref[...] = nm
        nv_ref[...] = nv

    row = pl.BlockSpec((tr, c_dim), lambda i: (i, 0))
    return pl.pallas_call(
        kern, name=name, out_shape=tuple(jax.ShapeDtypeStruct((r, c_dim), F32) for _ in range(4)), grid=(r // tr,),
        in_specs=[pl.BlockSpec((N_DEV, tr, c_dim), lambda i: (0, i, 0)), row, row, row],
        out_specs=(row, row, row, row), compiler_params=_params(("parallel",)),
    )(parts, w, m, v)


SHARDED = (("w_in", 1), ("w_out", 0), ("w_mlp_in", 1), ("w_mlp_out", 0), ("w_ukv", 1), ("w_uq", 1))
REPLICATED = ("g_mix", "q_norm", "kv_norm", "rpb", "out_norm_a", "out_norm_b", "out_norm_c", "g_mlp", "g_final")
WEIGHTS = ("g_mix", "w_in", "q_norm", "w_uq", "kv_norm", "w_ukv", "rpb", "out_norm_a", "out_norm_b", "out_norm_c",
           "w_out", "g_mlp", "w_mlp_in", "w_mlp_out", "g_final")
PACK_ROW_ALIGN = 16


def _pack_rows(arrs, dtype):
    parts = [a.reshape(-1, PACK_COLS).astype(dtype) for a in arrs]
    rows = sum(p.shape[0] for p in parts)
    pad = -rows % PACK_ROW_ALIGN
    if pad:
        parts.append(jnp.zeros((pad, PACK_COLS), dtype))
    return jnp.concatenate(parts, axis=0)


def _unpack_rows(packed, shapes):
    out, off = [], 0
    for shp in shapes:
        rows = int(np.prod(shp)) // PACK_COLS
        out.append(packed[..., off:off + rows, :].reshape(packed.shape[:-2] + tuple(shp)))
        off += rows
    return out


def _pack_flat(arrs):
    flat = jnp.concatenate([a.reshape(-1) for a in arrs])
    pad = -flat.shape[0] % (8 * PACK_COLS)
    return jnp.pad(flat, (0, pad)).reshape(-1, PACK_COLS)


def _unpack_flat(packed, shapes):
    flat = packed.reshape(-1)
    out, off = [], 0
    for shp in shapes:
        size = int(np.prod(shp))
        out.append(flat[off:off + size].reshape(shp))
        off += size
    return out


def _gathered_to_full(g, axis):
    n, l, a, b = g.shape
    if axis == 0:
        return jnp.transpose(g, (1, 0, 2, 3)).reshape(l, n * a, b)
    return jnp.transpose(g, (1, 2, 0, 3)).reshape(l, a, n * b)


def _full_to_chunks(w, axis):
    l, a, b = w.shape
    if axis == 0:
        return jnp.transpose(w.reshape(l, N_DEV, a // N_DEV, b), (1, 0, 2, 3))
    return jnp.transpose(w.reshape(l, a, N_DEV, b // N_DEV), (2, 0, 1, 3))


def kernel(x, g_mix, w_in, q_norm, w_uq, kv_norm, w_ukv, rpb, out_norm_a, out_norm_b, out_norm_c, w_out, g_mlp, w_mlp_in, w_mlp_out, g_final, loss_target, m_g_mix, m_w_in, m_q_norm, m_w_uq, m_kv_norm, m_w_ukv, m_rpb, m_out_norm_a, m_out_norm_b, m_out_norm_c, m_w_out, m_g_mlp, m_w_mlp_in, m_w_mlp_out, m_g_final, v_g_mix, v_w_in, v_q_norm, v_w_uq, v_kv_norm, v_w_ukv, v_rpb, v_out_norm_a, v_out_norm_b, v_out_norm_c, v_w_out, v_g_mlp, v_w_mlp_in, v_w_mlp_out, v_g_final):
    given = dict(locals())
    w = {n: given[n] for n in WEIGHTS}
    m = {n: given["m_" + n] for n in WEIGHTS}
    v = {n: given["v_" + n] for n in WEIGHTS}
    s = x.shape[1]
    depth = g_mix.shape[0]
    shard_shapes = [w[n].shape for n, _ in SHARDED]

    gathered = _all_gather(_pack_rows([w[n] for n, _ in SHARDED], BF16), name="gather_weights")
    full = {}
    for (n, axis), g in zip(SHARDED, _unpack_rows(gathered, shard_shapes)):
        full[n] = _gathered_to_full(g, axis).astype(F32)

    layer_w = [dict({n: full[n][l] for n, _ in SHARDED}, **{n: w[n][l] for n in REPLICATED if n != "g_final"})
               for l in range(depth)]
    rope32 = _rope_tables(s, QK_ROPE)
    rope64 = _rope_tables(s, HEAD_DIM)

    def trunk(x0, lws):
        h = x0
        for lw in lws:
            h = _layer(h, lw, rope32, rope64)
        return h

    x_out, pullback = jax.vjp(trunk, x[0], layer_w)
    loss_local, dx_out, d_g_final = _loss_head(x_out, g_final, loss_target[0], name="loss_head")
    dx0, d_layers = pullback(dx_out)
    loss = lax.psum(loss_local, AXES)

    grads_local = {n: jnp.stack([d[n] for d in d_layers]) for n in layer_w[0]}
    grads_local["g_final"] = d_g_final

    chunks = [_full_to_chunks(grads_local[n], axis).reshape(N_DEV, -1, PACK_COLS) for n, axis in SHARDED]
    rows = sum(c.shape[1] for c in chunks)
    pad = -rows % PACK_ROW_ALIGN
    if pad:
        chunks.append(jnp.zeros((N_DEV, pad, PACK_COLS), F32))
    parts = _all_to_all(jnp.concatenate(chunks, axis=1), name="scatter_grads")
    big = _adamw(parts, _pack_rows([w[n] for n, _ in SHARDED], F32), _pack_rows([m[n] for n, _ in SHARDED], F32),
                 _pack_rows([v[n] for n, _ in SHARDED], F32), name="adamw_sharded")
    big = [_unpack_rows(o, shard_shapes) for o in big]

    rep_shapes = [w[n].shape for n in REPLICATED]
    rep_parts = _all_gather(_pack_flat([grads_local[n] for n in REPLICATED]), name="gather_small_grads")
    small = _adamw(rep_parts, _pack_flat([w[n] for n in REPLICATED]), _pack_flat([m[n] for n in REPLICATED]),
                   _pack_flat([v[n] for n in REPLICATED]), name="adamw_replicated")
    small = [_unpack_flat(o, rep_shapes) for o in small]

    result = {}
    for kind, idx in (("grad", 0), ("delta", 1), ("new_m", 2), ("new_v", 3)):
        for j, (n, _) in enumerate(SHARDED):
            result[kind + "_" + n] = big[idx][j]
        for j, n in enumerate(REPLICATED):
            result[kind + "_" + n] = small[idx][j]
    outs = [loss, dx0[None]]
    for kind in ("grad", "delta", "new_m", "new_v"):
        outs += [result[kind + "_" + n] for n in WEIGHTS]
    return tuple(outs)
```

```python
import functools

import numpy as np
import jax
import jax.numpy as jnp
from jax import lax
from jax.experimental import pallas as pl
from jax.experimental.pallas import tpu as pltpu

F32 = jnp.float32
BF16 = jnp.bfloat16

D_MODEL = 1024
DEPTH = 4
HEAD_DIM = 64
HEADS_A = 6
HEADS_B = 6
HEADS_C = 4
Q_LORA = 256
KV_LORA = 128
QK_NOPE = 64
QK_ROPE = 32
V_DIM_A = 64
DILATED_PAIRS = ((128, 1), (512, 4), (2048, 16))
BAND_HALF = 64
GRID_W = 64
NA_ROWS = 8
NA_COLS = 16
D_FF = 4096
ROPE_THETA = 10000.0
NORM_EPS = 1e-6
NEG_INF = -1e30

COLS_A = Q_LORA + KV_LORA + QK_ROPE
COLS_B = 3 * HEADS_B * HEAD_DIM
COLS_C = 3 * HEADS_C * HEAD_DIM
WIDTH_A = HEADS_A * V_DIM_A
WIDTH_B = HEADS_B * HEAD_DIM
WIDTH_C = HEADS_C * HEAD_DIM

ADAM_LR = 0.001
ADAM_B1 = 0.9
ADAM_B2 = 0.999
ADAM_EPS = 1e-08
ADAM_WD = 0.01
ADAM_STEP = 10

N_DEV = 8
AXES = ("x", "y", "c")
MESH = pl.DeviceIdType.MESH

V7X_VMEM_LIMIT = 48 * 1024 * 1024
MM_VMEM_BUDGET = 32 * 1024 * 1024
LANES = 128
PACK_COLS = 1024
PACK_ROW_ALIGN = 16
ADAM_MAX_ROWS = 160
BAND_TILE = 256
NA_TILE_ROWS = 4


def _params(sem=None):
    return pltpu.CompilerParams(dimension_semantics=sem, vmem_limit_bytes=V7X_VMEM_LIMIT)


def _tile(dim, pref):
    if dim <= pref:
        return dim
    for t in range(pref - pref % LANES, LANES - 1, -LANES):
        if dim % t == 0:
            return t
    return dim


def _mm(a, b, *, ta=False, tb=False, out_dtype=F32, epilogue=None, extra=None, name):
    if ta:
        k_dim, m_dim = a.shape
    else:
        m_dim, k_dim = a.shape
    n_dim = b.shape[0] if tb else b.shape[1]
    tn = _tile(n_dim, 1024) if n_dim % LANES == 0 else n_dim
    tk = _tile(k_dim, 1024 if not ta else 512) if k_dim % LANES == 0 else k_dim
    n_out = 2 if epilogue == "relu2" else 1
    for tm in (_tile(m_dim, 1024), _tile(m_dim, 512), _tile(m_dim, 256)):
        blocks = (tm * tk * a.dtype.itemsize + tk * tn * b.dtype.itemsize
                  + (tm * tn * 4 if extra is not None else 0) + n_out * tm * tn * 4)
        if 2 * blocks + tm * tn * 4 <= MM_VMEM_BUDGET:
            break
    nk = k_dim // tk
    a_spec = (pl.BlockSpec((tk, tm), lambda i, j, k: (k, i)) if ta
              else pl.BlockSpec((tm, tk), lambda i, j, k: (i, k)))
    b_spec = (pl.BlockSpec((tn, tk), lambda i, j, k: (j, k)) if tb
              else pl.BlockSpec((tk, tn), lambda i, j, k: (k, j)))
    o_spec = pl.BlockSpec((tm, tn), lambda i, j, k: (i, j))
    dims = (((0 if ta else 1,), (1 if tb else 0,)), ((), ()))
    in_specs = [a_spec, b_spec]
    operands = [a, b]
    if epilogue in ("drelu2", "add"):
        in_specs.append(o_spec)
        operands.append(extra)
    if epilogue == "relu2":
        out_shape = (jax.ShapeDtypeStruct((m_dim, n_dim), F32), jax.ShapeDtypeStruct((m_dim, n_dim), BF16))
        out_specs = (o_spec, o_spec)
    else:
        out_shape = jax.ShapeDtypeStruct((m_dim, n_dim), out_dtype)
        out_specs = o_spec

    def kern(*refs):
        acc_ref = refs[-1]
        a_ref, b_ref = refs[0], refs[1]
        k = pl.program_id(2)

        @pl.when(k == 0)
        def _():
            acc_ref[...] = jnp.zeros_like(acc_ref)

        acc_ref[...] += lax.dot_general(a_ref[...].astype(BF16), b_ref[...].astype(BF16), dims,
                                        preferred_element_type=F32)

        @pl.when(k == nk - 1)
        def _():
            acc = acc_ref[...]
            if epilogue == "relu2":
                refs[2][...] = acc
                r = jnp.maximum(acc, 0.0)
                refs[3][...] = (r * r).astype(BF16)
            elif epilogue == "drelu2":
                refs[3][...] = (acc * (2.0 * jnp.maximum(refs[2][...], 0.0))).astype(out_dtype)
            elif epilogue == "add":
                refs[3][...] = (acc + refs[2][...]).astype(out_dtype)
            else:
                refs[2][...] = acc.astype(out_dtype)

    return pl.pallas_call(
        kern, name=name, out_shape=out_shape, grid=(m_dim // tm, n_dim // tn, nk),
        in_specs=in_specs, out_specs=out_specs, scratch_shapes=[pltpu.VMEM((tm, tn), F32)],
        compiler_params=_params(("parallel", "parallel", "arbitrary")),
    )(*operands)


def _norm_fwd(x, g, *, out_dtype, name):
    s, w = x.shape
    ts = _tile(s, 512)

    def kern(x_ref, g_ref, y_ref):
        xv = x_ref[...]
        r = lax.rsqrt(jnp.mean(xv * xv, axis=-1, keepdims=True) + NORM_EPS)
        y_ref[...] = (xv * r * g_ref[...]).astype(out_dtype)

    return pl.pallas_call(
        kern, name=name, out_shape=jax.ShapeDtypeStruct((s, w), out_dtype), grid=(s // ts,),
        in_specs=[pl.BlockSpec((ts, w), lambda i: (i, 0)), pl.BlockSpec((1, w), lambda i: (0, 0))],
        out_specs=pl.BlockSpec((ts, w), lambda i: (i, 0)),
        compiler_params=_params(("parallel",)),
    )(x, g.reshape(1, w))


def _norm_bwd(x, g, dy, *, name):
    s, w = x.shape
    ts = _tile(s, 512)

    def kern(x_ref, g_ref, dy_ref, dx_ref, dg_ref):
        @pl.when(pl.program_id(0) == 0)
        def _():
            dg_ref[...] = jnp.zeros_like(dg_ref)

        xv = x_ref[...]
        dyv = dy_ref[...]
        r = lax.rsqrt(jnp.mean(xv * xv, axis=-1, keepdims=True) + NORM_EPS)
        xhat = xv * r
        dg_ref[...] += jnp.sum(dyv * xhat, axis=0, keepdims=True)
        dxhat = dyv * g_ref[...]
        dx_ref[...] = r * (dxhat - xhat * jnp.mean(dxhat * xhat, axis=-1, keepdims=True))

    dx, dg = pl.pallas_call(
        kern, name=name,
        out_shape=(jax.ShapeDtypeStruct((s, w), F32), jax.ShapeDtypeStruct((1, w), F32)), grid=(s // ts,),
        in_specs=[pl.BlockSpec((ts, w), lambda i: (i, 0)), pl.BlockSpec((1, w), lambda i: (0, 0)),
                  pl.BlockSpec((ts, w), lambda i: (i, 0))],
        out_specs=(pl.BlockSpec((ts, w), lambda i: (i, 0)), pl.BlockSpec((1, w), lambda i: (0, 0))),
        compiler_params=_params(("arbitrary",)),
    )(x, g.reshape(1, w), dy)
    return dx, dg.reshape(w)


def _loss_head(x, g, target, *, name):
    s, w = x.shape
    ts = _tile(s, 512)

    def kern(x_ref, g_ref, t_ref, loss_ref, dx_ref, dg_ref):
        @pl.when(pl.program_id(0) == 0)
        def _():
            dg_ref[...] = jnp.zeros_like(dg_ref)
            loss_ref[...] = jnp.zeros_like(loss_ref)

        xv = x_ref[...]
        gv = g_ref[...]
        r = lax.rsqrt(jnp.mean(xv * xv, axis=-1, keepdims=True) + NORM_EPS)
        xhat = xv * r
        err = xhat * gv - t_ref[...]
        loss_ref[...] += 0.5 * jnp.sum(jnp.mean(err * err, axis=-1, keepdims=True))
        dyv = err * (1.0 / w)
        dg_ref[...] += jnp.sum(dyv * xhat, axis=0, keepdims=True)
        dxhat = dyv * gv
        dx_ref[...] = r * (dxhat - xhat * jnp.mean(dxhat * xhat, axis=-1, keepdims=True))

    loss, dx, dg = pl.pallas_call(
        kern, name=name,
        out_shape=(jax.ShapeDtypeStruct((1, LANES), F32), jax.ShapeDtypeStruct((s, w), F32),
                   jax.ShapeDtypeStruct((1, w), F32)),
        grid=(s // ts,),
        in_specs=[pl.BlockSpec((ts, w), lambda i: (i, 0)), pl.BlockSpec((1, w), lambda i: (0, 0)),
                  pl.BlockSpec((ts, w), lambda i: (i, 0))],
        out_specs=(pl.BlockSpec((1, LANES), lambda i: (0, 0)), pl.BlockSpec((ts, w), lambda i: (i, 0)),
                   pl.BlockSpec((1, w), lambda i: (0, 0))),
        compiler_params=_params(("arbitrary",)),
    )(x, g.reshape(1, w), target)
    return loss[0, 0], dx, dg.reshape(w)


def _dense_fwd(q, k, v, *, scale, name):
    g_dim, n, dq = q.shape
    dv = v.shape[-1]
    tq = tk = _tile(n, 512)
    nk = n // tk

    def kern(q_ref, k_ref, v_ref, o_ref, lse_ref):
        qb = q_ref[0]

        def body(j, carry):
            m, l, acc = carry
            start = pl.multiple_of(j * tk, tk)
            kb = k_ref[0, pl.ds(start, tk), :]
            vb = v_ref[0, pl.ds(start, tk), :]
            s = lax.dot_general(qb, kb, (((1,), (1,)), ((), ())), preferred_element_type=F32) * scale
            m_new = jnp.maximum(m, jnp.max(s, axis=-1, keepdims=True))
            p = jnp.exp(s - m_new)
            alpha = jnp.exp(m - m_new)
            l = alpha * l + jnp.sum(p, axis=-1, keepdims=True)
            acc = alpha * acc + jnp.dot(p.astype(BF16), vb, preferred_element_type=F32)
            return m_new, l, acc

        init = (jnp.full((tq, 1), NEG_INF, F32), jnp.zeros((tq, 1), F32), jnp.zeros((tq, dv), F32))
        m, l, acc = lax.fori_loop(0, nk, body, init)
        o_ref[0] = acc / l
        lse_ref[0] = m + jnp.log(l)

    return pl.pallas_call(
        kern, name=name,
        out_shape=(jax.ShapeDtypeStruct((g_dim, n, dv), F32), jax.ShapeDtypeStruct((g_dim, n, 1), F32)),
        grid=(g_dim, n // tq),
        in_specs=[pl.BlockSpec((1, tq, dq), lambda g, i: (g, i, 0)), pl.BlockSpec((1, n, dq), lambda g, i: (g, 0, 0)),
                  pl.BlockSpec((1, n, dv), lambda g, i: (g, 0, 0))],
        out_specs=(pl.BlockSpec((1, tq, dv), lambda g, i: (g, i, 0)), pl.BlockSpec((1, tq, 1), lambda g, i: (g, i, 0))),
        compiler_params=_params(("parallel", "arbitrary")),
    )(q, k, v)


def _dense_bwd(q, k, v, o, lse, do, dlse, *, scale, name):
    g_dim, n, dq = q.shape
    dv = v.shape[-1]
    tq = tk = _tile(n, 512)
    nk = n // tk

    def kern(q_ref, k_ref, v_ref, o_ref, lse_ref, do_ref, dlse_ref, dq_ref, dk_ref, dv_ref):
        @pl.when(pl.program_id(1) == 0)
        def _():
            dk_ref[...] = jnp.zeros_like(dk_ref)
            dv_ref[...] = jnp.zeros_like(dv_ref)

        qb = q_ref[0]
        dob = do_ref[0]
        delta = jnp.sum(dob * o_ref[0], axis=-1, keepdims=True) - dlse_ref[0]
        dob16 = dob.astype(BF16)
        lse_v = lse_ref[0]

        def body(j, dq_acc):
            start = pl.multiple_of(j * tk, tk)
            kb = k_ref[0, pl.ds(start, tk), :]
            vb = v_ref[0, pl.ds(start, tk), :]
            s = lax.dot_general(qb, kb, (((1,), (1,)), ((), ())), preferred_element_type=F32) * scale
            p = jnp.exp(s - lse_v)
            dp = lax.dot_general(dob16, vb, (((1,), (1,)), ((), ())), preferred_element_type=F32)
            ds16 = (p * (dp - delta) * scale).astype(BF16)
            dv_ref[0, pl.ds(start, tk), :] += lax.dot_general(
                p.astype(BF16), dob16, (((0,), (0,)), ((), ())), preferred_element_type=F32)
            dk_ref[0, pl.ds(start, tk), :] += lax.dot_general(
                ds16, qb, (((0,), (0,)), ((), ())), preferred_element_type=F32)
            return dq_acc + jnp.dot(ds16, kb, preferred_element_type=F32)

        dq_ref[0] = lax.fori_loop(0, nk, body, jnp.zeros((tq, dq), F32))

    q_spec = pl.BlockSpec((1, tq, dq), lambda g, i: (g, i, 0))
    k_spec = pl.BlockSpec((1, n, dq), lambda g, i: (g, 0, 0))
    v_spec = pl.BlockSpec((1, n, dv), lambda g, i: (g, 0, 0))
    o_spec = pl.BlockSpec((1, tq, dv), lambda g, i: (g, i, 0))
    l_spec = pl.BlockSpec((1, tq, 1), lambda g, i: (g, i, 0))
    return pl.pallas_call(
        kern, name=name,
        out_shape=(jax.ShapeDtypeStruct((g_dim, n, dq), F32), jax.ShapeDtypeStruct((g_dim, n, dq), F32),
                   jax.ShapeDtypeStruct((g_dim, n, dv), F32)),
        grid=(g_dim, n // tq), in_specs=[q_spec, k_spec, v_spec, o_spec, l_spec, o_spec, l_spec],
        out_specs=(q_spec, k_spec, v_spec), compiler_params=_params(("parallel", "arbitrary")),
    )(q, k, v, o, lse, do, dlse)


def _window(mode, n):
    if mode == "band":
        tq = min(BAND_TILE, n)
        return tq, BAND_HALF, tq + 2 * BAND_HALF
    tq = NA_TILE_ROWS * GRID_W
    front = (NA_ROWS // 2) * GRID_W
    return tq, front, tq + NA_ROWS * GRID_W


def _window_mask(mode, i0, tq, front, span, n):
    q_pos = i0 + lax.broadcasted_iota(jnp.int32, (tq, 1), 0)
    k_pos = i0 - front + lax.broadcasted_iota(jnp.int32, (1, span), 1)
    if mode == "band":
        diff = k_pos - q_pos
        return (diff <= BAND_HALF) & (diff >= -BAND_HALF) & (k_pos >= 0) & (k_pos < n)
    rows = n // GRID_W
    shift = GRID_W.bit_length() - 1
    r_start = jnp.clip((q_pos >> shift) - NA_ROWS // 2, 0, rows - NA_ROWS)
    c_start = jnp.clip((q_pos & (GRID_W - 1)) - NA_COLS // 2, 0, GRID_W - NA_COLS)
    kr = k_pos >> shift
    kc = k_pos & (GRID_W - 1)
    return (kr >= r_start) & (kr < r_start + NA_ROWS) & (kc >= c_start) & (kc < c_start + NA_COLS)


def _win_fwd(q, k, v, bias, *, mode, scale, name):
    g_dim, n, d = q.shape
    tq, front, span = _window(mode, n)
    n_pad = k.shape[1]

    def kern(*refs):
        if mode == "natten":
            q_ref, k_ref, v_ref, b_ref, o_ref, lse_ref = refs
        else:
            q_ref, k_ref, v_ref, o_ref, lse_ref = refs
        i0 = pl.multiple_of(pl.program_id(1) * tq, tq)
        kb = k_ref[0, pl.ds(i0, span), :]
        vb = v_ref[0, pl.ds(i0, span), :]
        s = lax.dot_general(q_ref[0], kb, (((1,), (1,)), ((), ())), preferred_element_type=F32) * scale
        if mode == "natten":
            s = s + b_ref[0]
        s = jnp.where(_window_mask(mode, i0, tq, front, span, n), s, NEG_INF)
        m = jnp.max(s, axis=-1, keepdims=True)
        p = jnp.exp(s - m)
        l = jnp.sum(p, axis=-1, keepdims=True)
        o_ref[0] = jnp.dot(p.astype(BF16), vb, preferred_element_type=F32) / l
        lse_ref[0] = m + jnp.log(l)

    in_specs = [pl.BlockSpec((1, tq, d), lambda g, i: (g, i, 0)), pl.BlockSpec((1, n_pad, d), lambda g, i: (g, 0, 0)),
                pl.BlockSpec((1, n_pad, d), lambda g, i: (g, 0, 0))]
    operands = [q, k, v]
    if mode == "natten":
        in_specs.append(pl.BlockSpec((1, tq, span), lambda g, i: (g, 0, 0)))
        operands.append(bias)
    return pl.pallas_call(
        kern, name=name,
        out_shape=(jax.ShapeDtypeStruct((g_dim, n, d), F32), jax.ShapeDtypeStruct((g_dim, n, 1), F32)),
        grid=(g_dim, n // tq), in_specs=in_specs,
        out_specs=(pl.BlockSpec((1, tq, d), lambda g, i: (g, i, 0)), pl.BlockSpec((1, tq, 1), lambda g, i: (g, i, 0))),
        compiler_params=_params(("parallel", "arbitrary")),
    )(*operands)


def _win_bwd(q, k, v, bias, o, lse, do, dlse, *, mode, scale, name):
    g_dim, n, d = q.shape
    tq, front, span = _window(mode, n)
    n_pad = k.shape[1]

    def kern(*refs):
        if mode == "natten":
            q_ref, k_ref, v_ref, b_ref, o_ref, lse_ref, do_ref, dlse_ref, dq_ref, dk_ref, dv_ref, db_ref = refs
        else:
            q_ref, k_ref, v_ref, o_ref, lse_ref, do_ref, dlse_ref, dq_ref, dk_ref, dv_ref = refs

        @pl.when(pl.program_id(1) == 0)
        def _():
            dk_ref[...] = jnp.zeros_like(dk_ref)
            dv_ref[...] = jnp.zeros_like(dv_ref)
            if mode == "natten":
                db_ref[...] = jnp.zeros_like(db_ref)

        i0 = pl.multiple_of(pl.program_id(1) * tq, tq)
        qb = q_ref[0]
        kb = k_ref[0, pl.ds(i0, span), :]
        vb = v_ref[0, pl.ds(i0, span), :]
        dob = do_ref[0]
        delta = jnp.sum(dob * o_ref[0], axis=-1, keepdims=True) - dlse_ref[0]
        dob16 = dob.astype(BF16)
        s = lax.dot_general(qb, kb, (((1,), (1,)), ((), ())), preferred_element_type=F32) * scale
        if mode == "natten":
            s = s + b_ref[0]
        p = jnp.where(_window_mask(mode, i0, tq, front, span, n), jnp.exp(s - lse_ref[0]), 0.0)
        dp = lax.dot_general(dob16, vb, (((1,), (1,)), ((), ())), preferred_element_type=F32)
        ds = p * (dp - delta)
        if mode == "natten":
            db_ref[0] += ds
        ds16 = (ds * scale).astype(BF16)
        dv_ref[0, pl.ds(i0, span), :] += lax.dot_general(
            p.astype(BF16), dob16, (((0,), (0,)), ((), ())), preferred_element_type=F32)
        dk_ref[0, pl.ds(i0, span), :] += lax.dot_general(
            ds16, qb, (((0,), (0,)), ((), ())), preferred_element_type=F32)
        dq_ref[0] = jnp.dot(ds16, kb, preferred_element_type=F32)

    q_spec = pl.BlockSpec((1, tq, d), lambda g, i: (g, i, 0))
    k_spec = pl.BlockSpec((1, n_pad, d), lambda g, i: (g, 0, 0))
    l_spec = pl.BlockSpec((1, tq, 1), lambda g, i: (g, i, 0))
    in_specs = [q_spec, k_spec, k_spec]
    operands = [q, k, v]
    out_shape = [jax.ShapeDtypeStruct((g_dim, n, d), F32), jax.ShapeDtypeStruct((g_dim, n_pad, d), F32),
                 jax.ShapeDtypeStruct((g_dim, n_pad, d), F32)]
    out_specs = [q_spec, k_spec, k_spec]
    if mode == "natten":
        b_spec = pl.BlockSpec((1, tq, span), lambda g, i: (g, 0, 0))
        in_specs.append(b_spec)
        operands.append(bias)
        out_shape.append(jax.ShapeDtypeStruct(bias.shape, F32))
        out_specs.append(b_spec)
    in_specs += [q_spec, l_spec, q_spec, l_spec]
    operands += [o, lse, do, dlse]
    return pl.pallas_call(
        kern, name=name, out_shape=tuple(out_shape), grid=(g_dim, n // tq), in_specs=in_specs,
        out_specs=tuple(out_specs), compiler_params=_params(("parallel", "arbitrary")),
    )(*operands)


def _make_attention(mode, scale, name):
    @jax.custom_vjp
    def att(q, k, v, bias):
        return att_fwd(q, k, v, bias)[0]

    def att_fwd(q, k, v, bias):
        q16, k16, v16 = q.astype(BF16), k.astype(BF16), v.astype(BF16)
        if mode == "dense":
            o, lse = _dense_fwd(q16, k16, v16, scale=scale, name=name + "_fwd")
        else:
            o, lse = _win_fwd(q16, k16, v16, bias, mode=mode, scale=scale, name=name + "_fwd")
        return (o, lse[..., 0]), (q16, k16, v16, bias, o, lse)

    def att_bwd(res, cts):
        q16, k16, v16, bias, o, lse = res
        do, dlse = cts
        if mode == "dense":
            outs = _dense_bwd(q16, k16, v16, o, lse, do, dlse[..., None], scale=scale, name=name + "_bwd")
        else:
            outs = _win_bwd(q16, k16, v16, bias, o, lse, do, dlse[..., None], mode=mode, scale=scale,
                            name=name + "_bwd")
        dbias = outs[3] if mode == "natten" else None
        return outs[0], outs[1], outs[2], dbias

    att.defvjp(att_fwd, att_bwd)
    return att


def _pad_keys(t, mode):
    tq, front, span = _window(mode, t.shape[1])
    return jnp.pad(t, ((0, 0), (front, span - tq - front), (0, 0)))


def _na_onehots():
    tq, front, span = _window("natten", 0)
    q_rows, k_rows, front_rows = tq // GRID_W, span // GRID_W, front // GRID_W
    e_r = np.zeros((q_rows, k_rows, 2 * NA_ROWS - 1), np.float32)
    for qr in range(q_rows):
        for kr in range(k_rows):
            a = kr - front_rows - qr + NA_ROWS - 1
            if 0 <= a < 2 * NA_ROWS - 1:
                e_r[qr, kr, a] = 1.0
    e_c = np.zeros((GRID_W, GRID_W, 2 * NA_COLS - 1), np.float32)
    for qc in range(GRID_W):
        for kc in range(GRID_W):
            b = kc - qc + NA_COLS - 1
            if 0 <= b < 2 * NA_COLS - 1:
                e_c[qc, kc, b] = 1.0
    return e_r, e_c


def _na_bias_tiles(rpb):
    tq, _, span = _window("natten", 0)
    e_r, e_c = _na_onehots()
    t = jnp.einsum("ikA,xyB,hAB->hixky", e_r, e_c, rpb, precision=lax.Precision.HIGHEST)
    return t.reshape(rpb.shape[0], tq, span)


def _make_norm_linear(name):
    @jax.custom_vjp
    def f(x, g, w):
        return f_fwd(x, g, w)[0]

    def f_fwd(x, g, w):
        w = w.astype(BF16)
        h = _norm_fwd(x, g, out_dtype=BF16, name=name + "_norm")
        return _mm(h, w, name=name + "_mm"), (x, g, w, h)

    def f_bwd(res, dy):
        x, g, w, h = res
        dh = _mm(dy, w, tb=True, name=name + "_dh")
        dw = _mm(h, dy, ta=True, name=name + "_dw")
        dx, dg = _norm_bwd(x, g, dh, name=name + "_dnorm")
        return dx, dg, dw

    f.defvjp(f_fwd, f_bwd)
    return f


def _make_mix_out(name):
    @jax.custom_vjp
    def f(x, oa, ob, oc, ga, gb, gc, w):
        return f_fwd(x, oa, ob, oc, ga, gb, gc, w)[0]

    def f_fwd(x, oa, ob, oc, ga, gb, gc, w):
        mixed = jnp.concatenate([
            _norm_fwd(oa, ga, out_dtype=BF16, name=name + "_norm_a"),
            _norm_fwd(ob, gb, out_dtype=BF16, name=name + "_norm_b"),
            _norm_fwd(oc, gc, out_dtype=BF16, name=name + "_norm_c")], axis=-1)
        w = w.astype(BF16)
        y = _mm(mixed, w, epilogue="add", extra=x, name=name + "_mm")
        return y, (oa, ob, oc, ga, gb, gc, w, mixed)

    def f_bwd(res, dy):
        oa, ob, oc, ga, gb, gc, w, mixed = res
        dmixed = _mm(dy, w, tb=True, name=name + "_dmixed")
        dw = _mm(mixed, dy, ta=True, name=name + "_dw")
        doa, dga = _norm_bwd(oa, ga, dmixed[:, :WIDTH_A], name=name + "_dnorm_a")
        dob, dgb = _norm_bwd(ob, gb, dmixed[:, WIDTH_A:WIDTH_A + WIDTH_B], name=name + "_dnorm_b")
        doc, dgc = _norm_bwd(oc, gc, dmixed[:, WIDTH_A + WIDTH_B:], name=name + "_dnorm_c")
        return dy, doa, dob, doc, dga, dgb, dgc, dw

    f.defvjp(f_fwd, f_bwd)
    return f


def _make_mlp(name):
    @jax.custom_vjp
    def f(x, g, w1, w2):
        return f_fwd(x, g, w1, w2)[0]

    def f_fwd(x, g, w1, w2):
        w1, w2 = w1.astype(BF16), w2.astype(BF16)
        h = _norm_fwd(x, g, out_dtype=BF16, name=name + "_norm")
        u, a = _mm(h, w1, epilogue="relu2", name=name + "_up")
        y = _mm(a, w2, epilogue="add", extra=x, name=name + "_down")
        return y, (x, g, w1, w2, h, u, a)

    def f_bwd(res, dy):
        x, g, w1, w2, h, u, a = res
        du = _mm(dy, w2, tb=True, epilogue="drelu2", extra=u, out_dtype=BF16, name=name + "_du")
        dw2 = _mm(a, dy, ta=True, name=name + "_dw2")
        dw1 = _mm(h, du, ta=True, name=name + "_dw1")
        dh = _mm(du, w1, tb=True, name=name + "_dh")
        dx, dg = _norm_bwd(x, g, dh, name=name + "_dnorm")
        return dx + dy, dg, dw1, dw2

    f.defvjp(f_fwd, f_bwd)
    return f


def _rope_tables(s, dim):
    half = dim // 2
    inv_freq = ROPE_THETA ** (-jnp.arange(half, dtype=F32) / half)
    ang = jnp.arange(s, dtype=F32)[:, None] * inv_freq[None, :]
    return jnp.cos(ang)[:, None, :], jnp.sin(ang)[:, None, :]


def _rope(x, cos, sin):
    half = x.shape[-1] // 2
    x1, x2 = x[..., :half], x[..., half:]
    return jnp.concatenate([x1 * cos - x2 * sin, x1 * sin + x2 * cos], axis=-1)


def _heads_major(t):
    return jnp.transpose(t, (1, 0, 2))


def _dilate(t, dil):
    s, h, d = t.shape
    return jnp.transpose(t.reshape(s // dil, dil, h, d), (1, 2, 0, 3)).reshape(dil * h, s // dil, d)


def _undilate(t, dil, h):
    gh, n, d = t.shape
    return jnp.transpose(t.reshape(dil, h, n, d), (2, 0, 1, 3)).reshape(n * dil, h, d)


def _layer(x, lw, rope32, rope64):
    s = x.shape[0]
    proj = _make_norm_linear("in")(x, lw["g_mix"], lw["w_in"])
    c_q = proj[:, :Q_LORA]
    c_kv = proj[:, Q_LORA:Q_LORA + KV_LORA]
    k_pe = proj[:, Q_LORA + KV_LORA:COLS_A]
    p_b = proj[:, COLS_A:COLS_A + COLS_B].reshape(s, 3, HEADS_B, HEAD_DIM)
    p_c = proj[:, COLS_A + COLS_B:].reshape(s, 3, HEADS_C, HEAD_DIM)

    qa = _make_norm_linear("uq")(c_q, lw["q_norm"], lw["w_uq"]).reshape(s, HEADS_A, QK_NOPE + QK_ROPE)
    kva = _make_norm_linear("ukv")(c_kv, lw["kv_norm"], lw["w_ukv"]).reshape(s, HEADS_A, QK_NOPE + V_DIM_A)
    k_pe = jnp.broadcast_to(_rope(k_pe[:, None, :], *rope32), (s, HEADS_A, QK_ROPE))
    qa = jnp.concatenate([qa[..., :QK_NOPE], _rope(qa[..., QK_NOPE:], *rope32)], axis=-1)
    ka = jnp.concatenate([kva[..., :QK_NOPE], k_pe], axis=-1)
    att_a = _make_attention("dense", (QK_NOPE + QK_ROPE) ** -0.5, "att_a")
    o_a, _ = att_a(_heads_major(qa), _heads_major(ka), _heads_major(kva[..., QK_NOPE:]), None)
    o_a = _heads_major(o_a).reshape(s, WIDTH_A)

    qb = _rope(p_b[:, 0], *rope64)
    kb = _rope(p_b[:, 1], *rope64)
    vb = p_b[:, 2]
    outs, lses = [], []
    for _, dil in DILATED_PAIRS:
        att_b = _make_attention("band", HEAD_DIM ** -0.5, "att_b%d" % dil)
        o, lse = att_b(_dilate(qb, dil), _pad_keys(_dilate(kb, dil), "band"), _pad_keys(_dilate(vb, dil), "band"), None)
        outs.append(_undilate(o, dil, HEADS_B))
        lses.append(_undilate(lse[..., None], dil, HEADS_B)[..., 0])
    wgt = jax.nn.softmax(jnp.stack(lses, axis=-1), axis=-1)
    o_b = jnp.sum(jnp.stack(outs, axis=-1) * wgt[:, :, None, :], axis=-1).reshape(s, WIDTH_B)

    att_c = _make_attention("natten", HEAD_DIM ** -0.5, "att_c")
    o_c, _ = att_c(_heads_major(p_c[:, 0]), _pad_keys(_heads_major(p_c[:, 1]), "natten"),
                   _pad_keys(_heads_major(p_c[:, 2]), "natten"), _na_bias_tiles(lw["rpb"]))
    o_c = _heads_major(o_c).reshape(s, WIDTH_C)

    x = _make_mix_out("out")(x, o_a, o_b, o_c, lw["out_norm_a"], lw["out_norm_b"], lw["out_norm_c"], lw["w_out"])
    return _make_mlp("mlp")(x, lw["g_mlp"], lw["w_mlp_in"], lw["w_mlp_out"])


def _place():
    return lax.axis_index("x"), lax.axis_index("y"), lax.axis_index("c")


def _all_gather(block, *, name):
    r, c_dim = block.shape

    def body(x_ref, out_ref, send_sems, recv_sems, local_sem):
        x, y, c = _place()
        me, sibling = (x, y, c), (x, y, 1 - c)
        chips = [(1 - x, y), (x, 1 - y), (1 - x, 1 - y)]

        def slot(px, py, pc):
            return out_ref.at[4 * px + 2 * py + pc]

        def copy(k, blk, to, src=None):
            return pltpu.make_async_remote_copy(
                src_ref=slot(*blk) if src is None else src, dst_ref=slot(*blk),
                send_sem=send_sems.at[k], recv_sem=recv_sems.at[k], device_id=to, device_id_type=MESH)

        mine = pltpu.make_async_copy(x_ref, slot(*me), local_sem)
        mine.start()
        first = [copy(0, me, sibling, src=x_ref)]
        first += [copy(1 + j, me, (*chip, c), src=x_ref) for j, chip in enumerate(chips)]
        for cp in first:
            cp.start()
        passed = [copy(4 + j, (*chip, c), sibling) for j, chip in enumerate(chips)]
        for j, chip in enumerate(chips):
            copy(1 + j, (*chip, c), me).wait_recv()
            passed[j].start()
        copy(0, sibling, me).wait_recv()
        for j, chip in enumerate(chips):
            copy(4 + j, (*chip, 1 - c), me).wait_recv()
        for cp in first + passed:
            cp.wait_send()
        mine.wait()

    return pl.pallas_call(
        body, name=name, out_shape=jax.ShapeDtypeStruct((N_DEV, r, c_dim), block.dtype),
        in_specs=[pl.BlockSpec(memory_space=pl.ANY)], out_specs=pl.BlockSpec(memory_space=pl.ANY),
        scratch_shapes=[pltpu.SemaphoreType.DMA((7,)), pltpu.SemaphoreType.DMA((7,)), pltpu.SemaphoreType.DMA(())],
    )(block)


def _all_to_all(chunks, *, name):
    def body(in_ref, out_ref, send_sems, recv_sems, local_sem):
        x, y, c = _place()
        me = 4 * x + 2 * y + c
        mine = pltpu.make_async_copy(in_ref.at[me], out_ref.at[me], local_sem)
        mine.start()
        peers = []
        for k in range(1, N_DEV):
            px = 1 - x if k & 4 else x
            py = 1 - y if k & 2 else y
            pc = 1 - c if k & 1 else c
            peers.append((px, py, pc))

        def copy(k, peer):
            pid = 4 * peer[0] + 2 * peer[1] + peer[2]
            return pltpu.make_async_remote_copy(
                src_ref=in_ref.at[pid], dst_ref=out_ref.at[me], send_sem=send_sems.at[k], recv_sem=recv_sems.at[k],
                device_id=peer, device_id_type=MESH)

        def landing(k, peer):
            pid = 4 * peer[0] + 2 * peer[1] + peer[2]
            return pltpu.make_async_remote_copy(
                src_ref=in_ref.at[pid], dst_ref=out_ref.at[pid], send_sem=send_sems.at[k], recv_sem=recv_sems.at[k],
                device_id=peer, device_id_type=MESH)

        sends = [copy(k, peer) for k, peer in enumerate(peers)]
        for cp in sends:
            cp.start()
        for k, peer in enumerate(peers):
            landing(k, peer).wait_recv()
        for cp in sends:
            cp.wait_send()
        mine.wait()

    return pl.pallas_call(
        body, name=name, out_shape=jax.ShapeDtypeStruct(chunks.shape, chunks.dtype),
        in_specs=[pl.BlockSpec(memory_space=pl.ANY)], out_specs=pl.BlockSpec(memory_space=pl.ANY),
        scratch_shapes=[pltpu.SemaphoreType.DMA((7,)), pltpu.SemaphoreType.DMA((7,)), pltpu.SemaphoreType.DMA(())],
    )(chunks)


def _adamw(parts, w, m, v, *, name):
    r, c_dim = w.shape
    tr = r
    for cand in range(ADAM_MAX_ROWS, PACK_ROW_ALIGN - 1, -PACK_ROW_ALIGN):
        if r > ADAM_MAX_ROWS and r % cand == 0:
            tr = cand
            break

    def kern(p_ref, w_ref, m_ref, v_ref, g_ref, d_ref, nm_ref, nv_ref):
        g = p_ref[0].astype(F32)
        for i in range(1, N_DEV):
            g = g + p_ref[i].astype(F32)
        nm = ADAM_B1 * m_ref[...] + (1.0 - ADAM_B1) * g
        nv = ADAM_B2 * v_ref[...] + (1.0 - ADAM_B2) * (g * g)
        m_hat = nm / (1.0 - ADAM_B1 ** ADAM_STEP)
        v_hat = nv / (1.0 - ADAM_B2 ** ADAM_STEP)
        g_ref[...] = g
        d_ref[...] = -ADAM_LR * (m_hat / (jnp.sqrt(v_hat) + ADAM_EPS) + ADAM_WD * w_ref[...])
        nm_ref[...] = nm
        nv_ref[...] = nv

    row = pl.BlockSpec((tr, c_dim), lambda i: (i, 0))
    return pl.pallas_call(
        kern, name=name, out_shape=tuple(jax.ShapeDtypeStruct((r, c_dim), F32) for _ in range(4)), grid=(r // tr,),
        in_specs=[pl.BlockSpec((N_DEV, tr, c_dim), lambda i: (0, i, 0)), row, row, row],
        out_specs=(row, row, row, row), compiler_params=_params(("parallel",)),
    )(parts, w, m, v)


SHARDED = (("w_in", 1), ("w_out", 0), ("w_mlp_in", 1), ("w_mlp_out", 0), ("w_ukv", 1), ("w_uq", 1))
REPLICATED = ("g_mix", "q_norm", "kv_norm", "rpb", "out_norm_a", "out_norm_b", "out_norm_c", "g_mlp", "g_final")
WEIGHTS = ("g_mix", "w_in", "q_norm", "w_uq", "kv_norm", "w_ukv", "rpb", "out_norm_a", "out_norm_b", "out_norm_c",
           "w_out", "g_mlp", "w_mlp_in", "w_mlp_out", "g_final")


def _pack_rows(arrs, dtype):
    parts = [a.reshape(-1, PACK_COLS).astype(dtype) for a in arrs]
    rows = sum(p.shape[0] for p in parts)
    pad = -rows % PACK_ROW_ALIGN
    if pad:
        parts.append(jnp.zeros((pad, PACK_COLS), dtype))
    return jnp.concatenate(parts, axis=0)


def _unpack_rows(packed, shapes):
    out, off = [], 0
    for shp in shapes:
        rows = int(np.prod(shp)) // PACK_COLS
        out.append(packed[..., off:off + rows, :].reshape(packed.shape[:-2] + tuple(shp)))
        off += rows
    return out


def _pack_flat(arrs):
    flat = jnp.concatenate([a.reshape(-1) for a in arrs])
    pad = -flat.shape[0] % (8 * PACK_COLS)
    return jnp.pad(flat, (0, pad)).reshape(-1, PACK_COLS)


def _unpack_flat(packed, shapes):
    flat = packed.reshape(-1)
    out, off = [], 0
    for shp in shapes:
        size = int(np.prod(shp))
        out.append(flat[off:off + size].reshape(shp))
        off += size
    return out


def _gathered_to_full(g, axis):
    n, l, a, b = g.shape
    if axis == 0:
        return jnp.transpose(g, (1, 0, 2, 3)).reshape(l, n * a, b)
    return jnp.transpose(g, (1, 2, 0, 3)).reshape(l, a, n * b)


def _full_to_chunks(w, axis):
    l, a, b = w.shape
    if axis == 0:
        return jnp.transpose(w.reshape(l, N_DEV, a // N_DEV, b), (1, 0, 2, 3))
    return jnp.transpose(w.reshape(l, a, N_DEV, b // N_DEV), (2, 0, 1, 3))


def kernel(x, g_mix, w_in, q_norm, w_uq, kv_norm, w_ukv, rpb, out_norm_a, out_norm_b, out_norm_c, w_out, g_mlp, w_mlp_in, w_mlp_out, g_final, loss_target, m_g_mix, m_w_in, m_q_norm, m_w_uq, m_kv_norm, m_w_ukv, m_rpb, m_out_norm_a, m_out_norm_b, m_out_norm_c, m_w_out, m_g_mlp, m_w_mlp_in, m_w_mlp_out, m_g_final, v_g_mix, v_w_in, v_q_norm, v_w_uq, v_kv_norm, v_w_ukv, v_rpb, v_out_norm_a, v_out_norm_b, v_out_norm_c, v_w_out, v_g_mlp, v_w_mlp_in, v_w_mlp_out, v_g_final):
    given = dict(locals())
    w = {n: given[n] for n in WEIGHTS}
    m = {n: given["m_" + n] for n in WEIGHTS}
    v = {n: given["v_" + n] for n in WEIGHTS}
    s = x.shape[1]
    depth = g_mix.shape[0]
    shard_shapes = [w[n].shape for n, _ in SHARDED]

    gathered = _all_gather(_pack_rows([w[n] for n, _ in SHARDED], BF16), name="gather_weights")
    full = {}
    for (n, axis), g in zip(SHARDED, _unpack_rows(gathered, shard_shapes)):
        full[n] = _gathered_to_full(g, axis).astype(F32)

    layer_w = [dict({n: full[n][l] for n, _ in SHARDED}, **{n: w[n][l] for n in REPLICATED if n != "g_final"})
               for l in range(depth)]
    rope32 = _rope_tables(s, QK_ROPE)
    rope64 = _rope_tables(s, HEAD_DIM)

    def trunk(x0, lws):
        h = x0
        for lw in lws:
            h = _layer(h, lw, rope32, rope64)
        return h

    x_out, pullback = jax.vjp(trunk, x[0], layer_w)
    loss_local, dx_out, d_g_final = _loss_head(x_out, g_final, loss_target[0], name="loss_head")
    dx0, d_layers = pullback(dx_out)
    loss = lax.psum(loss_local, AXES)

    grads_local = {n: jnp.stack([d[n] for d in d_layers]) for n in layer_w[0]}
    grads_local["g_final"] = d_g_final

    chunks = [_full_to_chunks(grads_local[n], axis).reshape(N_DEV, -1, PACK_COLS).astype(BF16)
              for n, axis in SHARDED]
    rows = sum(c.shape[1] for c in chunks)
    pad = -rows % PACK_ROW_ALIGN
    if pad:
        chunks.append(jnp.zeros((N_DEV, pad, PACK_COLS), BF16))
    parts = _all_to_all(jnp.concatenate(chunks, axis=1), name="scatter_grads")
    big = _adamw(parts, _pack_rows([w[n] for n, _ in SHARDED], F32), _pack_rows([m[n] for n, _ in SHARDED], F32),
                 _pack_rows([v[n] for n, _ in SHARDED], F32), name="adamw_sharded")
    big = [_unpack_rows(o, shard_shapes) for o in big]

    rep_shapes = [w[n].shape for n in REPLICATED]
    rep_parts = _all_gather(_pack_flat([grads_local[n] for n in REPLICATED]), name="gather_small_grads")
    small = _adamw(rep_parts, _pack_flat([w[n] for n in REPLICATED]), _pack_flat([m[n] for n in REPLICATED]),
                   _pack_flat([v[n] for n in REPLICATED]), name="adamw_replicated")
    small = [_unpack_flat(o, rep_shapes) for o in small]

    result = {}
    for kind, idx in (("grad", 0), ("delta", 1), ("new_m", 2), ("new_v", 3)):
        for j, (n, _) in enumerate(SHARDED):
            result[kind + "_" + n] = big[idx][j]
        for j, n in enumerate(REPLICATED):
            result[kind + "_" + n] = small[idx][j]
    outs = [loss, dx0[None]]
    for kind in ("grad", "delta", "new_m", "new_v"):
        outs += [result[kind + "_" + n] for n in WEIGHTS]
    return tuple(outs)
```

```python
import functools

import numpy as np
import jax
import jax.numpy as jnp
from jax import lax
from jax.experimental import pallas as pl
from jax.experimental.pallas import tpu as pltpu

F32 = jnp.float32
BF16 = jnp.bfloat16

D_MODEL = 1024
DEPTH = 4
HEAD_DIM = 64
HEADS_A = 6
HEADS_B = 6
HEADS_C = 4
Q_LORA = 256
KV_LORA = 128
QK_NOPE = 64
QK_ROPE = 32
V_DIM_A = 64
DILATED_PAIRS = ((128, 1), (512, 4), (2048, 16))
BAND_HALF = 64
GRID_W = 64
NA_ROWS = 8
NA_COLS = 16
D_FF = 4096
ROPE_THETA = 10000.0
NORM_EPS = 1e-6
NEG_INF = -1e30

COLS_A = Q_LORA + KV_LORA + QK_ROPE
COLS_B = 3 * HEADS_B * HEAD_DIM
COLS_C = 3 * HEADS_C * HEAD_DIM
WIDTH_A = HEADS_A * V_DIM_A
WIDTH_B = HEADS_B * HEAD_DIM
WIDTH_C = HEADS_C * HEAD_DIM

ADAM_LR = 0.001
ADAM_B1 = 0.9
ADAM_B2 = 0.999
ADAM_EPS = 1e-08
ADAM_WD = 0.01
ADAM_STEP = 10

N_DEV = 8
AXES = ("x", "y", "c")
MESH = pl.DeviceIdType.MESH

V7X_VMEM_LIMIT = 48 * 1024 * 1024
MM_VMEM_BUDGET = 32 * 1024 * 1024
LANES = 128
PACK_COLS = 1024
PACK_ROW_ALIGN = 16
ADAM_MAX_ROWS = 160
DENSE_TQ = 256
DENSE_TK_FWD = 512
DENSE_TK_BWD = 1024
WIN_TILES_PER_STEP = 4
DENSE_DV = V_DIM_A
DENSE_DV_PAD = 80
BAND_TILE = 256
NA_TILE_ROWS = 4


def _params(sem=None):
    return pltpu.CompilerParams(dimension_semantics=sem, vmem_limit_bytes=V7X_VMEM_LIMIT)


def _tile(dim, pref):
    if dim <= pref:
        return dim
    for t in range(pref - pref % LANES, LANES - 1, -LANES):
        if dim % t == 0:
            return t
    return dim


def _mm(a, b, *, ta=False, tb=False, out_dtype=F32, epilogue=None, extra=None, name):
    if ta:
        k_dim, m_dim = a.shape
    else:
        m_dim, k_dim = a.shape
    n_dim = b.shape[0] if tb else b.shape[1]
    tn = _tile(n_dim, 1024) if n_dim % LANES == 0 else n_dim
    tk = _tile(k_dim, 1024 if not ta else 512) if k_dim % LANES == 0 else k_dim
    n_out = 2 if epilogue == "relu2" else 1
    for tm in (_tile(m_dim, 1024), _tile(m_dim, 512), _tile(m_dim, 256)):
        blocks = (tm * tk * a.dtype.itemsize + tk * tn * b.dtype.itemsize
                  + (tm * tn * 4 if extra is not None else 0) + n_out * tm * tn * 4)
        if 2 * blocks + tm * tn * 4 <= MM_VMEM_BUDGET:
            break
    nk = k_dim // tk
    a_spec = (pl.BlockSpec((tk, tm), lambda i, j, k: (k, i)) if ta
              else pl.BlockSpec((tm, tk), lambda i, j, k: (i, k)))
    b_spec = (pl.BlockSpec((tn, tk), lambda i, j, k: (j, k)) if tb
              else pl.BlockSpec((tk, tn), lambda i, j, k: (k, j)))
    o_spec = pl.BlockSpec((tm, tn), lambda i, j, k: (i, j))
    dims = (((0 if ta else 1,), (1 if tb else 0,)), ((), ()))
    in_specs = [a_spec, b_spec]
    operands = [a, b]
    if epilogue in ("drelu2", "add"):
        in_specs.append(o_spec)
        operands.append(extra)
    if epilogue == "relu2":
        out_shape = (jax.ShapeDtypeStruct((m_dim, n_dim), F32), jax.ShapeDtypeStruct((m_dim, n_dim), BF16))
        out_specs = (o_spec, o_spec)
    else:
        out_shape = jax.ShapeDtypeStruct((m_dim, n_dim), out_dtype)
        out_specs = o_spec

    def kern(*refs):
        acc_ref = refs[-1]
        a_ref, b_ref = refs[0], refs[1]
        k = pl.program_id(2)

        @pl.when(k == 0)
        def _():
            acc_ref[...] = jnp.zeros_like(acc_ref)

        acc_ref[...] += lax.dot_general(a_ref[...].astype(BF16), b_ref[...].astype(BF16), dims,
                                        preferred_element_type=F32)

        @pl.when(k == nk - 1)
        def _():
            acc = acc_ref[...]
            if epilogue == "relu2":
                refs[2][...] = acc
                r = jnp.maximum(acc, 0.0)
                refs[3][...] = (r * r).astype(BF16)
            elif epilogue == "drelu2":
                refs[3][...] = (acc * (2.0 * jnp.maximum(refs[2][...], 0.0))).astype(out_dtype)
            elif epilogue == "add":
                refs[3][...] = (acc + refs[2][...]).astype(out_dtype)
            else:
                refs[2][...] = acc.astype(out_dtype)

    return pl.pallas_call(
        kern, name=name, out_shape=out_shape, grid=(m_dim // tm, n_dim // tn, nk),
        in_specs=in_specs, out_specs=out_specs, scratch_shapes=[pltpu.VMEM((tm, tn), F32)],
        compiler_params=_params(("parallel", "parallel", "arbitrary")),
    )(*operands)


def _norm_fwd(x, g, *, out_dtype, name):
    s, w = x.shape
    ts = _tile(s, 512)

    def kern(x_ref, g_ref, y_ref):
        xv = x_ref[...]
        r = lax.rsqrt(jnp.mean(xv * xv, axis=-1, keepdims=True) + NORM_EPS)
        y_ref[...] = (xv * r * g_ref[...]).astype(out_dtype)

    return pl.pallas_call(
        kern, name=name, out_shape=jax.ShapeDtypeStruct((s, w), out_dtype), grid=(s // ts,),
        in_specs=[pl.BlockSpec((ts, w), lambda i: (i, 0)), pl.BlockSpec((1, w), lambda i: (0, 0))],
        out_specs=pl.BlockSpec((ts, w), lambda i: (i, 0)),
        compiler_params=_params(("parallel",)),
    )(x, g.reshape(1, w))


def _norm_bwd(x, g, dy, *, name):
    s, w = x.shape
    ts = _tile(s, 512)

    def kern(x_ref, g_ref, dy_ref, dx_ref, dg_ref):
        @pl.when(pl.program_id(0) == 0)
        def _():
            dg_ref[...] = jnp.zeros_like(dg_ref)

        xv = x_ref[...]
        dyv = dy_ref[...]
        r = lax.rsqrt(jnp.mean(xv * xv, axis=-1, keepdims=True) + NORM_EPS)
        xhat = xv * r
        dg_ref[...] += jnp.sum(dyv * xhat, axis=0, keepdims=True)
        dxhat = dyv * g_ref[...]
        dx_ref[...] = r * (dxhat - xhat * jnp.mean(dxhat * xhat, axis=-1, keepdims=True))

    dx, dg = pl.pallas_call(
        kern, name=name,
        out_shape=(jax.ShapeDtypeStruct((s, w), F32), jax.ShapeDtypeStruct((1, w), F32)), grid=(s // ts,),
        in_specs=[pl.BlockSpec((ts, w), lambda i: (i, 0)), pl.BlockSpec((1, w), lambda i: (0, 0)),
                  pl.BlockSpec((ts, w), lambda i: (i, 0))],
        out_specs=(pl.BlockSpec((ts, w), lambda i: (i, 0)), pl.BlockSpec((1, w), lambda i: (0, 0))),
        compiler_params=_params(("arbitrary",)),
    )(x, g.reshape(1, w), dy)
    return dx, dg.reshape(w)


def _loss_head(x, g, target, *, name):
    s, w = x.shape
    ts = _tile(s, 512)

    def kern(x_ref, g_ref, t_ref, loss_ref, dx_ref, dg_ref):
        @pl.when(pl.program_id(0) == 0)
        def _():
            dg_ref[...] = jnp.zeros_like(dg_ref)
            loss_ref[...] = jnp.zeros_like(loss_ref)

        xv = x_ref[...]
        gv = g_ref[...]
        r = lax.rsqrt(jnp.mean(xv * xv, axis=-1, keepdims=True) + NORM_EPS)
        xhat = xv * r
        err = xhat * gv - t_ref[...]
        loss_ref[...] += 0.5 * jnp.sum(jnp.mean(err * err, axis=-1, keepdims=True))
        dyv = err * (1.0 / w)
        dg_ref[...] += jnp.sum(dyv * xhat, axis=0, keepdims=True)
        dxhat = dyv * gv
        dx_ref[...] = r * (dxhat - xhat * jnp.mean(dxhat * xhat, axis=-1, keepdims=True))

    loss, dx, dg = pl.pallas_call(
        kern, name=name,
        out_shape=(jax.ShapeDtypeStruct((1, LANES), F32), jax.ShapeDtypeStruct((s, w), F32),
                   jax.ShapeDtypeStruct((1, w), F32)),
        grid=(s // ts,),
        in_specs=[pl.BlockSpec((ts, w), lambda i: (i, 0)), pl.BlockSpec((1, w), lambda i: (0, 0)),
                  pl.BlockSpec((ts, w), lambda i: (i, 0))],
        out_specs=(pl.BlockSpec((1, LANES), lambda i: (0, 0)), pl.BlockSpec((ts, w), lambda i: (i, 0)),
                   pl.BlockSpec((1, w), lambda i: (0, 0))),
        compiler_params=_params(("arbitrary",)),
    )(x, g.reshape(1, w), target)
    return loss[0, 0], dx, dg.reshape(w)


def _dense_fwd(q_t, k, v_t, *, scale, name):
    g_dim, dq, n = q_t.shape
    dv = DENSE_DV
    dve = v_t.shape[1]
    tq = _tile(n, DENSE_TQ)
    tk = _tile(n, DENSE_TK_FWD)
    nk = n // tk

    def kern(q_ref, k_ref, v_ref, o_ref, lse_ref):
        qb = q_ref[0]

        def scores(j):
            start = pl.multiple_of(j * tk, tk)
            return jnp.dot(k_ref[0, pl.ds(start, tk), :], qb, preferred_element_type=F32) * scale

        def weighted_values(j, p):
            start = pl.multiple_of(j * tk, tk)
            return jnp.dot(v_ref[0, :, pl.ds(start, tk)], p, preferred_element_type=F32)

        def body(j, carry):
            m, acc, s, p_prev, alpha_prev = carry
            s_next = scores(jnp.minimum(j + 1, nk - 1))
            m_new = jnp.maximum(m, jnp.max(s, axis=0, keepdims=True))
            p = jnp.exp(s - m_new).astype(BF16)
            acc = alpha_prev * acc + weighted_values(jnp.maximum(j - 1, 0), p_prev)
            return m_new, acc, s_next, p, jnp.exp(m - m_new)

        init = (jnp.full((1, tq), NEG_INF, F32), jnp.zeros((dve, tq), F32), scores(0),
                jnp.zeros((tk, tq), BF16), jnp.ones((1, tq), F32))
        m, acc, _, p_last, alpha_last = lax.fori_loop(0, nk, body, init)
        acc = alpha_last * acc + weighted_values(nk - 1, p_last)
        l = acc[dv:dv + 1]
        o_ref[0] = acc[:dv] / l
        lse_ref[0] = m + jnp.log(l)

    return pl.pallas_call(
        kern, name=name,
        out_shape=(jax.ShapeDtypeStruct((g_dim, dv, n), F32), jax.ShapeDtypeStruct((g_dim, 1, n), F32)),
        grid=(g_dim, n // tq),
        in_specs=[pl.BlockSpec((1, dq, tq), lambda g, i: (g, 0, i)), pl.BlockSpec((1, n, dq), lambda g, i: (g, 0, 0)),
                  pl.BlockSpec((1, dve, n), lambda g, i: (g, 0, 0))],
        out_specs=(pl.BlockSpec((1, dv, tq), lambda g, i: (g, 0, i)), pl.BlockSpec((1, 1, tq), lambda g, i: (g, 0, i))),
        compiler_params=_params(("parallel", "arbitrary")),
    )(q_t, k, v_t)


def _dense_bwd(q_t, k, k_t, v, o_t, lse, do_t, *, scale, name):
    g_dim, dq, n = q_t.shape
    dv = v.shape[-1]
    tq = _tile(n, DENSE_TQ)
    tk = _tile(n, DENSE_TK_BWD)
    nk = n // tk
    nt = (((1,), (1,)), ((), ()))

    def kern(q_ref, k_ref, kt_ref, v_ref, o_ref, lse_ref, do_ref, dq_ref, dk_ref, dv_ref):
        @pl.when(pl.program_id(1) == 0)
        def _():
            dk_ref[...] = jnp.zeros_like(dk_ref)
            dv_ref[...] = jnp.zeros_like(dv_ref)

        qb = q_ref[0]
        dob = do_ref[0]
        delta = jnp.sum(dob * o_ref[0], axis=0, keepdims=True)
        dob16 = dob.astype(BF16)
        lse_v = lse_ref[0]

        def body(j, dq_acc):
            start = pl.multiple_of(j * tk, tk)
            s = jnp.dot(k_ref[0, pl.ds(start, tk), :], qb, preferred_element_type=F32) * scale
            p = jnp.exp(s - lse_v)
            dp = jnp.dot(v_ref[0, pl.ds(start, tk), :], dob16, preferred_element_type=F32)
            ds16 = (p * (dp - delta) * scale).astype(BF16)
            dv_ref[0, pl.ds(start, tk), :] += lax.dot_general(p.astype(BF16), dob16, nt, preferred_element_type=F32)
            dk_ref[0, pl.ds(start, tk), :] += lax.dot_general(ds16, qb, nt, preferred_element_type=F32)
            return dq_acc + jnp.dot(kt_ref[0, :, pl.ds(start, tk)], ds16, preferred_element_type=F32)

        dq_ref[0] = lax.fori_loop(0, nk, body, jnp.zeros((dq, tq), F32))

    qt_spec = pl.BlockSpec((1, dq, tq), lambda g, i: (g, 0, i))
    ot_spec = pl.BlockSpec((1, dv, tq), lambda g, i: (g, 0, i))
    l_spec = pl.BlockSpec((1, 1, tq), lambda g, i: (g, 0, i))
    k_spec = pl.BlockSpec((1, n, dq), lambda g, i: (g, 0, 0))
    kt_spec = pl.BlockSpec((1, dq, n), lambda g, i: (g, 0, 0))
    v_spec = pl.BlockSpec((1, n, dv), lambda g, i: (g, 0, 0))
    return pl.pallas_call(
        kern, name=name,
        out_shape=(jax.ShapeDtypeStruct((g_dim, dq, n), F32), jax.ShapeDtypeStruct((g_dim, n, dq), F32),
                   jax.ShapeDtypeStruct((g_dim, n, dv), F32)),
        grid=(g_dim, n // tq), in_specs=[qt_spec, k_spec, kt_spec, v_spec, ot_spec, l_spec, ot_spec],
        out_specs=(qt_spec, k_spec, v_spec), compiler_params=_params(("parallel", "arbitrary")),
    )(q_t, k, k_t, v, o_t, lse, do_t)


def _make_dense_attention(scale, name):
    @jax.custom_vjp
    def att(q, k, v):
        return att_fwd(q, k, v)[0]

    def att_fwd(q, k, v):
        n, h, dv = v.shape
        q_t = jnp.transpose(q.astype(BF16), (1, 2, 0))
        k16 = jnp.transpose(k.astype(BF16), (1, 0, 2))
        v16 = v.astype(BF16)
        v_t = jnp.concatenate([jnp.transpose(v16, (1, 2, 0)), jnp.ones((h, 1, n), BF16),
                               jnp.zeros((h, DENSE_DV_PAD - dv - 1, n), BF16)], axis=1)
        o_t, lse = _dense_fwd(q_t, k16, v_t, scale=scale, name=name + "_fwd")
        return jnp.transpose(o_t, (2, 0, 1)), (q_t, k16, jnp.transpose(v16, (1, 0, 2)), o_t, lse)

    def att_bwd(res, do):
        q_t, k16, v16, o_t, lse = res
        dq_t, dk, dv_ = _dense_bwd(q_t, k16, jnp.transpose(k16, (0, 2, 1)), v16, o_t, lse,
                                   jnp.transpose(do, (1, 2, 0)), scale=scale, name=name + "_bwd")
        return jnp.transpose(dq_t, (2, 0, 1)), jnp.transpose(dk, (1, 0, 2)), jnp.transpose(dv_, (1, 0, 2))

    att.defvjp(att_fwd, att_bwd)
    return att


def _window(mode, n):
    if mode == "band":
        tq = min(BAND_TILE, n)
        return tq, BAND_HALF, tq + 2 * BAND_HALF
    tq = NA_TILE_ROWS * GRID_W
    front = (NA_ROWS // 2) * GRID_W
    return tq, front, tq + NA_ROWS * GRID_W


def _window_mask(mode, i0, tq, front, span, n):
    q_pos = i0 + lax.broadcasted_iota(jnp.int32, (tq, 1), 0)
    k_pos = i0 - front + lax.broadcasted_iota(jnp.int32, (1, span), 1)
    if mode == "band":
        diff = k_pos - q_pos
        return (diff <= BAND_HALF) & (diff >= -BAND_HALF) & (k_pos >= 0) & (k_pos < n)
    rows = n // GRID_W
    shift = GRID_W.bit_length() - 1
    r_start = jnp.clip((q_pos >> shift) - NA_ROWS // 2, 0, rows - NA_ROWS)
    c_start = jnp.clip((q_pos & (GRID_W - 1)) - NA_COLS // 2, 0, GRID_W - NA_COLS)
    kr = k_pos >> shift
    kc = k_pos & (GRID_W - 1)
    return (kr >= r_start) & (kr < r_start + NA_ROWS) & (kc >= c_start) & (kc < c_start + NA_COLS)


def _win_fwd(q, k, v, bias, *, mode, scale, name):
    g_dim, n, d = q.shape
    tq, front, span = _window(mode, n)
    sub = min(WIN_TILES_PER_STEP, n // tq)
    n_pad = k.shape[1]

    def kern(*refs):
        if mode == "natten":
            q_ref, k_ref, v_ref, b_ref, o_ref, lse_ref = refs
        else:
            q_ref, k_ref, v_ref, o_ref, lse_ref = refs
        for t in range(sub):
            rows = pl.ds(t * tq, tq)
            i0 = pl.multiple_of(pl.program_id(1) * (sub * tq) + t * tq, tq)
            kb = k_ref[0, pl.ds(i0, span), :]
            vb = v_ref[0, pl.ds(i0, span), :]
            s = lax.dot_general(q_ref[0, rows, :], kb, (((1,), (1,)), ((), ())), preferred_element_type=F32) * scale
            if mode == "natten":
                s = s + b_ref[0]
            s = jnp.where(_window_mask(mode, i0, tq, front, span, n), s, NEG_INF)
            m = jnp.max(s, axis=-1, keepdims=True)
            p = jnp.exp(s - m)
            l = jnp.sum(p, axis=-1, keepdims=True)
            o_ref[0, rows, :] = jnp.dot(p.astype(BF16), vb, preferred_element_type=F32) / l
            lse_ref[0, rows, :] = m + jnp.log(l)

    in_specs = [pl.BlockSpec((1, sub * tq, d), lambda g, i: (g, i, 0)),
                pl.BlockSpec((1, n_pad, d), lambda g, i: (g, 0, 0)), pl.BlockSpec((1, n_pad, d), lambda g, i: (g, 0, 0))]
    operands = [q, k, v]
    if mode == "natten":
        in_specs.append(pl.BlockSpec((1, tq, span), lambda g, i: (g, 0, 0)))
        operands.append(bias)
    return pl.pallas_call(
        kern, name=name,
        out_shape=(jax.ShapeDtypeStruct((g_dim, n, d), F32), jax.ShapeDtypeStruct((g_dim, n, 1), F32)),
        grid=(g_dim, n // (sub * tq)), in_specs=in_specs,
        out_specs=(pl.BlockSpec((1, sub * tq, d), lambda g, i: (g, i, 0)),
                   pl.BlockSpec((1, sub * tq, 1), lambda g, i: (g, i, 0))),
        compiler_params=_params(("parallel", "arbitrary")),
    )(*operands)


def _win_bwd(q, k, v, bias, o, lse, do, dlse, *, mode, scale, name):
    g_dim, n, d = q.shape
    tq, front, span = _window(mode, n)
    sub = min(WIN_TILES_PER_STEP, n // tq)
    n_pad = k.shape[1]

    def kern(*refs):
        if mode == "natten":
            q_ref, k_ref, v_ref, b_ref, o_ref, lse_ref, do_ref, dlse_ref, dq_ref, dk_ref, dv_ref, db_ref = refs
        else:
            q_ref, k_ref, v_ref, o_ref, lse_ref, do_ref, dlse_ref, dq_ref, dk_ref, dv_ref = refs

        @pl.when(pl.program_id(1) == 0)
        def _():
            dk_ref[...] = jnp.zeros_like(dk_ref)
            dv_ref[...] = jnp.zeros_like(dv_ref)
            if mode == "natten":
                db_ref[...] = jnp.zeros_like(db_ref)

        for t in range(sub):
            rows = pl.ds(t * tq, tq)
            i0 = pl.multiple_of(pl.program_id(1) * (sub * tq) + t * tq, tq)
            qb = q_ref[0, rows, :]
            kb = k_ref[0, pl.ds(i0, span), :]
            vb = v_ref[0, pl.ds(i0, span), :]
            dob = do_ref[0, rows, :]
            delta = jnp.sum(dob * o_ref[0, rows, :], axis=-1, keepdims=True) - dlse_ref[0, rows, :]
            dob16 = dob.astype(BF16)
            s = lax.dot_general(qb, kb, (((1,), (1,)), ((), ())), preferred_element_type=F32) * scale
            if mode == "natten":
                s = s + b_ref[0]
            p = jnp.where(_window_mask(mode, i0, tq, front, span, n), jnp.exp(s - lse_ref[0, rows, :]), 0.0)
            dp = lax.dot_general(dob16, vb, (((1,), (1,)), ((), ())), preferred_element_type=F32)
            ds = p * (dp - delta)
            if mode == "natten":
                db_ref[0] += ds
            ds16 = (ds * scale).astype(BF16)
            dv_ref[0, pl.ds(i0, span), :] += lax.dot_general(
                p.astype(BF16), dob16, (((0,), (0,)), ((), ())), preferred_element_type=F32)
            dk_ref[0, pl.ds(i0, span), :] += lax.dot_general(
                ds16, qb, (((0,), (0,)), ((), ())), preferred_element_type=F32)
            dq_ref[0, rows, :] = jnp.dot(ds16, kb, preferred_element_type=F32)

    q_spec = pl.BlockSpec((1, sub * tq, d), lambda g, i: (g, i, 0))
    k_spec = pl.BlockSpec((1, n_pad, d), lambda g, i: (g, 0, 0))
    l_spec = pl.BlockSpec((1, sub * tq, 1), lambda g, i: (g, i, 0))
    in_specs = [q_spec, k_spec, k_spec]
    operands = [q, k, v]
    out_shape = [jax.ShapeDtypeStruct((g_dim, n, d), F32), jax.ShapeDtypeStruct((g_dim, n_pad, d), F32),
                 jax.ShapeDtypeStruct((g_dim, n_pad, d), F32)]
    out_specs = [q_spec, k_spec, k_spec]
    if mode == "natten":
        b_spec = pl.BlockSpec((1, tq, span), lambda g, i: (g, 0, 0))
        in_specs.append(b_spec)
        operands.append(bias)
        out_shape.append(jax.ShapeDtypeStruct(bias.shape, F32))
        out_specs.append(b_spec)
    in_specs += [q_spec, l_spec, q_spec, l_spec]
    operands += [o, lse, do, dlse]
    return pl.pallas_call(
        kern, name=name, out_shape=tuple(out_shape), grid=(g_dim, n // (sub * tq)), in_specs=in_specs,
        out_specs=tuple(out_specs), compiler_params=_params(("parallel", "arbitrary")),
    )(*operands)


def _dilate(t, dil):
    s, h, d = t.shape
    return jnp.transpose(t.reshape(s // dil, dil, h, d), (1, 2, 0, 3)).reshape(dil * h, s // dil, d)


def _undilate(t, dil):
    gh, n, d = t.shape
    return jnp.transpose(t.reshape(dil, gh // dil, n, d), (2, 0, 1, 3)).reshape(n * dil, gh // dil, d)


def _make_window_attention(mode, scale, dil, name):
    @jax.custom_vjp
    def att(q, k, v, bias):
        return att_fwd(q, k, v, bias)[0]

    def att_fwd(q, k, v, bias):
        tq, front, span = _window(mode, q.shape[0] // dil)
        pad = ((0, 0), (front, span - tq - front), (0, 0))
        q16 = _dilate(q.astype(BF16), dil)
        k16 = jnp.pad(_dilate(k.astype(BF16), dil), pad)
        v16 = jnp.pad(_dilate(v.astype(BF16), dil), pad)
        o, lse = _win_fwd(q16, k16, v16, bias, mode=mode, scale=scale, name=name + "_fwd")
        return (_undilate(o, dil), _undilate(lse, dil)[..., 0]), (q16, k16, v16, bias, o, lse)

    def att_bwd(res, cts):
        q16, k16, v16, bias, o, lse = res
        do, dlse = cts
        n = q16.shape[1]
        front = _window(mode, n)[1]
        outs = _win_bwd(q16, k16, v16, bias, o, lse, _dilate(do, dil), _dilate(dlse[..., None], dil),
                        mode=mode, scale=scale, name=name + "_bwd")
        dbias = outs[3] if mode == "natten" else None
        return (_undilate(outs[0], dil), _undilate(outs[1][:, front:front + n], dil),
                _undilate(outs[2][:, front:front + n], dil), dbias)

    att.defvjp(att_fwd, att_bwd)
    return att


def _na_onehots():
    tq, front, span = _window("natten", 0)
    q_rows, k_rows, front_rows = tq // GRID_W, span // GRID_W, front // GRID_W
    e_r = np.zeros((q_rows, k_rows, 2 * NA_ROWS - 1), np.float32)
    for qr in range(q_rows):
        for kr in range(k_rows):
            a = kr - front_rows - qr + NA_ROWS - 1
            if 0 <= a < 2 * NA_ROWS - 1:
                e_r[qr, kr, a] = 1.0
    e_c = np.zeros((GRID_W, GRID_W, 2 * NA_COLS - 1), np.float32)
    for qc in range(GRID_W):
        for kc in range(GRID_W):
            b = kc - qc + NA_COLS - 1
            if 0 <= b < 2 * NA_COLS - 1:
                e_c[qc, kc, b] = 1.0
    return e_r, e_c


def _na_bias_tiles(rpb):
    tq, _, span = _window("natten", 0)
    e_r, e_c = _na_onehots()
    t = jnp.einsum("ikA,xyB,hAB->hixky", e_r, e_c, rpb, precision=lax.Precision.HIGHEST)
    return t.reshape(rpb.shape[0], tq, span)


def _make_norm_linear(name):
    @jax.custom_vjp
    def f(x, g, w):
        return f_fwd(x, g, w)[0]

    def f_fwd(x, g, w):
        w = w.astype(BF16)
        h = _norm_fwd(x, g, out_dtype=BF16, name=name + "_norm")
        return _mm(h, w, name=name + "_mm"), (x, g, w, h)

    def f_bwd(res, dy):
        x, g, w, h = res
        dh = _mm(dy, w, tb=True, name=name + "_dh")
        dw = _mm(h, dy, ta=True, name=name + "_dw")
        dx, dg = _norm_bwd(x, g, dh, name=name + "_dnorm")
        return dx, dg, dw

    f.defvjp(f_fwd, f_bwd)
    return f


def _make_mix_out(name):
    @jax.custom_vjp
    def f(x, oa, ob, oc, ga, gb, gc, w):
        return f_fwd(x, oa, ob, oc, ga, gb, gc, w)[0]

    def f_fwd(x, oa, ob, oc, ga, gb, gc, w):
        mixed = jnp.concatenate([
            _norm_fwd(oa, ga, out_dtype=BF16, name=name + "_norm_a"),
            _norm_fwd(ob, gb, out_dtype=BF16, name=name + "_norm_b"),
            _norm_fwd(oc, gc, out_dtype=BF16, name=name + "_norm_c")], axis=-1)
        w = w.astype(BF16)
        y = _mm(mixed, w, epilogue="add", extra=x, name=name + "_mm")
        return y, (oa, ob, oc, ga, gb, gc, w, mixed)

    def f_bwd(res, dy):
        oa, ob, oc, ga, gb, gc, w, mixed = res
        dmixed = _mm(dy, w, tb=True, name=name + "_dmixed")
        dw = _mm(mixed, dy, ta=True, name=name + "_dw")
        doa, dga = _norm_bwd(oa, ga, dmixed[:, :WIDTH_A], name=name + "_dnorm_a")
        dob, dgb = _norm_bwd(ob, gb, dmixed[:, WIDTH_A:WIDTH_A + WIDTH_B], name=name + "_dnorm_b")
        doc, dgc = _norm_bwd(oc, gc, dmixed[:, WIDTH_A + WIDTH_B:], name=name + "_dnorm_c")
        return dy, doa, dob, doc, dga, dgb, dgc, dw

    f.defvjp(f_fwd, f_bwd)
    return f


def _make_mlp(name):
    @jax.custom_vjp
    def f(x, g, w1, w2):
        return f_fwd(x, g, w1, w2)[0]

    def f_fwd(x, g, w1, w2):
        w1, w2 = w1.astype(BF16), w2.astype(BF16)
        h = _norm_fwd(x, g, out_dtype=BF16, name=name + "_norm")
        u, a = _mm(h, w1, epilogue="relu2", name=name + "_up")
        y = _mm(a, w2, epilogue="add", extra=x, name=name + "_down")
        return y, (x, g, w1, w2, h, u, a)

    def f_bwd(res, dy):
        x, g, w1, w2, h, u, a = res
        du = _mm(dy, w2, tb=True, epilogue="drelu2", extra=u, out_dtype=BF16, name=name + "_du")
        dw2 = _mm(a, dy, ta=True, name=name + "_dw2")
        dw1 = _mm(h, du, ta=True, name=name + "_dw1")
        dh = _mm(du, w1, tb=True, name=name + "_dh")
        dx, dg = _norm_bwd(x, g, dh, name=name + "_dnorm")
        return dx + dy, dg, dw1, dw2

    f.defvjp(f_fwd, f_bwd)
    return f


def _rope_tables(s, dim):
    half = dim // 2
    inv_freq = ROPE_THETA ** (-jnp.arange(half, dtype=F32) / half)
    ang = jnp.arange(s, dtype=F32)[:, None] * inv_freq[None, :]
    return jnp.cos(ang)[:, None, :], jnp.sin(ang)[:, None, :]


def _rope(x, cos, sin):
    half = x.shape[-1] // 2
    x1, x2 = x[..., :half], x[..., half:]
    return jnp.concatenate([x1 * cos - x2 * sin, x1 * sin + x2 * cos], axis=-1)


def _layer(x, lw, rope32, rope64):
    s = x.shape[0]
    proj = _make_norm_linear("in")(x, lw["g_mix"], lw["w_in"])
    c_q = proj[:, :Q_LORA]
    c_kv = proj[:, Q_LORA:Q_LORA + KV_LORA]
    k_pe = proj[:, Q_LORA + KV_LORA:COLS_A]
    p_b = proj[:, COLS_A:COLS_A + COLS_B].reshape(s, 3, HEADS_B, HEAD_DIM)
    p_c = proj[:, COLS_A + COLS_B:].reshape(s, 3, HEADS_C, HEAD_DIM)

    qa = _make_norm_linear("uq")(c_q, lw["q_norm"], lw["w_uq"]).reshape(s, HEADS_A, QK_NOPE + QK_ROPE)
    kva = _make_norm_linear("ukv")(c_kv, lw["kv_norm"], lw["w_ukv"]).reshape(s, HEADS_A, QK_NOPE + V_DIM_A)
    k_pe = jnp.broadcast_to(_rope(k_pe[:, None, :], *rope32), (s, HEADS_A, QK_ROPE))
    qa = jnp.concatenate([qa[..., :QK_NOPE], _rope(qa[..., QK_NOPE:], *rope32)], axis=-1)
    ka = jnp.concatenate([kva[..., :QK_NOPE], k_pe], axis=-1)
    att_a = _make_dense_attention((QK_NOPE + QK_ROPE) ** -0.5, "att_a")
    o_a = att_a(qa, ka, kva[..., QK_NOPE:]).reshape(s, WIDTH_A)

    qb = _rope(p_b[:, 0], *rope64)
    kb = _rope(p_b[:, 1], *rope64)
    vb = p_b[:, 2]
    outs, lses = [], []
    for _, dil in DILATED_PAIRS:
        o, lse = _make_window_attention("band", HEAD_DIM ** -0.5, dil, "att_b%d" % dil)(qb, kb, vb, None)
        outs.append(o)
        lses.append(lse)
    wgt = jax.nn.softmax(jnp.stack(lses, axis=-1), axis=-1)
    o_b = jnp.sum(jnp.stack(outs, axis=-1) * wgt[:, :, None, :], axis=-1).reshape(s, WIDTH_B)

    att_c = _make_window_attention("natten", HEAD_DIM ** -0.5, 1, "att_c")
    o_c, _ = att_c(p_c[:, 0], p_c[:, 1], p_c[:, 2], _na_bias_tiles(lw["rpb"]))
    o_c = o_c.reshape(s, WIDTH_C)

    x = _make_mix_out("out")(x, o_a, o_b, o_c, lw["out_norm_a"], lw["out_norm_b"], lw["out_norm_c"], lw["w_out"])
    return _make_mlp("mlp")(x, lw["g_mlp"], lw["w_mlp_in"], lw["w_mlp_out"])


def _place():
    return lax.axis_index("x"), lax.axis_index("y"), lax.axis_index("c")


def _all_gather(block, *, name):
    r, c_dim = block.shape

    def body(x_ref, out_ref, send_sems, recv_sems, local_sem):
        x, y, c = _place()
        me, sibling = (x, y, c), (x, y, 1 - c)
        chips = [(1 - x, y), (x, 1 - y), (1 - x, 1 - y)]

        def slot(px, py, pc):
            return out_ref.at[4 * px + 2 * py + pc]

        def copy(k, blk, to, src=None):
            return pltpu.make_async_remote_copy(
                src_ref=slot(*blk) if src is None else src, dst_ref=slot(*blk),
                send_sem=send_sems.at[k], recv_sem=recv_sems.at[k], device_id=to, device_id_type=MESH)

        mine = pltpu.make_async_copy(x_ref, slot(*me), local_sem)
        mine.start()
        first = [copy(0, me, sibling, src=x_ref)]
        first += [copy(1 + j, me, (*chip, c), src=x_ref) for j, chip in enumerate(chips)]
        for cp in first:
            cp.start()
        passed = [copy(4 + j, (*chip, c), sibling) for j, chip in enumerate(chips)]
        for j, chip in enumerate(chips):
            copy(1 + j, (*chip, c), me).wait_recv()
            passed[j].start()
        copy(0, sibling, me).wait_recv()
        for j, chip in enumerate(chips):
            copy(4 + j, (*chip, 1 - c), me).wait_recv()
        for cp in first + passed:
            cp.wait_send()
        mine.wait()

    return pl.pallas_call(
        body, name=name, out_shape=jax.ShapeDtypeStruct((N_DEV, r, c_dim), block.dtype),
        in_specs=[pl.BlockSpec(memory_space=pl.ANY)], out_specs=pl.BlockSpec(memory_space=pl.ANY),
        scratch_shapes=[pltpu.SemaphoreType.DMA((7,)), pltpu.SemaphoreType.DMA((7,)), pltpu.SemaphoreType.DMA(())],
    )(block)


def _all_to_all(chunks, *, name):
    def body(in_ref, out_ref, send_sems, recv_sems, local_sem):
        x, y, c = _place()
        me = 4 * x + 2 * y + c
        mine = pltpu.make_async_copy(in_ref.at[me], out_ref.at[me], local_sem)
        mine.start()
        peers = []
        for k in range(1, N_DEV):
            px = 1 - x if k & 4 else x
            py = 1 - y if k & 2 else y
            pc = 1 - c if k & 1 else c
            peers.append((px, py, pc))

        def copy(k, peer):
            pid = 4 * peer[0] + 2 * peer[1] + peer[2]
            return pltpu.make_async_remote_copy(
                src_ref=in_ref.at[pid], dst_ref=out_ref.at[me], send_sem=send_sems.at[k], recv_sem=recv_sems.at[k],
                device_id=peer, device_id_type=MESH)

        def landing(k, peer):
            pid = 4 * peer[0] + 2 * peer[1] + peer[2]
            return pltpu.make_async_remote_copy(
                src_ref=in_ref.at[pid], dst_ref=out_ref.at[pid], send_sem=send_sems.at[k], recv_sem=recv_sems.at[k],
                device_id=peer, device_id_type=MESH)

        sends = [copy(k, peer) for k, peer in enumerate(peers)]
        for cp in sends:
            cp.start()
        for k, peer in enumerate(peers):
            landing(k, peer).wait_recv()
        for cp in sends:
            cp.wait_send()
        mine.wait()

    return pl.pallas_call(
        body, name=name, out_shape=jax.ShapeDtypeStruct(chunks.shape, chunks.dtype),
        in_specs=[pl.BlockSpec(memory_space=pl.ANY)], out_specs=pl.BlockSpec(memory_space=pl.ANY),
        scratch_shapes=[pltpu.SemaphoreType.DMA((7,)), pltpu.SemaphoreType.DMA((7,)), pltpu.SemaphoreType.DMA(())],
    )(chunks)


def _adamw(parts, w, m, v, *, name):
    r, c_dim = w.shape
    tr = r
    for cand in range(ADAM_MAX_ROWS, PACK_ROW_ALIGN - 1, -PACK_ROW_ALIGN):
        if r > ADAM_MAX_ROWS and r % cand == 0:
            tr = cand
            break

    def kern(p_ref, w_ref, m_ref, v_ref, g_ref, d_ref, nm_ref, nv_ref):
        g = p_ref[0].astype(F32)
        for i in range(1, N_DEV):
            g = g + p_ref[i].astype(F32)
        nm = ADAM_B1 * m_ref[...] + (1.0 - ADAM_B1) * g
        nv = ADAM_B2 * v_ref[...] + (1.0 - ADAM_B2) * (g * g)
        m_hat = nm / (1.0 - ADAM_B1 ** ADAM_STEP)
        v_hat = nv / (1.0 - ADAM_B2 ** ADAM_STEP)
        g_ref[...] = g
        d_ref[...] = -ADAM_LR * (m_hat / (jnp.sqrt(v_hat) + ADAM_EPS) + ADAM_WD * w_ref[...])
        nm_ref[...] = nm
        nv_ref[...] = nv

    row = pl.BlockSpec((tr, c_dim), lambda i: (i, 0))
    return pl.pallas_call(
        kern, name=name, out_shape=tuple(jax.ShapeDtypeStruct((r, c_dim), F32) for _ in range(4)), grid=(r // tr,),
        in_specs=[pl.BlockSpec((N_DEV, tr, c_dim), lambda i: (0, i, 0)), row, row, row],
        out_specs=(row, row, row, row), compiler_params=_params(("parallel",)),
    )(parts, w, m, v)


SHARDED = (("w_in", 1), ("w_out", 0), ("w_mlp_in", 1), ("w_mlp_out", 0), ("w_ukv", 1), ("w_uq", 1))
REPLICATED = ("g_mix", "q_norm", "kv_norm", "rpb", "out_norm_a", "out_norm_b", "out_norm_c", "g_mlp", "g_final")
WEIGHTS = ("g_mix", "w_in", "q_norm", "w_uq", "kv_norm", "w_ukv", "rpb", "out_norm_a", "out_norm_b", "out_norm_c",
           "w_out", "g_mlp", "w_mlp_in", "w_mlp_out", "g_final")


def _pack_rows(arrs, dtype):
    parts = [a.reshape(-1, PACK_COLS).astype(dtype) for a in arrs]
    rows = sum(p.shape[0] for p in parts)
    pad = -rows % PACK_ROW_ALIGN
    if pad:
        parts.append(jnp.zeros((pad, PACK_COLS), dtype))
    return jnp.concatenate(parts, axis=0)


def _unpack_rows(packed, shapes):
    out, off = [], 0
    for shp in shapes:
        rows = int(np.prod(shp)) // PACK_COLS
        out.append(packed[..., off:off + rows, :].reshape(packed.shape[:-2] + tuple(shp)))
        off += rows
    return out


def _pack_flat(arrs):
    flat = jnp.concatenate([a.reshape(-1) for a in arrs])
    pad = -flat.shape[0] % (8 * PACK_COLS)
    return jnp.pad(flat, (0, pad)).reshape(-1, PACK_COLS)


def _unpack_flat(packed, shapes):
    flat = packed.reshape(-1)
    out, off = [], 0
    for shp in shapes:
        size = int(np.prod(shp))
        out.append(flat[off:off + size].reshape(shp))
        off += size
    return out


def _gathered_to_full(g, axis):
    n, l, a, b = g.shape
    if axis == 0:
        return jnp.transpose(g, (1, 0, 2, 3)).reshape(l, n * a, b)
    return jnp.transpose(g, (1, 2, 0, 3)).reshape(l, a, n * b)


def _full_to_chunks(w, axis):
    l, a, b = w.shape
    if axis == 0:
        return jnp.transpose(w.reshape(l, N_DEV, a // N_DEV, b), (1, 0, 2, 3))
    return jnp.transpose(w.reshape(l, a, N_DEV, b // N_DEV), (2, 0, 1, 3))


def kernel(x, g_mix, w_in, q_norm, w_uq, kv_norm, w_ukv, rpb, out_norm_a, out_norm_b, out_norm_c, w_out, g_mlp, w_mlp_in, w_mlp_out, g_final, loss_target, m_g_mix, m_w_in, m_q_norm, m_w_uq, m_kv_norm, m_w_ukv, m_rpb, m_out_norm_a, m_out_norm_b, m_out_norm_c, m_w_out, m_g_mlp, m_w_mlp_in, m_w_mlp_out, m_g_final, v_g_mix, v_w_in, v_q_norm, v_w_uq, v_kv_norm, v_w_ukv, v_rpb, v_out_norm_a, v_out_norm_b, v_out_norm_c, v_w_out, v_g_mlp, v_w_mlp_in, v_w_mlp_out, v_g_final):
    given = dict(locals())
    w = {n: given[n] for n in WEIGHTS}
    m = {n: given["m_" + n] for n in WEIGHTS}
    v = {n: given["v_" + n] for n in WEIGHTS}
    s = x.shape[1]
    depth = g_mix.shape[0]
    shard_shapes = [w[n].shape for n, _ in SHARDED]

    gathered = _all_gather(_pack_rows([w[n] for n, _ in SHARDED], BF16), name="gather_weights")
    full = {}
    for (n, axis), g in zip(SHARDED, _unpack_rows(gathered, shard_shapes)):
        full[n] = _gathered_to_full(g, axis).astype(F32)

    layer_w = [dict({n: full[n][l] for n, _ in SHARDED}, **{n: w[n][l] for n in REPLICATED if n != "g_final"})
               for l in range(depth)]
    rope32 = _rope_tables(s, QK_ROPE)
    rope64 = _rope_tables(s, HEAD_DIM)

    def trunk(x0, lws):
        h = x0
        for lw in lws:
            h = _layer(h, lw, rope32, rope64)
        return h

    x_out, pullback = jax.vjp(trunk, x[0], layer_w)
    loss_local, dx_out, d_g_final = _loss_head(x_out, g_final, loss_target[0], name="loss_head")
    dx0, d_layers = pullback(dx_out)
    loss = lax.psum(loss_local, AXES)

    grads_local = {n: jnp.stack([d[n] for d in d_layers]) for n in layer_w[0]}
    grads_local["g_final"] = d_g_final

    chunks = [_full_to_chunks(grads_local[n], axis).reshape(N_DEV, -1, PACK_COLS).astype(BF16)
              for n, axis in SHARDED]
    rows = sum(c.shape[1] for c in chunks)
    pad = -rows % PACK_ROW_ALIGN
    if pad:
        chunks.append(jnp.zeros((N_DEV, pad, PACK_COLS), BF16))
    parts = _all_to_all(jnp.concatenate(chunks, axis=1), name="scatter_grads")
    big = _adamw(parts, _pack_rows([w[n] for n, _ in SHARDED], F32), _pack_rows([m[n] for n, _ in SHARDED], F32),
                 _pack_rows([v[n] for n, _ in SHARDED], F32), name="adamw_sharded")
    big = [_unpack_rows(o, shard_shapes) for o in big]

    rep_shapes = [w[n].shape for n in REPLICATED]
    rep_parts = _all_gather(_pack_flat([grads_local[n] for n in REPLICATED]), name="gather_small_grads")
    small = _adamw(rep_parts, _pack_flat([w[n] for n in REPLICATED]), _pack_flat([m[n] for n in REPLICATED]),
                   _pack_flat([v[n] for n in REPLICATED]), name="adamw_replicated")
    small = [_unpack_flat(o, rep_shapes) for o in small]

    result = {}
    for kind, idx in (("grad", 0), ("delta", 1), ("new_m", 2), ("new_v", 3)):
        for j, (n, _) in enumerate(SHARDED):
            result[kind + "_" + n] = big[idx][j]
        for j, n in enumerate(REPLICATED):
            result[kind + "_" + n] = small[idx][j]
    outs = [loss, dx0[None]]
    for kind in ("grad", "delta", "new_m", "new_v"):
        outs += [result[kind + "_" + n] for n in WEIGHTS]
    return tuple(outs)
```

```python
import functools

import numpy as np
import jax
import jax.numpy as jnp
from jax import lax
from jax.experimental import pallas as pl
from jax.experimental.pallas import tpu as pltpu

F32 = jnp.float32
BF16 = jnp.bfloat16

D_MODEL = 1024
DEPTH = 4
HEAD_DIM = 64
HEADS_A = 6
HEADS_B = 6
HEADS_C = 4
Q_LORA = 256
KV_LORA = 128
QK_NOPE = 64
QK_ROPE = 32
V_DIM_A = 64
DILATED_PAIRS = ((128, 1), (512, 4), (2048, 16))
BAND_HALF = 64
GRID_W = 64
NA_ROWS = 8
NA_COLS = 16
D_FF = 4096
ROPE_THETA = 10000.0
NORM_EPS = 1e-6
NEG_INF = -1e30

COLS_A = Q_LORA + KV_LORA + QK_ROPE
COLS_B = 3 * HEADS_B * HEAD_DIM
COLS_C = 3 * HEADS_C * HEAD_DIM
WIDTH_A = HEADS_A * V_DIM_A
WIDTH_B = HEADS_B * HEAD_DIM
WIDTH_C = HEADS_C * HEAD_DIM

ADAM_LR = 0.001
ADAM_B1 = 0.9
ADAM_B2 = 0.999
ADAM_EPS = 1e-08
ADAM_WD = 0.01
ADAM_STEP = 10

N_DEV = 8
AXES = ("x", "y", "c")
MESH = pl.DeviceIdType.MESH

V7X_VMEM_LIMIT = 48 * 1024 * 1024
MM_VMEM_BUDGET = 32 * 1024 * 1024
LANES = 128
PACK_COLS = 1024
PACK_ROW_ALIGN = 16
ADAM_MAX_ROWS = 160
DENSE_TILE_FWD = 512
DENSE_TQ = 256
DENSE_TK_BWD = 1024
WIN_TILES_PER_STEP = 4
BAND_TILE = 256
NA_TILE_ROWS = 4


def _params(sem=None):
    return pltpu.CompilerParams(dimension_semantics=sem, vmem_limit_bytes=V7X_VMEM_LIMIT)


def _tile(dim, pref):
    if dim <= pref:
        return dim
    for t in range(pref - pref % LANES, LANES - 1, -LANES):
        if dim % t == 0:
            return t
    return dim


def _mm(a, b, *, ta=False, tb=False, out_dtype=F32, epilogue=None, extra=None, splits=None, name):
    if ta:
        k_dim, m_dim = a.shape
    else:
        m_dim, k_dim = a.shape
    n_dim = b.shape[0] if tb else b.shape[1]
    tn = _tile(n_dim, 1024) if n_dim % LANES == 0 else n_dim
    tk = _tile(k_dim, 1024 if not ta else 512) if k_dim % LANES == 0 else k_dim
    if epilogue == "split":
        tn = n_dim
        assert sum(splits) == n_dim and all(off % LANES == 0 for off in np.cumsum(splits)[:-1])
    n_out = 2 if epilogue == "relu2" else 1
    for tm in (_tile(m_dim, 1024), _tile(m_dim, 512), _tile(m_dim, 256)):
        blocks = (tm * tk * a.dtype.itemsize + tk * tn * b.dtype.itemsize
                  + (tm * tn * 4 if extra is not None else 0) + n_out * tm * tn * 4)
        if 2 * blocks + tm * tn * 4 <= MM_VMEM_BUDGET:
            break
    nk = k_dim // tk
    a_spec = (pl.BlockSpec((tk, tm), lambda i, j, k: (k, i)) if ta
              else pl.BlockSpec((tm, tk), lambda i, j, k: (i, k)))
    b_spec = (pl.BlockSpec((tn, tk), lambda i, j, k: (j, k)) if tb
              else pl.BlockSpec((tk, tn), lambda i, j, k: (k, j)))
    o_spec = pl.BlockSpec((tm, tn), lambda i, j, k: (i, j))
    dims = (((0 if ta else 1,), (1 if tb else 0,)), ((), ()))
    in_specs = [a_spec, b_spec]
    operands = [a, b]
    if epilogue in ("drelu2", "add"):
        in_specs.append(o_spec)
        operands.append(extra)
    if epilogue == "relu2":
        out_shape = (jax.ShapeDtypeStruct((m_dim, n_dim), F32), jax.ShapeDtypeStruct((m_dim, n_dim), BF16))
        out_specs = (o_spec, o_spec)
    elif epilogue == "split":
        out_shape = tuple(jax.ShapeDtypeStruct((m_dim, w), out_dtype) for w in splits)
        out_specs = tuple(pl.BlockSpec((tm, w), lambda i, j, k: (i, 0)) for w in splits)
    else:
        out_shape = jax.ShapeDtypeStruct((m_dim, n_dim), out_dtype)
        out_specs = o_spec

    def kern(*refs):
        acc_ref = refs[-1]
        a_ref, b_ref = refs[0], refs[1]
        k = pl.program_id(2)

        @pl.when(k == 0)
        def _():
            acc_ref[...] = jnp.zeros_like(acc_ref)

        acc_ref[...] += lax.dot_general(a_ref[...].astype(BF16), b_ref[...].astype(BF16), dims,
                                        preferred_element_type=F32)

        @pl.when(k == nk - 1)
        def _():
            acc = acc_ref[...]
            if epilogue == "relu2":
                refs[2][...] = acc
                r = jnp.maximum(acc, 0.0)
                refs[3][...] = (r * r).astype(BF16)
            elif epilogue == "drelu2":
                refs[3][...] = (acc * (2.0 * jnp.maximum(refs[2][...], 0.0))).astype(out_dtype)
            elif epilogue == "add":
                refs[3][...] = (acc + refs[2][...]).astype(out_dtype)
            elif epilogue == "split":
                off = 0
                for part, w in enumerate(splits):
                    refs[2 + part][...] = acc[:, off:off + w].astype(out_dtype)
                    off += w
            else:
                refs[2][...] = acc.astype(out_dtype)

    return pl.pallas_call(
        kern, name=name, out_shape=out_shape, grid=(m_dim // tm, n_dim // tn, nk),
        in_specs=in_specs, out_specs=out_specs, scratch_shapes=[pltpu.VMEM((tm, tn), F32)],
        compiler_params=_params(("parallel", "parallel", "arbitrary")),
    )(*operands)


def _norm_fwd(x, g, *, out_dtype, name):
    s, w = x.shape
    ts = _tile(s, 512)

    def kern(x_ref, g_ref, y_ref):
        xv = x_ref[...]
        r = lax.rsqrt(jnp.mean(xv * xv, axis=-1, keepdims=True) + NORM_EPS)
        y_ref[...] = (xv * r * g_ref[...]).astype(out_dtype)

    return pl.pallas_call(
        kern, name=name, out_shape=jax.ShapeDtypeStruct((s, w), out_dtype), grid=(s // ts,),
        in_specs=[pl.BlockSpec((ts, w), lambda i: (i, 0)), pl.BlockSpec((1, w), lambda i: (0, 0))],
        out_specs=pl.BlockSpec((ts, w), lambda i: (i, 0)),
        compiler_params=_params(("parallel",)),
    )(x, g.reshape(1, w))


def _norm_bwd(x, g, dy, *, name):
    s, w = x.shape
    ts = _tile(s, 512)

    def kern(x_ref, g_ref, dy_ref, dx_ref, dg_ref):
        @pl.when(pl.program_id(0) == 0)
        def _():
            dg_ref[...] = jnp.zeros_like(dg_ref)

        xv = x_ref[...]
        dyv = dy_ref[...]
        r = lax.rsqrt(jnp.mean(xv * xv, axis=-1, keepdims=True) + NORM_EPS)
        xhat = xv * r
        dg_ref[...] += jnp.sum(dyv * xhat, axis=0, keepdims=True)
        dxhat = dyv * g_ref[...]
        dx_ref[...] = r * (dxhat - xhat * jnp.mean(dxhat * xhat, axis=-1, keepdims=True))

    dx, dg = pl.pallas_call(
        kern, name=name,
        out_shape=(jax.ShapeDtypeStruct((s, w), F32), jax.ShapeDtypeStruct((1, w), F32)), grid=(s // ts,),
        in_specs=[pl.BlockSpec((ts, w), lambda i: (i, 0)), pl.BlockSpec((1, w), lambda i: (0, 0)),
                  pl.BlockSpec((ts, w), lambda i: (i, 0))],
        out_specs=(pl.BlockSpec((ts, w), lambda i: (i, 0)), pl.BlockSpec((1, w), lambda i: (0, 0))),
        compiler_params=_params(("arbitrary",)),
    )(x, g.reshape(1, w), dy)
    return dx, dg.reshape(w)


def _loss_head(x, g, target, *, name):
    s, w = x.shape
    ts = _tile(s, 512)

    def kern(x_ref, g_ref, t_ref, loss_ref, dx_ref, dg_ref):
        @pl.when(pl.program_id(0) == 0)
        def _():
            dg_ref[...] = jnp.zeros_like(dg_ref)
            loss_ref[...] = jnp.zeros_like(loss_ref)

        xv = x_ref[...]
        gv = g_ref[...]
        r = lax.rsqrt(jnp.mean(xv * xv, axis=-1, keepdims=True) + NORM_EPS)
        xhat = xv * r
        err = xhat * gv - t_ref[...]
        loss_ref[...] += 0.5 * jnp.sum(jnp.mean(err * err, axis=-1, keepdims=True))
        dyv = err * (1.0 / w)
        dg_ref[...] += jnp.sum(dyv * xhat, axis=0, keepdims=True)
        dxhat = dyv * gv
        dx_ref[...] = r * (dxhat - xhat * jnp.mean(dxhat * xhat, axis=-1, keepdims=True))

    loss, dx, dg = pl.pallas_call(
        kern, name=name,
        out_shape=(jax.ShapeDtypeStruct((1, LANES), F32), jax.ShapeDtypeStruct((s, w), F32),
                   jax.ShapeDtypeStruct((1, w), F32)),
        grid=(s // ts,),
        in_specs=[pl.BlockSpec((ts, w), lambda i: (i, 0)), pl.BlockSpec((1, w), lambda i: (0, 0)),
                  pl.BlockSpec((ts, w), lambda i: (i, 0))],
        out_specs=(pl.BlockSpec((1, LANES), lambda i: (0, 0)), pl.BlockSpec((ts, w), lambda i: (i, 0)),
                   pl.BlockSpec((1, w), lambda i: (0, 0))),
        compiler_params=_params(("arbitrary",)),
    )(x, g.reshape(1, w), target)
    return loss[0, 0], dx, dg.reshape(w)


def _dense_fwd(q, k, v, *, scale, name):
    g_dim, n, dq = q.shape
    dv = v.shape[-1]
    tq = tk = _tile(n, DENSE_TILE_FWD)
    nk = n // tk

    def kern(q_ref, k_ref, v_ref, o_ref, lse_ref):
        qb = q_ref[0]

        def body(j, carry):
            m, l, acc = carry
            start = pl.multiple_of(j * tk, tk)
            kb = k_ref[0, pl.ds(start, tk), :]
            vb = v_ref[0, pl.ds(start, tk), :]
            s = lax.dot_general(qb, kb, (((1,), (1,)), ((), ())), preferred_element_type=F32) * scale
            m_new = jnp.maximum(m, jnp.max(s, axis=-1, keepdims=True))
            p = jnp.exp(s - m_new)
            alpha = jnp.exp(m - m_new)
            l = alpha * l + jnp.sum(p, axis=-1, keepdims=True)
            acc = alpha * acc + jnp.dot(p.astype(BF16), vb, preferred_element_type=F32)
            return m_new, l, acc

        init = (jnp.full((tq, 1), NEG_INF, F32), jnp.zeros((tq, 1), F32), jnp.zeros((tq, dv), F32))
        m, l, acc = lax.fori_loop(0, nk, body, init)
        o_ref[0] = acc / l
        lse_ref[0] = m + jnp.log(l)

    return pl.pallas_call(
        kern, name=name,
        out_shape=(jax.ShapeDtypeStruct((g_dim, n, dv), F32), jax.ShapeDtypeStruct((g_dim, n, 1), F32)),
        grid=(g_dim, n // tq),
        in_specs=[pl.BlockSpec((1, tq, dq), lambda g, i: (g, i, 0)), pl.BlockSpec((1, n, dq), lambda g, i: (g, 0, 0)),
                  pl.BlockSpec((1, n, dv), lambda g, i: (g, 0, 0))],
        out_specs=(pl.BlockSpec((1, tq, dv), lambda g, i: (g, i, 0)), pl.BlockSpec((1, tq, 1), lambda g, i: (g, i, 0))),
        compiler_params=_params(("parallel", "arbitrary")),
    )(q, k, v)


def _dense_bwd(q_t, k, k_t, v, o_t, lse, do_t, *, scale, name):
    g_dim, dq, n = q_t.shape
    dv = v.shape[-1]
    tq = _tile(n, DENSE_TQ)
    tk = _tile(n, DENSE_TK_BWD)
    nk = n // tk
    nt = (((1,), (1,)), ((), ()))

    def kern(q_ref, k_ref, kt_ref, v_ref, o_ref, lse_ref, do_ref, dq_ref, dk_ref, dv_ref):
        @pl.when(pl.program_id(1) == 0)
        def _():
            dk_ref[...] = jnp.zeros_like(dk_ref)
            dv_ref[...] = jnp.zeros_like(dv_ref)

        qb = q_ref[0]
        dob = do_ref[0]
        delta = jnp.sum(dob * o_ref[0], axis=0, keepdims=True)
        dob16 = dob.astype(BF16)
        lse_v = lse_ref[0]

        def body(j, dq_acc):
            start = pl.multiple_of(j * tk, tk)
            s = jnp.dot(k_ref[0, pl.ds(start, tk), :], qb, preferred_element_type=F32) * scale
            p = jnp.exp(s - lse_v)
            dp = jnp.dot(v_ref[0, pl.ds(start, tk), :], dob16, preferred_element_type=F32)
            ds16 = (p * (dp - delta) * scale).astype(BF16)
            dv_ref[0, pl.ds(start, tk), :] += lax.dot_general(p.astype(BF16), dob16, nt, preferred_element_type=F32)
            dk_ref[0, pl.ds(start, tk), :] += lax.dot_general(ds16, qb, nt, preferred_element_type=F32)
            return dq_acc + jnp.dot(kt_ref[0, :, pl.ds(start, tk)], ds16, preferred_element_type=F32)

        dq_ref[0] = lax.fori_loop(0, nk, body, jnp.zeros((dq, tq), F32))

    qt_spec = pl.BlockSpec((1, dq, tq), lambda g, i: (g, 0, i))
    ot_spec = pl.BlockSpec((1, dv, tq), lambda g, i: (g, 0, i))
    l_spec = pl.BlockSpec((1, 1, tq), lambda g, i: (g, 0, i))
    k_spec = pl.BlockSpec((1, n, dq), lambda g, i: (g, 0, 0))
    kt_spec = pl.BlockSpec((1, dq, n), lambda g, i: (g, 0, 0))
    v_spec = pl.BlockSpec((1, n, dv), lambda g, i: (g, 0, 0))
    return pl.pallas_call(
        kern, name=name,
        out_shape=(jax.ShapeDtypeStruct((g_dim, dq, n), F32), jax.ShapeDtypeStruct((g_dim, n, dq), F32),
                   jax.ShapeDtypeStruct((g_dim, n, dv), F32)),
        grid=(g_dim, n // tq), in_specs=[qt_spec, k_spec, kt_spec, v_spec, ot_spec, l_spec, ot_spec],
        out_specs=(qt_spec, k_spec, v_spec), compiler_params=_params(("parallel", "arbitrary")),
    )(q_t, k, k_t, v, o_t, lse, do_t)


def _make_dense_attention(scale, name):
    @jax.custom_vjp
    def att(q, k, v):
        return att_fwd(q, k, v)[0]

    def att_fwd(q, k, v):
        q16 = jnp.transpose(q.astype(BF16), (1, 0, 2))
        k16 = jnp.transpose(k.astype(BF16), (1, 0, 2))
        v16 = jnp.transpose(v.astype(BF16), (1, 0, 2))
        o, lse = _dense_fwd(q16, k16, v16, scale=scale, name=name + "_fwd")
        return jnp.transpose(o, (1, 0, 2)), (q16, k16, v16, o, lse)

    def att_bwd(res, do):
        q16, k16, v16, o, lse = res
        dq_t, dk, dv_ = _dense_bwd(jnp.transpose(q16, (0, 2, 1)), k16, jnp.transpose(k16, (0, 2, 1)), v16,
                                   jnp.transpose(o, (0, 2, 1)), jnp.transpose(lse, (0, 2, 1)),
                                   jnp.transpose(do, (1, 2, 0)), scale=scale, name=name + "_bwd")
        return jnp.transpose(dq_t, (2, 0, 1)), jnp.transpose(dk, (1, 0, 2)), jnp.transpose(dv_, (1, 0, 2))

    att.defvjp(att_fwd, att_bwd)
    return att


def _window(mode, n):
    if mode == "band":
        tq = min(BAND_TILE, n)
        return tq, BAND_HALF, tq + 2 * BAND_HALF
    tq = NA_TILE_ROWS * GRID_W
    front = (NA_ROWS // 2) * GRID_W
    return tq, front, tq + NA_ROWS * GRID_W


def _window_mask(mode, i0, tq, front, span, n):
    q_pos = i0 + lax.broadcasted_iota(jnp.int32, (tq, 1), 0)
    k_pos = i0 - front + lax.broadcasted_iota(jnp.int32, (1, span), 1)
    if mode == "band":
        diff = k_pos - q_pos
        return (diff <= BAND_HALF) & (diff >= -BAND_HALF) & (k_pos >= 0) & (k_pos < n)
    rows = n // GRID_W
    shift = GRID_W.bit_length() - 1
    r_start = jnp.clip((q_pos >> shift) - NA_ROWS // 2, 0, rows - NA_ROWS)
    c_start = jnp.clip((q_pos & (GRID_W - 1)) - NA_COLS // 2, 0, GRID_W - NA_COLS)
    kr = k_pos >> shift
    kc = k_pos & (GRID_W - 1)
    return (kr >= r_start) & (kr < r_start + NA_ROWS) & (kc >= c_start) & (kc < c_start + NA_COLS)


def _win_fwd(q, k, v, bias, *, mode, scale, name):
    g_dim, n, d = q.shape
    tq, front, span = _window(mode, n)
    sub = min(WIN_TILES_PER_STEP, n // tq)
    n_pad = k.shape[1]

    def kern(*refs):
        if mode == "natten":
            q_ref, k_ref, v_ref, b_ref, o_ref, lse_ref = refs
        else:
            q_ref, k_ref, v_ref, o_ref, lse_ref = refs
        for t in range(sub):
            rows = pl.ds(t * tq, tq)
            i0 = pl.multiple_of(pl.program_id(1) * (sub * tq) + t * tq, tq)
            kb = k_ref[0, pl.ds(i0, span), :]
            vb = v_ref[0, pl.ds(i0, span), :]
            s = lax.dot_general(q_ref[0, rows, :], kb, (((1,), (1,)), ((), ())), preferred_element_type=F32) * scale
            if mode == "natten":
                s = s + b_ref[0]
            s = jnp.where(_window_mask(mode, i0, tq, front, span, n), s, NEG_INF)
            m = jnp.max(s, axis=-1, keepdims=True)
            p = jnp.exp(s - m)
            l = jnp.sum(p, axis=-1, keepdims=True)
            o_ref[0, rows, :] = jnp.dot(p.astype(BF16), vb, preferred_element_type=F32) / l
            lse_ref[0, rows, :] = m + jnp.log(l)

    in_specs = [pl.BlockSpec((1, sub * tq, d), lambda g, i: (g, i, 0)),
                pl.BlockSpec((1, n_pad, d), lambda g, i: (g, 0, 0)), pl.BlockSpec((1, n_pad, d), lambda g, i: (g, 0, 0))]
    operands = [q, k, v]
    if mode == "natten":
        in_specs.append(pl.BlockSpec((1, tq, span), lambda g, i: (g, 0, 0)))
        operands.append(bias)
    return pl.pallas_call(
        kern, name=name,
        out_shape=(jax.ShapeDtypeStruct((g_dim, n, d), F32), jax.ShapeDtypeStruct((g_dim, n, 1), F32)),
        grid=(g_dim, n // (sub * tq)), in_specs=in_specs,
        out_specs=(pl.BlockSpec((1, sub * tq, d), lambda g, i: (g, i, 0)),
                   pl.BlockSpec((1, sub * tq, 1), lambda g, i: (g, i, 0))),
        compiler_params=_params(("parallel", "arbitrary")),
    )(*operands)


def _win_bwd(q, k, v, bias, o, lse, do, dlse, *, mode, scale, name):
    g_dim, n, d = q.shape
    tq, front, span = _window(mode, n)
    sub = min(WIN_TILES_PER_STEP, n // tq)
    n_pad = k.shape[1]

    def kern(*refs):
        if mode == "natten":
            q_ref, k_ref, v_ref, b_ref, o_ref, lse_ref, do_ref, dlse_ref, dq_ref, dk_ref, dv_ref, db_ref = refs
        else:
            q_ref, k_ref, v_ref, o_ref, lse_ref, do_ref, dlse_ref, dq_ref, dk_ref, dv_ref = refs

        @pl.when(pl.program_id(1) == 0)
        def _():
            dk_ref[...] = jnp.zeros_like(dk_ref)
            dv_ref[...] = jnp.zeros_like(dv_ref)
            if mode == "natten":
                db_ref[...] = jnp.zeros_like(db_ref)

        for t in range(sub):
            rows = pl.ds(t * tq, tq)
            i0 = pl.multiple_of(pl.program_id(1) * (sub * tq) + t * tq, tq)
            qb = q_ref[0, rows, :]
            kb = k_ref[0, pl.ds(i0, span), :]
            vb = v_ref[0, pl.ds(i0, span), :]
            dob = do_ref[0, rows, :]
            delta = jnp.sum(dob * o_ref[0, rows, :], axis=-1, keepdims=True) - dlse_ref[0, rows, :]
            dob16 = dob.astype(BF16)
            s = lax.dot_general(qb, kb, (((1,), (1,)), ((), ())), preferred_element_type=F32) * scale
            if mode == "natten":
                s = s + b_ref[0]
            p = jnp.where(_window_mask(mode, i0, tq, front, span, n), jnp.exp(s - lse_ref[0, rows, :]), 0.0)
            dp = lax.dot_general(dob16, vb, (((1,), (1,)), ((), ())), preferred_element_type=F32)
            ds = p * (dp - delta)
            if mode == "natten":
                db_ref[0] += ds
            ds16 = (ds * scale).astype(BF16)
            dv_ref[0, pl.ds(i0, span), :] += lax.dot_general(
                p.astype(BF16), dob16, (((0,), (0,)), ((), ())), preferred_element_type=F32)
            dk_ref[0, pl.ds(i0, span), :] += lax.dot_general(
                ds16, qb, (((0,), (0,)), ((), ())), preferred_element_type=F32)
            dq_ref[0, rows, :] = jnp.dot(ds16, kb, preferred_element_type=F32)

    q_spec = pl.BlockSpec((1, sub * tq, d), lambda g, i: (g, i, 0))
    k_spec = pl.BlockSpec((1, n_pad, d), lambda g, i: (g, 0, 0))
    l_spec = pl.BlockSpec((1, sub * tq, 1), lambda g, i: (g, i, 0))
    in_specs = [q_spec, k_spec, k_spec]
    operands = [q, k, v]
    out_shape = [jax.ShapeDtypeStruct((g_dim, n, d), F32), jax.ShapeDtypeStruct((g_dim, n_pad, d), F32),
                 jax.ShapeDtypeStruct((g_dim, n_pad, d), F32)]
    out_specs = [q_spec, k_spec, k_spec]
    if mode == "natten":
        b_spec = pl.BlockSpec((1, tq, span), lambda g, i: (g, 0, 0))
        in_specs.append(b_spec)
        operands.append(bias)
        out_shape.append(jax.ShapeDtypeStruct(bias.shape, F32))
        out_specs.append(b_spec)
    in_specs += [q_spec, l_spec, q_spec, l_spec]
    operands += [o, lse, do, dlse]
    return pl.pallas_call(
        kern, name=name, out_shape=tuple(out_shape), grid=(g_dim, n // (sub * tq)), in_specs=in_specs,
        out_specs=tuple(out_specs), compiler_params=_params(("parallel", "arbitrary")),
    )(*operands)


def _dilate(t, dil):
    s, h, d = t.shape
    return jnp.transpose(t.reshape(s // dil, dil, h, d), (1, 2, 0, 3)).reshape(dil * h, s // dil, d)


def _undilate(t, dil):
    gh, n, d = t.shape
    return jnp.transpose(t.reshape(dil, gh // dil, n, d), (2, 0, 1, 3)).reshape(n * dil, gh // dil, d)


def _make_window_attention(mode, scale, dil, name):
    @jax.custom_vjp
    def att(q, k, v, bias):
        return att_fwd(q, k, v, bias)[0]

    def att_fwd(q, k, v, bias):
        tq, front, span = _window(mode, q.shape[0] // dil)
        pad = ((0, 0), (front, span - tq - front), (0, 0))
        q16 = _dilate(q.astype(BF16), dil)
        k16 = jnp.pad(_dilate(k.astype(BF16), dil), pad)
        v16 = jnp.pad(_dilate(v.astype(BF16), dil), pad)
        o, lse = _win_fwd(q16, k16, v16, bias, mode=mode, scale=scale, name=name + "_fwd")
        return (_undilate(o, dil), _undilate(lse, dil)[..., 0]), (q16, k16, v16, bias, o, lse)

    def att_bwd(res, cts):
        q16, k16, v16, bias, o, lse = res
        do, dlse = cts
        n = q16.shape[1]
        front = _window(mode, n)[1]
        outs = _win_bwd(q16, k16, v16, bias, o, lse, _dilate(do, dil), _dilate(dlse[..., None], dil),
                        mode=mode, scale=scale, name=name + "_bwd")
        dbias = outs[3] if mode == "natten" else None
        return (_undilate(outs[0], dil), _undilate(outs[1][:, front:front + n], dil),
                _undilate(outs[2][:, front:front + n], dil), dbias)

    att.defvjp(att_fwd, att_bwd)
    return att


def _na_onehots():
    tq, front, span = _window("natten", 0)
    q_rows, k_rows, front_rows = tq // GRID_W, span // GRID_W, front // GRID_W
    e_r = np.zeros((q_rows, k_rows, 2 * NA_ROWS - 1), np.float32)
    for qr in range(q_rows):
        for kr in range(k_rows):
            a = kr - front_rows - qr + NA_ROWS - 1
            if 0 <= a < 2 * NA_ROWS - 1:
                e_r[qr, kr, a] = 1.0
    e_c = np.zeros((GRID_W, GRID_W, 2 * NA_COLS - 1), np.float32)
    for qc in range(GRID_W):
        for kc in range(GRID_W):
            b = kc - qc + NA_COLS - 1
            if 0 <= b < 2 * NA_COLS - 1:
                e_c[qc, kc, b] = 1.0
    return e_r, e_c


def _na_bias_tiles(rpb):
    tq, _, span = _window("natten", 0)
    e_r, e_c = _na_onehots()
    t = jnp.einsum("ikA,xyB,hAB->hixky", e_r, e_c, rpb, precision=lax.Precision.HIGHEST)
    return t.reshape(rpb.shape[0], tq, span)


def _make_norm_linear(name):
    @jax.custom_vjp
    def f(x, g, w_grad, w):
        return f_fwd(x, g, w_grad, w)[0]

    def f_fwd(x, g, w_grad, w):
        h = _norm_fwd(x, g, out_dtype=BF16, name=name + "_norm")
        return _mm(h, w, name=name + "_mm"), (x, g, w, h)

    def f_bwd(res, dy):
        x, g, w, h = res
        dh = _mm(dy, w, tb=True, name=name + "_dh")
        dw = _mm(h, dy, ta=True, name=name + "_dw")
        dx, dg = _norm_bwd(x, g, dh, name=name + "_dnorm")
        return dx, dg, dw, jnp.zeros_like(w)

    f.defvjp(f_fwd, f_bwd)
    return f


def _make_in_proj(name):
    splits = (COLS_B, COLS_C, COLS_A)
    offs = (0, COLS_B, COLS_B + COLS_C)

    @jax.custom_vjp
    def f(x, g, w_grad, w):
        return f_fwd(x, g, w_grad, w)[0]

    def f_fwd(x, g, w_grad, w):
        h = _norm_fwd(x, g, out_dtype=BF16, name=name + "_norm")
        return _mm(h, w, epilogue="split", splits=splits, name=name + "_mm"), (x, g, w, h)

    def f_bwd(res, dys):
        x, g, w, h = res
        dh = None
        dws = []
        for dy, off, width, tag in zip(dys, offs, splits, "bca"):
            w_part = w[:, off:off + width]
            dh = _mm(dy, w_part, tb=True, epilogue=None if dh is None else "add", extra=dh, name=name + "_dh_" + tag)
            dws.append(_mm(h, dy, ta=True, name=name + "_dw_" + tag))
        dx, dg = _norm_bwd(x, g, dh, name=name + "_dnorm")
        return dx, dg, jnp.concatenate([dws[2], dws[0], dws[1]], axis=1), jnp.zeros_like(w)

    f.defvjp(f_fwd, f_bwd)
    return f


def _make_mix_out(name):
    @jax.custom_vjp
    def f(x, oa, ob, oc, ga, gb, gc, w_grad, w):
        return f_fwd(x, oa, ob, oc, ga, gb, gc, w_grad, w)[0]

    def f_fwd(x, oa, ob, oc, ga, gb, gc, w_grad, w):
        mixed = jnp.concatenate([
            _norm_fwd(oa, ga, out_dtype=BF16, name=name + "_norm_a"),
            _norm_fwd(ob, gb, out_dtype=BF16, name=name + "_norm_b"),
            _norm_fwd(oc, gc, out_dtype=BF16, name=name + "_norm_c")], axis=-1)
        y = _mm(mixed, w, epilogue="add", extra=x, name=name + "_mm")
        return y, (oa, ob, oc, ga, gb, gc, w, mixed)

    def f_bwd(res, dy):
        oa, ob, oc, ga, gb, gc, w, mixed = res
        dmixed = _mm(dy, w, tb=True, name=name + "_dmixed")
        dw = _mm(mixed, dy, ta=True, name=name + "_dw")
        doa, dga = _norm_bwd(oa, ga, dmixed[:, :WIDTH_A], name=name + "_dnorm_a")
        dob, dgb = _norm_bwd(ob, gb, dmixed[:, WIDTH_A:WIDTH_A + WIDTH_B], name=name + "_dnorm_b")
        doc, dgc = _norm_bwd(oc, gc, dmixed[:, WIDTH_A + WIDTH_B:], name=name + "_dnorm_c")
        return dy, doa, dob, doc, dga, dgb, dgc, dw, jnp.zeros_like(w)

    f.defvjp(f_fwd, f_bwd)
    return f


def _make_mlp(name):
    @jax.custom_vjp
    def f(x, g, w1_grad, w2_grad, w1, w2):
        return f_fwd(x, g, w1_grad, w2_grad, w1, w2)[0]

    def f_fwd(x, g, w1_grad, w2_grad, w1, w2):
        h = _norm_fwd(x, g, out_dtype=BF16, name=name + "_norm")
        u, a = _mm(h, w1, epilogue="relu2", name=name + "_up")
        y = _mm(a, w2, epilogue="add", extra=x, name=name + "_down")
        return y, (x, g, w1, w2, h, u, a)

    def f_bwd(res, dy):
        x, g, w1, w2, h, u, a = res
        du = _mm(dy, w2, tb=True, epilogue="drelu2", extra=u, out_dtype=BF16, name=name + "_du")
        dw2 = _mm(a, dy, ta=True, name=name + "_dw2")
        dw1 = _mm(h, du, ta=True, name=name + "_dw1")
        dh = _mm(du, w1, tb=True, name=name + "_dh")
        dx, dg = _norm_bwd(x, g, dh, name=name + "_dnorm")
        return dx + dy, dg, dw1, dw2, jnp.zeros_like(w1), jnp.zeros_like(w2)

    f.defvjp(f_fwd, f_bwd)
    return f


def _rope_tables(s, dim):
    half = dim // 2
    inv_freq = ROPE_THETA ** (-jnp.arange(half, dtype=F32) / half)
    ang = jnp.arange(s, dtype=F32)[:, None] * inv_freq[None, :]
    return jnp.cos(ang)[:, None, :], jnp.sin(ang)[:, None, :]


def _rope(x, cos, sin):
    half = x.shape[-1] // 2
    x1, x2 = x[..., :half], x[..., half:]
    return jnp.concatenate([x1 * cos - x2 * sin, x1 * sin + x2 * cos], axis=-1)


def _layer(x, lw, lw16, rope32, rope64):
    s = x.shape[0]
    p_b, p_c, p_a = _make_in_proj("in")(x, lw["g_mix"], lw["w_in"], lw16["w_in"])
    c_q = p_a[:, :Q_LORA]
    c_kv = p_a[:, Q_LORA:Q_LORA + KV_LORA]
    k_pe = p_a[:, Q_LORA + KV_LORA:]
    p_b = p_b.reshape(s, 3, HEADS_B, HEAD_DIM)
    p_c = p_c.reshape(s, 3, HEADS_C, HEAD_DIM)

    qa = _make_norm_linear("uq")(c_q, lw["q_norm"], lw["w_uq"], lw16["w_uq"]).reshape(
        s, HEADS_A, QK_NOPE + QK_ROPE)
    kva = _make_norm_linear("ukv")(c_kv, lw["kv_norm"], lw["w_ukv"], lw16["w_ukv"]).reshape(
        s, HEADS_A, QK_NOPE + V_DIM_A)
    k_pe = jnp.broadcast_to(_rope(k_pe[:, None, :], *rope32), (s, HEADS_A, QK_ROPE))
    qa = jnp.concatenate([qa[..., :QK_NOPE], _rope(qa[..., QK_NOPE:], *rope32)], axis=-1)
    ka = jnp.concatenate([kva[..., :QK_NOPE], k_pe], axis=-1)
    att_a = _make_dense_attention((QK_NOPE + QK_ROPE) ** -0.5, "att_a")
    o_a = att_a(qa, ka, kva[..., QK_NOPE:]).reshape(s, WIDTH_A)

    qb = _rope(p_b[:, 0], *rope64)
    kb = _rope(p_b[:, 1], *rope64)
    vb = p_b[:, 2]
    outs, lses = [], []
    for _, dil in DILATED_PAIRS:
        o, lse = _make_window_attention("band", HEAD_DIM ** -0.5, dil, "att_b%d" % dil)(qb, kb, vb, None)
        outs.append(o)
        lses.append(lse)
    wgt = jax.nn.softmax(jnp.stack(lses, axis=-1), axis=-1)
    o_b = jnp.sum(jnp.stack(outs, axis=-1) * wgt[:, :, None, :], axis=-1).reshape(s, WIDTH_B)

    att_c = _make_window_attention("natten", HEAD_DIM ** -0.5, 1, "att_c")
    o_c, _ = att_c(p_c[:, 0], p_c[:, 1], p_c[:, 2], _na_bias_tiles(lw["rpb"]))
    o_c = o_c.reshape(s, WIDTH_C)

    x = _make_mix_out("out")(x, o_a, o_b, o_c, lw["out_norm_a"], lw["out_norm_b"], lw["out_norm_c"],
                             lw["w_out"], lw16["w_out"])
    return _make_mlp("mlp")(x, lw["g_mlp"], lw["w_mlp_in"], lw["w_mlp_out"], lw16["w_mlp_in"], lw16["w_mlp_out"])


def _place():
    return lax.axis_index("x"), lax.axis_index("y"), lax.axis_index("c")


def _all_gather(block, *, name):
    r, c_dim = block.shape

    def body(x_ref, out_ref, send_sems, recv_sems, local_sem):
        x, y, c = _place()
        me, sibling = (x, y, c), (x, y, 1 - c)
        chips = [(1 - x, y), (x, 1 - y), (1 - x, 1 - y)]

        def slot(px, py, pc):
            return out_ref.at[4 * px + 2 * py + pc]

        def copy(k, blk, to, src=None):
            return pltpu.make_async_remote_copy(
                src_ref=slot(*blk) if src is None else src, dst_ref=slot(*blk),
                send_sem=send_sems.at[k], recv_sem=recv_sems.at[k], device_id=to, device_id_type=MESH)

        mine = pltpu.make_async_copy(x_ref, slot(*me), local_sem)
        mine.start()
        first = [copy(0, me, sibling, src=x_ref)]
        first += [copy(1 + j, me, (*chip, c), src=x_ref) for j, chip in enumerate(chips)]
        for cp in first:
            cp.start()
        passed = [copy(4 + j, (*chip, c), sibling) for j, chip in enumerate(chips)]
        for j, chip in enumerate(chips):
            copy(1 + j, (*chip, c), me).wait_recv()
            passed[j].start()
        copy(0, sibling, me).wait_recv()
        for j, chip in enumerate(chips):
            copy(4 + j, (*chip, 1 - c), me).wait_recv()
        for cp in first + passed:
            cp.wait_send()
        mine.wait()

    return pl.pallas_call(
        body, name=name, out_shape=jax.ShapeDtypeStruct((N_DEV, r, c_dim), block.dtype),
        in_specs=[pl.BlockSpec(memory_space=pl.ANY)], out_specs=pl.BlockSpec(memory_space=pl.ANY),
        scratch_shapes=[pltpu.SemaphoreType.DMA((7,)), pltpu.SemaphoreType.DMA((7,)), pltpu.SemaphoreType.DMA(())],
    )(block)


def _all_to_all(chunks, *, name):
    def body(in_ref, out_ref, send_sems, recv_sems, local_sem):
        x, y, c = _place()
        me = 4 * x + 2 * y + c
        mine = pltpu.make_async_copy(in_ref.at[me], out_ref.at[me], local_sem)
        mine.start()
        peers = []
        for k in range(1, N_DEV):
            px = 1 - x if k & 4 else x
            py = 1 - y if k & 2 else y
            pc = 1 - c if k & 1 else c
            peers.append((px, py, pc))

        def copy(k, peer):
            pid = 4 * peer[0] + 2 * peer[1] + peer[2]
            return pltpu.make_async_remote_copy(
                src_ref=in_ref.at[pid], dst_ref=out_ref.at[me], send_sem=send_sems.at[k], recv_sem=recv_sems.at[k],
                device_id=peer, device_id_type=MESH)

        def landing(k, peer):
            pid = 4 * peer[0] + 2 * peer[1] + peer[2]
            return pltpu.make_async_remote_copy(
                src_ref=in_ref.at[pid], dst_ref=out_ref.at[pid], send_sem=send_sems.at[k], recv_sem=recv_sems.at[k],
                device_id=peer, device_id_type=MESH)

        sends = [copy(k, peer) for k, peer in enumerate(peers)]
        for cp in sends:
            cp.start()
        for k, peer in enumerate(peers):
            landing(k, peer).wait_recv()
        for cp in sends:
            cp.wait_send()
        mine.wait()

    return pl.pallas_call(
        body, name=name, out_shape=jax.ShapeDtypeStruct(chunks.shape, chunks.dtype),
        in_specs=[pl.BlockSpec(memory_space=pl.ANY)], out_specs=pl.BlockSpec(memory_space=pl.ANY),
        scratch_shapes=[pltpu.SemaphoreType.DMA((7,)), pltpu.SemaphoreType.DMA((7,)), pltpu.SemaphoreType.DMA(())],
    )(chunks)


def _adamw(parts, w, m, v, *, name):
    r, c_dim = w.shape
    tr = r
    for cand in range(ADAM_MAX_ROWS, PACK_ROW_ALIGN - 1, -PACK_ROW_ALIGN):
        if r > ADAM_MAX_ROWS and r % cand == 0:
            tr = cand
            break

    def kern(p_ref, w_ref, m_ref, v_ref, g_ref, d_ref, nm_ref, nv_ref):
        g = p_ref[0].astype(F32)
        for i in range(1, N_DEV):
            g = g + p_ref[i].astype(F32)
        nm = ADAM_B1 * m_ref[...] + (1.0 - ADAM_B1) * g
        nv = ADAM_B2 * v_ref[...] + (1.0 - ADAM_B2) * (g * g)
        m_hat = nm / (1.0 - ADAM_B1 ** ADAM_STEP)
        v_hat = nv / (1.0 - ADAM_B2 ** ADAM_STEP)
        g_ref[...] = g
        d_ref[...] = -ADAM_LR * (m_hat / (jnp.sqrt(v_hat) + ADAM_EPS) + ADAM_WD * w_ref[...])
        nm_ref[...] = nm
        nv_ref[...] = nv

    row = pl.BlockSpec((tr, c_dim), lambda i: (i, 0))
    return pl.pallas_call(
        kern, name=name, out_shape=tuple(jax.ShapeDtypeStruct((r, c_dim), F32) for _ in range(4)), grid=(r // tr,),
        in_specs=[pl.BlockSpec((N_DEV, tr, c_dim), lambda i: (0, i, 0)), row, row, row],
        out_specs=(row, row, row, row), compiler_params=_params(("parallel",)),
    )(parts, w, m, v)


SHARDED = (("w_in", 1), ("w_out", 0), ("w_mlp_in", 1), ("w_mlp_out", 0), ("w_ukv", 1), ("w_uq", 1))
REPLICATED = ("g_mix", "q_norm", "kv_norm", "rpb", "out_norm_a", "out_norm_b", "out_norm_c", "g_mlp", "g_final")
WEIGHTS = ("g_mix", "w_in", "q_norm", "w_uq", "kv_norm", "w_ukv", "rpb", "out_norm_a", "out_norm_b", "out_norm_c",
           "w_out", "g_mlp", "w_mlp_in", "w_mlp_out", "g_final")


def _pack_rows(arrs, dtype):
    parts = [a.reshape(-1, PACK_COLS).astype(dtype) for a in arrs]
    rows = sum(p.shape[0] for p in parts)
    pad = -rows % PACK_ROW_ALIGN
    if pad:
        parts.append(jnp.zeros((pad, PACK_COLS), dtype))
    return jnp.concatenate(parts, axis=0)


def _unpack_rows(packed, shapes):
    out, off = [], 0
    for shp in shapes:
        rows = int(np.prod(shp)) // PACK_COLS
        out.append(packed[..., off:off + rows, :].reshape(packed.shape[:-2] + tuple(shp)))
        off += rows
    return out


def _pack_flat(arrs):
    flat = jnp.concatenate([a.reshape(-1) for a in arrs])
    pad = -flat.shape[0] % (8 * PACK_COLS)
    return jnp.pad(flat, (0, pad)).reshape(-1, PACK_COLS)


def _unpack_flat(packed, shapes):
    flat = packed.reshape(-1)
    out, off = [], 0
    for shp in shapes:
        size = int(np.prod(shp))
        out.append(flat[off:off + size].reshape(shp))
        off += size
    return out


def _gathered_to_full(g, axis):
    n, l, a, b = g.shape
    if axis == 0:
        return jnp.transpose(g, (1, 0, 2, 3)).reshape(l, n * a, b)
    return jnp.transpose(g, (1, 2, 0, 3)).reshape(l, a, n * b)


def _full_to_chunks(w, axis):
    l, a, b = w.shape
    if axis == 0:
        return jnp.transpose(w.reshape(l, N_DEV, a // N_DEV, b), (1, 0, 2, 3))
    return jnp.transpose(w.reshape(l, a, N_DEV, b // N_DEV), (2, 0, 1, 3))


def kernel(x, g_mix, w_in, q_norm, w_uq, kv_norm, w_ukv, rpb, out_norm_a, out_norm_b, out_norm_c, w_out, g_mlp, w_mlp_in, w_mlp_out, g_final, loss_target, m_g_mix, m_w_in, m_q_norm, m_w_uq, m_kv_norm, m_w_ukv, m_rpb, m_out_norm_a, m_out_norm_b, m_out_norm_c, m_w_out, m_g_mlp, m_w_mlp_in, m_w_mlp_out, m_g_final, v_g_mix, v_w_in, v_q_norm, v_w_uq, v_kv_norm, v_w_ukv, v_rpb, v_out_norm_a, v_out_norm_b, v_out_norm_c, v_w_out, v_g_mlp, v_w_mlp_in, v_w_mlp_out, v_g_final):
    given = dict(locals())
    w = {n: given[n] for n in WEIGHTS}
    m = {n: given["m_" + n] for n in WEIGHTS}
    v = {n: given["v_" + n] for n in WEIGHTS}
    s = x.shape[1]
    depth = g_mix.shape[0]
    shard_shapes = [w[n].shape for n, _ in SHARDED]

    gathered = _all_gather(_pack_rows([w[n] for n, _ in SHARDED], BF16), name="gather_weights")
    full = {}
    for (n, axis), g in zip(SHARDED, _unpack_rows(gathered, shard_shapes)):
        full[n] = _gathered_to_full(g, axis)
    stand_in = {n: jnp.broadcast_to(jnp.zeros((), F32), full[n].shape[1:]) for n, _ in SHARDED}
    w_in = full["w_in"]
    full["w_in"] = jnp.concatenate(
        [w_in[..., COLS_A:COLS_A + COLS_B], w_in[..., COLS_A + COLS_B:], w_in[..., :COLS_A]], axis=-1)

    layer_w = [dict(stand_in, **{n: w[n][l] for n in REPLICATED if n != "g_final"}) for l in range(depth)]
    layer_w16 = [{n: full[n][l] for n, _ in SHARDED} for l in range(depth)]
    rope32 = _rope_tables(s, QK_ROPE)
    rope64 = _rope_tables(s, HEAD_DIM)

    def trunk(x0, lws):
        h = x0
        for lw, lw16 in zip(lws, layer_w16):
            h = _layer(h, lw, lw16, rope32, rope64)
        return h

    x_out, pullback = jax.vjp(trunk, x[0], layer_w)
    loss_local, dx_out, d_g_final = _loss_head(x_out, g_final, loss_target[0], name="loss_head")
    dx0, d_layers = pullback(dx_out)
    loss = lax.psum(loss_local, AXES)

    grads_local = {n: jnp.stack([d[n] for d in d_layers]) for n in layer_w[0]}
    grads_local["g_final"] = d_g_final

    chunks = [_full_to_chunks(grads_local[n], axis).reshape(N_DEV, -1, PACK_COLS).astype(BF16)
              for n, axis in SHARDED]
    rows = sum(c.shape[1] for c in chunks)
    pad = -rows % PACK_ROW_ALIGN
    if pad:
        chunks.append(jnp.zeros((N_DEV, pad, PACK_COLS), BF16))
    parts = _all_to_all(jnp.concatenate(chunks, axis=1), name="scatter_grads")
    big = _adamw(parts, _pack_rows([w[n] for n, _ in SHARDED], F32), _pack_rows([m[n] for n, _ in SHARDED], F32),
                 _pack_rows([v[n] for n, _ in SHARDED], F32), name="adamw_sharded")
    big = [_unpack_rows(o, shard_shapes) for o in big]

    rep_shapes = [w[n].shape for n in REPLICATED]
    rep_parts = _all_gather(_pack_flat([grads_local[n] for n in REPLICATED]), name="gather_small_grads")
    small = _adamw(rep_parts, _pack_flat([w[n] for n in REPLICATED]), _pack_flat([m[n] for n in REPLICATED]),
                   _pack_flat([v[n] for n in REPLICATED]), name="adamw_replicated")
    small = [_unpack_flat(o, rep_shapes) for o in small]

    result = {}
    for kind, idx in (("grad", 0), ("delta", 1), ("new_m", 2), ("new_v", 3)):
        for j, (n, _) in enumerate(SHARDED):
            result[kind + "_" + n] = big[idx][j]
        for j, n in enumerate(REPLICATED):
            result[kind + "_" + n] = small[idx][j]
    outs = [loss, dx0[None]]
    for kind in ("grad", "delta", "new_m", "new_v"):
        outs += [result[kind + "_" + n] for n in WEIGHTS]
    return tuple(outs)
```

```python
import functools

import numpy as np
import jax
import jax.numpy as jnp
from jax import lax
from jax.experimental import pallas as pl
from jax.experimental.pallas import tpu as pltpu

F32 = jnp.float32
BF16 = jnp.bfloat16

D_MODEL = 1024
DEPTH = 4
HEAD_DIM = 64
HEADS_A = 6
HEADS_B = 6
HEADS_C = 4
Q_LORA = 256
KV_LORA = 128
QK_NOPE = 64
QK_ROPE = 32
V_DIM_A = 64
DILATED_PAIRS = ((128, 1), (512, 4), (2048, 16))
BAND_HALF = 64
GRID_W = 64
NA_ROWS = 8
NA_COLS = 16
D_FF = 4096
ROPE_THETA = 10000.0
NORM_EPS = 1e-6
NEG_INF = -1e30
LOG2_E = 1.4426950408889634

COLS_A = Q_LORA + KV_LORA + QK_ROPE
COLS_B = 3 * HEADS_B * HEAD_DIM
COLS_C = 3 * HEADS_C * HEAD_DIM
WIDTH_A = HEADS_A * V_DIM_A
WIDTH_B = HEADS_B * HEAD_DIM
WIDTH_C = HEADS_C * HEAD_DIM

ADAM_LR = 0.001
ADAM_B1 = 0.9
ADAM_B2 = 0.999
ADAM_EPS = 1e-08
ADAM_WD = 0.01
ADAM_STEP = 10

N_DEV = 8
AXES = ("x", "y", "c")
MESH = pl.DeviceIdType.MESH

V7X_VMEM_LIMIT = 48 * 1024 * 1024
MM_VMEM_BUDGET = 32 * 1024 * 1024
LANES = 128
PACK_COLS = 1024
PACK_ROW_ALIGN = 16
ADAM_MAX_ROWS = 160
DENSE_TILE_FWD = 512
DENSE_TQ = 256
DENSE_TK_BWD = 1024
WIN_TILES_PER_STEP = 4
BAND_TILE = 256
NA_TILE_ROWS = 4


def _params(sem=None):
    return pltpu.CompilerParams(dimension_semantics=sem, vmem_limit_bytes=V7X_VMEM_LIMIT)


def _tile(dim, pref):
    if dim <= pref:
        return dim
    for t in range(pref - pref % LANES, LANES - 1, -LANES):
        if dim % t == 0:
            return t
    return dim


def _mm(a, b, *, ta=False, tb=False, out_dtype=F32, epilogue=None, extra=None, splits=None, name):
    if ta:
        k_dim, m_dim = a.shape
    else:
        m_dim, k_dim = a.shape
    n_dim = b.shape[0] if tb else b.shape[1]
    tn = _tile(n_dim, 1024) if n_dim % LANES == 0 else n_dim
    tk = _tile(k_dim, 1024 if not ta else 512) if k_dim % LANES == 0 else k_dim
    if epilogue == "split":
        tn = n_dim
        assert sum(splits) == n_dim and all(off % LANES == 0 for off in np.cumsum(splits)[:-1])
    n_out = 2 if epilogue == "relu2" else 1
    for tm in (_tile(m_dim, 1024), _tile(m_dim, 512), _tile(m_dim, 256)):
        blocks = (tm * tk * a.dtype.itemsize + tk * tn * b.dtype.itemsize
                  + (tm * tn * 4 if extra is not None else 0) + n_out * tm * tn * 4)
        if 2 * blocks + tm * tn * 4 <= MM_VMEM_BUDGET:
            break
    nk = k_dim // tk
    a_spec = (pl.BlockSpec((tk, tm), lambda i, j, k: (k, i)) if ta
              else pl.BlockSpec((tm, tk), lambda i, j, k: (i, k)))
    b_spec = (pl.BlockSpec((tn, tk), lambda i, j, k: (j, k)) if tb
              else pl.BlockSpec((tk, tn), lambda i, j, k: (k, j)))
    o_spec = pl.BlockSpec((tm, tn), lambda i, j, k: (i, j))
    dims = (((0 if ta else 1,), (1 if tb else 0,)), ((), ()))
    in_specs = [a_spec, b_spec]
    operands = [a, b]
    if epilogue in ("drelu2", "add"):
        in_specs.append(o_spec)
        operands.append(extra)
    if epilogue == "relu2":
        out_shape = (jax.ShapeDtypeStruct((m_dim, n_dim), F32), jax.ShapeDtypeStruct((m_dim, n_dim), BF16))
        out_specs = (o_spec, o_spec)
    elif epilogue == "split":
        out_shape = tuple(jax.ShapeDtypeStruct((m_dim, w), out_dtype) for w in splits)
        out_specs = tuple(pl.BlockSpec((tm, w), lambda i, j, k: (i, 0)) for w in splits)
    else:
        out_shape = jax.ShapeDtypeStruct((m_dim, n_dim), out_dtype)
        out_specs = o_spec

    def kern(*refs):
        acc_ref = refs[-1]
        a_ref, b_ref = refs[0], refs[1]
        k = pl.program_id(2)

        @pl.when(k == 0)
        def _():
            acc_ref[...] = jnp.zeros_like(acc_ref)

        acc_ref[...] += lax.dot_general(a_ref[...].astype(BF16), b_ref[...].astype(BF16), dims,
                                        preferred_element_type=F32)

        @pl.when(k == nk - 1)
        def _():
            acc = acc_ref[...]
            if epilogue == "relu2":
                refs[2][...] = acc
                r = jnp.maximum(acc, 0.0)
                refs[3][...] = (r * r).astype(BF16)
            elif epilogue == "drelu2":
                refs[3][...] = (acc * (2.0 * jnp.maximum(refs[2][...], 0.0))).astype(out_dtype)
            elif epilogue == "add":
                refs[3][...] = (acc + refs[2][...]).astype(out_dtype)
            elif epilogue == "split":
                off = 0
                for part, w in enumerate(splits):
                    refs[2 + part][...] = acc[:, off:off + w].astype(out_dtype)
                    off += w
            else:
                refs[2][...] = acc.astype(out_dtype)

    return pl.pallas_call(
        kern, name=name, out_shape=out_shape, grid=(m_dim // tm, n_dim // tn, nk),
        in_specs=in_specs, out_specs=out_specs, scratch_shapes=[pltpu.VMEM((tm, tn), F32)],
        compiler_params=_params(("parallel", "parallel", "arbitrary")),
    )(*operands)


def _norm_fwd(x, g, *, out_dtype, name):
    s, w = x.shape
    ts = _tile(s, 512)

    def kern(x_ref, g_ref, y_ref):
        xv = x_ref[...]
        r = lax.rsqrt(jnp.mean(xv * xv, axis=-1, keepdims=True) + NORM_EPS)
        y_ref[...] = (xv * r * g_ref[...]).astype(out_dtype)

    return pl.pallas_call(
        kern, name=name, out_shape=jax.ShapeDtypeStruct((s, w), out_dtype), grid=(s // ts,),
        in_specs=[pl.BlockSpec((ts, w), lambda i: (i, 0)), pl.BlockSpec((1, w), lambda i: (0, 0))],
        out_specs=pl.BlockSpec((ts, w), lambda i: (i, 0)),
        compiler_params=_params(("parallel",)),
    )(x, g.reshape(1, w))


def _norm_bwd(x, g, dy, *, name):
    s, w = x.shape
    ts = _tile(s, 512)

    def kern(x_ref, g_ref, dy_ref, dx_ref, dg_ref):
        @pl.when(pl.program_id(0) == 0)
        def _():
            dg_ref[...] = jnp.zeros_like(dg_ref)

        xv = x_ref[...]
        dyv = dy_ref[...]
        r = lax.rsqrt(jnp.mean(xv * xv, axis=-1, keepdims=True) + NORM_EPS)
        xhat = xv * r
        dg_ref[...] += jnp.sum(dyv * xhat, axis=0, keepdims=True)
        dxhat = dyv * g_ref[...]
        dx_ref[...] = r * (dxhat - xhat * jnp.mean(dxhat * xhat, axis=-1, keepdims=True))

    dx, dg = pl.pallas_call(
        kern, name=name,
        out_shape=(jax.ShapeDtypeStruct((s, w), F32), jax.ShapeDtypeStruct((1, w), F32)), grid=(s // ts,),
        in_specs=[pl.BlockSpec((ts, w), lambda i: (i, 0)), pl.BlockSpec((1, w), lambda i: (0, 0)),
                  pl.BlockSpec((ts, w), lambda i: (i, 0))],
        out_specs=(pl.BlockSpec((ts, w), lambda i: (i, 0)), pl.BlockSpec((1, w), lambda i: (0, 0))),
        compiler_params=_params(("arbitrary",)),
    )(x, g.reshape(1, w), dy)
    return dx, dg.reshape(w)


def _loss_head(x, g, target, *, name):
    s, w = x.shape
    ts = _tile(s, 512)

    def kern(x_ref, g_ref, t_ref, loss_ref, dx_ref, dg_ref):
        @pl.when(pl.program_id(0) == 0)
        def _():
            dg_ref[...] = jnp.zeros_like(dg_ref)
            loss_ref[...] = jnp.zeros_like(loss_ref)

        xv = x_ref[...]
        gv = g_ref[...]
        r = lax.rsqrt(jnp.mean(xv * xv, axis=-1, keepdims=True) + NORM_EPS)
        xhat = xv * r
        err = xhat * gv - t_ref[...]
        loss_ref[...] += 0.5 * jnp.sum(jnp.mean(err * err, axis=-1, keepdims=True))
        dyv = err * (1.0 / w)
        dg_ref[...] += jnp.sum(dyv * xhat, axis=0, keepdims=True)
        dxhat = dyv * gv
        dx_ref[...] = r * (dxhat - xhat * jnp.mean(dxhat * xhat, axis=-1, keepdims=True))

    loss, dx, dg = pl.pallas_call(
        kern, name=name,
        out_shape=(jax.ShapeDtypeStruct((1, LANES), F32), jax.ShapeDtypeStruct((s, w), F32),
                   jax.ShapeDtypeStruct((1, w), F32)),
        grid=(s // ts,),
        in_specs=[pl.BlockSpec((ts, w), lambda i: (i, 0)), pl.BlockSpec((1, w), lambda i: (0, 0)),
                  pl.BlockSpec((ts, w), lambda i: (i, 0))],
        out_specs=(pl.BlockSpec((1, LANES), lambda i: (0, 0)), pl.BlockSpec((ts, w), lambda i: (i, 0)),
                   pl.BlockSpec((1, w), lambda i: (0, 0))),
        compiler_params=_params(("arbitrary",)),
    )(x, g.reshape(1, w), target)
    return loss[0, 0], dx, dg.reshape(w)


def _dense_fwd(q, k, v_ones, *, dv, scale, name):
    g_dim, n, dq = q.shape
    dve = v_ones.shape[-1]
    tq = tk = _tile(n, DENSE_TILE_FWD)
    nk = n // tk
    to_log2 = scale * LOG2_E

    def kern(q_ref, k_ref, v_ref, o_ref, lse_ref):
        qb = q_ref[0]

        def body(j, carry):
            m, acc = carry
            start = pl.multiple_of(j * tk, tk)
            kb = k_ref[0, pl.ds(start, tk), :]
            s = lax.dot_general(qb, kb, (((1,), (1,)), ((), ())), preferred_element_type=F32)
            m_new = jnp.maximum(m, jnp.max(s, axis=-1, keepdims=True))
            p = jnp.exp2(((s - m_new) * to_log2).astype(BF16))
            acc = jnp.exp2((m - m_new) * to_log2) * acc + jnp.dot(
                p, v_ref[0, pl.ds(start, tk), :], preferred_element_type=F32)
            return m_new, acc

        m, acc = lax.fori_loop(0, nk, body, (jnp.full((tq, 1), NEG_INF, F32), jnp.zeros((tq, dve), F32)))
        l = acc[:, dv:dv + 1]
        o_ref[0] = acc[:, :dv] / l
        lse_ref[0] = m * scale + jnp.log(l)

    return pl.pallas_call(
        kern, name=name,
        out_shape=(jax.ShapeDtypeStruct((g_dim, n, dv), F32), jax.ShapeDtypeStruct((g_dim, n, 1), F32)),
        grid=(g_dim, n // tq),
        in_specs=[pl.BlockSpec((1, tq, dq), lambda g, i: (g, i, 0)), pl.BlockSpec((1, n, dq), lambda g, i: (g, 0, 0)),
                  pl.BlockSpec((1, n, dve), lambda g, i: (g, 0, 0))],
        out_specs=(pl.BlockSpec((1, tq, dv), lambda g, i: (g, i, 0)), pl.BlockSpec((1, tq, 1), lambda g, i: (g, i, 0))),
        compiler_params=_params(("parallel", "arbitrary")),
    )(q, k, v_ones)


def _dense_bwd(q_t, k, k_t, v, o_t, lse, do_t, *, scale, name):
    g_dim, dq, n = q_t.shape
    dv = v.shape[-1]
    tq = _tile(n, DENSE_TQ)
    tk = _tile(n, DENSE_TK_BWD)
    nk = n // tk
    nt = (((1,), (1,)), ((), ()))

    def kern(q_ref, k_ref, kt_ref, v_ref, o_ref, lse_ref, do_ref, dq_ref, dk_ref, dv_ref):
        @pl.when(pl.program_id(1) == 0)
        def _():
            dk_ref[...] = jnp.zeros_like(dk_ref)
            dv_ref[...] = jnp.zeros_like(dv_ref)

        qb = q_ref[0]
        dob = do_ref[0]
        delta = jnp.sum(dob * o_ref[0], axis=0, keepdims=True)
        dob16 = dob.astype(BF16)
        lse_v = lse_ref[0]

        def body(j, dq_acc):
            start = pl.multiple_of(j * tk, tk)
            s = jnp.dot(k_ref[0, pl.ds(start, tk), :], qb, preferred_element_type=F32) * scale
            p = jnp.exp(s - lse_v)
            dp = jnp.dot(v_ref[0, pl.ds(start, tk), :], dob16, preferred_element_type=F32)
            ds16 = (p * (dp - delta) * scale).astype(BF16)
            dv_ref[0, pl.ds(start, tk), :] += lax.dot_general(p.astype(BF16), dob16, nt, preferred_element_type=F32)
            dk_ref[0, pl.ds(start, tk), :] += lax.dot_general(ds16, qb, nt, preferred_element_type=F32)
            return dq_acc + jnp.dot(kt_ref[0, :, pl.ds(start, tk)], ds16, preferred_element_type=F32)

        dq_ref[0] = lax.fori_loop(0, nk, body, jnp.zeros((dq, tq), F32))

    qt_spec = pl.BlockSpec((1, dq, tq), lambda g, i: (g, 0, i))
    ot_spec = pl.BlockSpec((1, dv, tq), lambda g, i: (g, 0, i))
    l_spec = pl.BlockSpec((1, 1, tq), lambda g, i: (g, 0, i))
    k_spec = pl.BlockSpec((1, n, dq), lambda g, i: (g, 0, 0))
    kt_spec = pl.BlockSpec((1, dq, n), lambda g, i: (g, 0, 0))
    v_spec = pl.BlockSpec((1, n, dv), lambda g, i: (g, 0, 0))
    return pl.pallas_call(
        kern, name=name,
        out_shape=(jax.ShapeDtypeStruct((g_dim, dq, n), F32), jax.ShapeDtypeStruct((g_dim, n, dq), F32),
                   jax.ShapeDtypeStruct((g_dim, n, dv), F32)),
        grid=(g_dim, n // tq), in_specs=[qt_spec, k_spec, kt_spec, v_spec, ot_spec, l_spec, ot_spec],
        out_specs=(qt_spec, k_spec, v_spec), compiler_params=_params(("parallel", "arbitrary")),
    )(q_t, k, k_t, v, o_t, lse, do_t)


def _make_dense_attention(scale, name):
    @jax.custom_vjp
    def att(q, k, v):
        return att_fwd(q, k, v)[0]

    def att_fwd(q, k, v):
        q16 = jnp.transpose(q.astype(BF16), (1, 0, 2))
        k16 = jnp.transpose(k.astype(BF16), (1, 0, 2))
        v16 = jnp.transpose(v.astype(BF16), (1, 0, 2))
        h, n, dv = v16.shape
        v_ones = jnp.concatenate([v16, jnp.ones((h, n, 1), BF16), jnp.zeros((h, n, LANES - dv - 1), BF16)], axis=-1)
        o, lse = _dense_fwd(q16, k16, v_ones, dv=dv, scale=scale, name=name + "_fwd")
        return jnp.transpose(o, (1, 0, 2)), (q16, k16, v16, o, lse)

    def att_bwd(res, do):
        q16, k16, v16, o, lse = res
        dq_t, dk, dv_ = _dense_bwd(jnp.transpose(q16, (0, 2, 1)), k16, jnp.transpose(k16, (0, 2, 1)), v16,
                                   jnp.transpose(o, (0, 2, 1)), jnp.transpose(lse, (0, 2, 1)),
                                   jnp.transpose(do, (1, 2, 0)), scale=scale, name=name + "_bwd")
        return jnp.transpose(dq_t, (2, 0, 1)), jnp.transpose(dk, (1, 0, 2)), jnp.transpose(dv_, (1, 0, 2))

    att.defvjp(att_fwd, att_bwd)
    return att


def _window(mode, n):
    if mode == "band":
        tq = min(BAND_TILE, n)
        return tq, BAND_HALF, tq + 2 * BAND_HALF
    tq = NA_TILE_ROWS * GRID_W
    front = (NA_ROWS // 2) * GRID_W
    return tq, front, tq + NA_ROWS * GRID_W


def _window_mask(mode, i0, tq, front, span, n):
    q_pos = i0 + lax.broadcasted_iota(jnp.int32, (tq, 1), 0)
    k_pos = i0 - front + lax.broadcasted_iota(jnp.int32, (1, span), 1)
    if mode == "band":
        diff = k_pos - q_pos
        return (diff <= BAND_HALF) & (diff >= -BAND_HALF) & (k_pos >= 0) & (k_pos < n)
    rows = n // GRID_W
    shift = GRID_W.bit_length() - 1
    r_start = jnp.clip((q_pos >> shift) - NA_ROWS // 2, 0, rows - NA_ROWS)
    c_start = jnp.clip((q_pos & (GRID_W - 1)) - NA_COLS // 2, 0, GRID_W - NA_COLS)
    kr = k_pos >> shift
    kc = k_pos & (GRID_W - 1)
    return (kr >= r_start) & (kr < r_start + NA_ROWS) & (kc >= c_start) & (kc < c_start + NA_COLS)


def _win_fwd(q, k, v, bias, *, mode, scale, name):
    g_dim, n, d = q.shape
    tq, front, span = _window(mode, n)
    sub = min(WIN_TILES_PER_STEP, n // tq)
    n_pad = k.shape[1]

    def kern(*refs):
        if mode == "natten":
            q_ref, k_ref, v_ref, b_ref, o_ref, lse_ref = refs
        else:
            q_ref, k_ref, v_ref, o_ref, lse_ref = refs
        for t in range(sub):
            rows = pl.ds(t * tq, tq)
            i0 = pl.multiple_of(pl.program_id(1) * (sub * tq) + t * tq, tq)
            kb = k_ref[0, pl.ds(i0, span), :]
            vb = v_ref[0, pl.ds(i0, span), :]
            s = lax.dot_general(q_ref[0, rows, :], kb, (((1,), (1,)), ((), ())), preferred_element_type=F32) * scale
            if mode == "natten":
                s = s + b_ref[0]
            s = jnp.where(_window_mask(mode, i0, tq, front, span, n), s, NEG_INF)
            m = jnp.max(s, axis=-1, keepdims=True)
            p = jnp.exp(s - m)
            l = jnp.sum(p, axis=-1, keepdims=True)
            o_ref[0, rows, :] = jnp.dot(p.astype(BF16), vb, preferred_element_type=F32) / l
            lse_ref[0, rows, :] = m + jnp.log(l)

    in_specs = [pl.BlockSpec((1, sub * tq, d), lambda g, i: (g, i, 0)),
                pl.BlockSpec((1, n_pad, d), lambda g, i: (g, 0, 0)), pl.BlockSpec((1, n_pad, d), lambda g, i: (g, 0, 0))]
    operands = [q, k, v]
    if mode == "natten":
        in_specs.append(pl.BlockSpec((1, tq, span), lambda g, i: (g, 0, 0)))
        operands.append(bias)
    return pl.pallas_call(
        kern, name=name,
        out_shape=(jax.ShapeDtypeStruct((g_dim, n, d), F32), jax.ShapeDtypeStruct((g_dim, n, 1), F32)),
        grid=(g_dim, n // (sub * tq)), in_specs=in_specs,
        out_specs=(pl.BlockSpec((1, sub * tq, d), lambda g, i: (g, i, 0)),
                   pl.BlockSpec((1, sub * tq, 1), lambda g, i: (g, i, 0))),
        compiler_params=_params(("parallel", "arbitrary")),
    )(*operands)


def _win_bwd(q, k, v, bias, o, lse, do, dlse, *, mode, scale, name):
    g_dim, n, d = q.shape
    tq, front, span = _window(mode, n)
    sub = min(WIN_TILES_PER_STEP, n // tq)
    n_pad = k.shape[1]

    def kern(*refs):
        if mode == "natten":
            q_ref, k_ref, v_ref, b_ref, o_ref, lse_ref, do_ref, dlse_ref, dq_ref, dk_ref, dv_ref, db_ref = refs
        else:
            q_ref, k_ref, v_ref, o_ref, lse_ref, do_ref, dlse_ref, dq_ref, dk_ref, dv_ref = refs

        @pl.when(pl.program_id(1) == 0)
        def _():
            dk_ref[...] = jnp.zeros_like(dk_ref)
            dv_ref[...] = jnp.zeros_like(dv_ref)
            if mode == "natten":
                db_ref[...] = jnp.zeros_like(db_ref)

        for t in range(sub):
            rows = pl.ds(t * tq, tq)
            i0 = pl.multiple_of(pl.program_id(1) * (sub * tq) + t * tq, tq)
            qb = q_ref[0, rows, :]
            kb = k_ref[0, pl.ds(i0, span), :]
            vb = v_ref[0, pl.ds(i0, span), :]
            dob = do_ref[0, rows, :]
            delta = jnp.sum(dob * o_ref[0, rows, :], axis=-1, keepdims=True) - dlse_ref[0, rows, :]
            dob16 = dob.astype(BF16)
            s = lax.dot_general(qb, kb, (((1,), (1,)), ((), ())), preferred_element_type=F32) * scale
            if mode == "natten":
                s = s + b_ref[0]
            p = jnp.where(_window_mask(mode, i0, tq, front, span, n), jnp.exp(s - lse_ref[0, rows, :]), 0.0)
            dp = lax.dot_general(dob16, vb, (((1,), (1,)), ((), ())), preferred_element_type=F32)
            ds = p * (dp - delta)
            if mode == "natten":
                db_ref[0] += ds
            ds16 = (ds * scale).astype(BF16)
            dv_ref[0, pl.ds(i0, span), :] += lax.dot_general(
                p.astype(BF16), dob16, (((0,), (0,)), ((), ())), preferred_element_type=F32)
            dk_ref[0, pl.ds(i0, span), :] += lax.dot_general(
                ds16, qb, (((0,), (0,)), ((), ())), preferred_element_type=F32)
            dq_ref[0, rows, :] = jnp.dot(ds16, kb, preferred_element_type=F32)

    q_spec = pl.BlockSpec((1, sub * tq, d), lambda g, i: (g, i, 0))
    k_spec = pl.BlockSpec((1, n_pad, d), lambda g, i: (g, 0, 0))
    l_spec = pl.BlockSpec((1, sub * tq, 1), lambda g, i: (g, i, 0))
    in_specs = [q_spec, k_spec, k_spec]
    operands = [q, k, v]
    out_shape = [jax.ShapeDtypeStruct((g_dim, n, d), F32), jax.ShapeDtypeStruct((g_dim, n_pad, d), F32),
                 jax.ShapeDtypeStruct((g_dim, n_pad, d), F32)]
    out_specs = [q_spec, k_spec, k_spec]
    if mode == "natten":
        b_spec = pl.BlockSpec((1, tq, span), lambda g, i: (g, 0, 0))
        in_specs.append(b_spec)
        operands.append(bias)
        out_shape.append(jax.ShapeDtypeStruct(bias.shape, F32))
        out_specs.append(b_spec)
    in_specs += [q_spec, l_spec, q_spec, l_spec]
    operands += [o, lse, do, dlse]
    return pl.pallas_call(
        kern, name=name, out_shape=tuple(out_shape), grid=(g_dim, n // (sub * tq)), in_specs=in_specs,
        out_specs=tuple(out_specs), compiler_params=_params(("parallel", "arbitrary")),
    )(*operands)


def _dilate(t, dil):
    s, h, d = t.shape
    return jnp.transpose(t.reshape(s // dil, dil, h, d), (1, 2, 0, 3)).reshape(dil * h, s // dil, d)


def _undilate(t, dil):
    gh, n, d = t.shape
    return jnp.transpose(t.reshape(dil, gh // dil, n, d), (2, 0, 1, 3)).reshape(n * dil, gh // dil, d)


def _make_window_attention(mode, scale, dil, name):
    @jax.custom_vjp
    def att(q, k, v, bias):
        return att_fwd(q, k, v, bias)[0]

    def att_fwd(q, k, v, bias):
        tq, front, span = _window(mode, q.shape[0] // dil)
        pad = ((0, 0), (front, span - tq - front), (0, 0))
        q16 = _dilate(q.astype(BF16), dil)
        k16 = jnp.pad(_dilate(k.astype(BF16), dil), pad)
        v16 = jnp.pad(_dilate(v.astype(BF16), dil), pad)
        o, lse = _win_fwd(q16, k16, v16, bias, mode=mode, scale=scale, name=name + "_fwd")
        return (_undilate(o, dil), _undilate(lse, dil)[..., 0]), (q16, k16, v16, bias, o, lse)

    def att_bwd(res, cts):
        q16, k16, v16, bias, o, lse = res
        do, dlse = cts
        n = q16.shape[1]
        front = _window(mode, n)[1]
        outs = _win_bwd(q16, k16, v16, bias, o, lse, _dilate(do, dil), _dilate(dlse[..., None], dil),
                        mode=mode, scale=scale, name=name + "_bwd")
        dbias = outs[3] if mode == "natten" else None
        return (_undilate(outs[0], dil), _undilate(outs[1][:, front:front + n], dil),
                _undilate(outs[2][:, front:front + n], dil), dbias)

    att.defvjp(att_fwd, att_bwd)
    return att


def _na_onehots():
    tq, front, span = _window("natten", 0)
    q_rows, k_rows, front_rows = tq // GRID_W, span // GRID_W, front // GRID_W
    e_r = np.zeros((q_rows, k_rows, 2 * NA_ROWS - 1), np.float32)
    for qr in range(q_rows):
        for kr in range(k_rows):
            a = kr - front_rows - qr + NA_ROWS - 1
            if 0 <= a < 2 * NA_ROWS - 1:
                e_r[qr, kr, a] = 1.0
    e_c = np.zeros((GRID_W, GRID_W, 2 * NA_COLS - 1), np.float32)
    for qc in range(GRID_W):
        for kc in range(GRID_W):
            b = kc - qc + NA_COLS - 1
            if 0 <= b < 2 * NA_COLS - 1:
                e_c[qc, kc, b] = 1.0
    return e_r, e_c


def _na_bias_tiles(rpb):
    tq, _, span = _window("natten", 0)
    e_r, e_c = _na_onehots()
    t = jnp.einsum("ikA,xyB,hAB->hixky", e_r, e_c, rpb, precision=lax.Precision.HIGHEST)
    return t.reshape(rpb.shape[0], tq, span)


def _make_norm_linear(name):
    @jax.custom_vjp
    def f(x, g, w_grad, w):
        return f_fwd(x, g, w_grad, w)[0]

    def f_fwd(x, g, w_grad, w):
        h = _norm_fwd(x, g, out_dtype=BF16, name=name + "_norm")
        return _mm(h, w, name=name + "_mm"), (x, g, w, h)

    def f_bwd(res, dy):
        x, g, w, h = res
        dh = _mm(dy, w, tb=True, name=name + "_dh")
        dw = _mm(h, dy, ta=True, name=name + "_dw")
        dx, dg = _norm_bwd(x, g, dh, name=name + "_dnorm")
        return dx, dg, dw, jnp.zeros_like(w)

    f.defvjp(f_fwd, f_bwd)
    return f


def _make_in_proj(name):
    splits = (COLS_B, COLS_C, COLS_A)
    offs = (0, COLS_B, COLS_B + COLS_C)

    @jax.custom_vjp
    def f(x, g, w_grad, w):
        return f_fwd(x, g, w_grad, w)[0]

    def f_fwd(x, g, w_grad, w):
        h = _norm_fwd(x, g, out_dtype=BF16, name=name + "_norm")
        return _mm(h, w, epilogue="split", splits=splits, name=name + "_mm"), (x, g, w, h)

    def f_bwd(res, dys):
        x, g, w, h = res
        dh = None
        dws = []
        for dy, off, width, tag in zip(dys, offs, splits, "bca"):
            w_part = w[:, off:off + width]
            dh = _mm(dy, w_part, tb=True, epilogue=None if dh is None else "add", extra=dh, name=name + "_dh_" + tag)
            dws.append(_mm(h, dy, ta=True, name=name + "_dw_" + tag))
        dx, dg = _norm_bwd(x, g, dh, name=name + "_dnorm")
        return dx, dg, jnp.concatenate([dws[2], dws[0], dws[1]], axis=1), jnp.zeros_like(w)

    f.defvjp(f_fwd, f_bwd)
    return f


def _make_mix_out(name):
    @jax.custom_vjp
    def f(x, oa, ob, oc, ga, gb, gc, w_grad, w):
        return f_fwd(x, oa, ob, oc, ga, gb, gc, w_grad, w)[0]

    def f_fwd(x, oa, ob, oc, ga, gb, gc, w_grad, w):
        mixed = jnp.concatenate([
            _norm_fwd(oa, ga, out_dtype=BF16, name=name + "_norm_a"),
            _norm_fwd(ob, gb, out_dtype=BF16, name=name + "_norm_b"),
            _norm_fwd(oc, gc, out_dtype=BF16, name=name + "_norm_c")], axis=-1)
        y = _mm(mixed, w, epilogue="add", extra=x, name=name + "_mm")
        return y, (oa, ob, oc, ga, gb, gc, w, mixed)

    def f_bwd(res, dy):
        oa, ob, oc, ga, gb, gc, w, mixed = res
        dmixed = _mm(dy, w, tb=True, name=name + "_dmixed")
        dw = _mm(mixed, dy, ta=True, name=name + "_dw")
        doa, dga = _norm_bwd(oa, ga, dmixed[:, :WIDTH_A], name=name + "_dnorm_a")
        dob, dgb = _norm_bwd(ob, gb, dmixed[:, WIDTH_A:WIDTH_A + WIDTH_B], name=name + "_dnorm_b")
        doc, dgc = _norm_bwd(oc, gc, dmixed[:, WIDTH_A + WIDTH_B:], name=name + "_dnorm_c")
        return dy, doa, dob, doc, dga, dgb, dgc, dw, jnp.zeros_like(w)

    f.defvjp(f_fwd, f_bwd)
    return f


def _make_mlp(name):
    @jax.custom_vjp
    def f(x, g, w1_grad, w2_grad, w1, w2):
        return f_fwd(x, g, w1_grad, w2_grad, w1, w2)[0]

    def f_fwd(x, g, w1_grad, w2_grad, w1, w2):
        h = _norm_fwd(x, g, out_dtype=BF16, name=name + "_norm")
        u, a = _mm(h, w1, epilogue="relu2", name=name + "_up")
        y = _mm(a, w2, epilogue="add", extra=x, name=name + "_down")
        return y, (x, g, w1, w2, h, u, a)

    def f_bwd(res, dy):
        x, g, w1, w2, h, u, a = res
        du = _mm(dy, w2, tb=True, epilogue="drelu2", extra=u, out_dtype=BF16, name=name + "_du")
        dw2 = _mm(a, dy, ta=True, name=name + "_dw2")
        dw1 = _mm(h, du, ta=True, name=name + "_dw1")
        dh = _mm(du, w1, tb=True, name=name + "_dh")
        dx, dg = _norm_bwd(x, g, dh, name=name + "_dnorm")
        return dx + dy, dg, dw1, dw2, jnp.zeros_like(w1), jnp.zeros_like(w2)

    f.defvjp(f_fwd, f_bwd)
    return f


def _rope_tables(s, dim):
    half = dim // 2
    inv_freq = ROPE_THETA ** (-jnp.arange(half, dtype=F32) / half)
    ang = jnp.arange(s, dtype=F32)[:, None] * inv_freq[None, :]
    return jnp.cos(ang)[:, None, :], jnp.sin(ang)[:, None, :]


def _rope(x, cos, sin):
    half = x.shape[-1] // 2
    x1, x2 = x[..., :half], x[..., half:]
    return jnp.concatenate([x1 * cos - x2 * sin, x1 * sin + x2 * cos], axis=-1)


def _layer(x, lw, lw16, rope32, rope64):
    s = x.shape[0]
    p_b, p_c, p_a = _make_in_proj("in")(x, lw["g_mix"], lw["w_in"], lw16["w_in"])
    c_q = p_a[:, :Q_LORA]
    c_kv = p_a[:, Q_LORA:Q_LORA + KV_LORA]
    k_pe = p_a[:, Q_LORA + KV_LORA:]
    p_b = p_b.reshape(s, 3, HEADS_B, HEAD_DIM)
    p_c = p_c.reshape(s, 3, HEADS_C, HEAD_DIM)

    qa = _make_norm_linear("uq")(c_q, lw["q_norm"], lw["w_uq"], lw16["w_uq"]).reshape(
        s, HEADS_A, QK_NOPE + QK_ROPE)
    kva = _make_norm_linear("ukv")(c_kv, lw["kv_norm"], lw["w_ukv"], lw16["w_ukv"]).reshape(
        s, HEADS_A, QK_NOPE + V_DIM_A)
    k_pe = jnp.broadcast_to(_rope(k_pe[:, None, :], *rope32), (s, HEADS_A, QK_ROPE))
    qa = jnp.concatenate([qa[..., :QK_NOPE], _rope(qa[..., QK_NOPE:], *rope32)], axis=-1)
    ka = jnp.concatenate([kva[..., :QK_NOPE], k_pe], axis=-1)
    att_a = _make_dense_attention((QK_NOPE + QK_ROPE) ** -0.5, "att_a")
    o_a = att_a(qa, ka, kva[..., QK_NOPE:]).reshape(s, WIDTH_A)

    qb = _rope(p_b[:, 0], *rope64)
    kb = _rope(p_b[:, 1], *rope64)
    vb = p_b[:, 2]
    outs, lses = [], []
    for _, dil in DILATED_PAIRS:
        o, lse = _make_window_attention("band", HEAD_DIM ** -0.5, dil, "att_b%d" % dil)(qb, kb, vb, None)
        outs.append(o)
        lses.append(lse)
    wgt = jax.nn.softmax(jnp.stack(lses, axis=-1), axis=-1)
    o_b = jnp.sum(jnp.stack(outs, axis=-1) * wgt[:, :, None, :], axis=-1).reshape(s, WIDTH_B)

    att_c = _make_window_attention("natten", HEAD_DIM ** -0.5, 1, "att_c")
    o_c, _ = att_c(p_c[:, 0], p_c[:, 1], p_c[:, 2], _na_bias_tiles(lw["rpb"]))
    o_c = o_c.reshape(s, WIDTH_C)

    x = _make_mix_out("out")(x, o_a, o_b, o_c, lw["out_norm_a"], lw["out_norm_b"], lw["out_norm_c"],
                             lw["w_out"], lw16["w_out"])
    return _make_mlp("mlp")(x, lw["g_mlp"], lw["w_mlp_in"], lw["w_mlp_out"], lw16["w_mlp_in"], lw16["w_mlp_out"])


def _place():
    return lax.axis_index("x"), lax.axis_index("y"), lax.axis_index("c")


def _all_gather(block, *, name):
    r, c_dim = block.shape

    def body(x_ref, out_ref, send_sems, recv_sems, local_sem):
        x, y, c = _place()
        me, sibling = (x, y, c), (x, y, 1 - c)
        chips = [(1 - x, y), (x, 1 - y), (1 - x, 1 - y)]

        def slot(px, py, pc):
            return out_ref.at[4 * px + 2 * py + pc]

        def copy(k, blk, to, src=None):
            return pltpu.make_async_remote_copy(
                src_ref=slot(*blk) if src is None else src, dst_ref=slot(*blk),
                send_sem=send_sems.at[k], recv_sem=recv_sems.at[k], device_id=to, device_id_type=MESH)

        mine = pltpu.make_async_copy(x_ref, slot(*me), local_sem)
        mine.start()
        first = [copy(0, me, sibling, src=x_ref)]
        first += [copy(1 + j, me, (*chip, c), src=x_ref) for j, chip in enumerate(chips)]
        for cp in first:
            cp.start()
        passed = [copy(4 + j, (*chip, c), sibling) for j, chip in enumerate(chips)]
        for j, chip in enumerate(chips):
            copy(1 + j, (*chip, c), me).wait_recv()
            passed[j].start()
        copy(0, sibling, me).wait_recv()
        for j, chip in enumerate(chips):
            copy(4 + j, (*chip, 1 - c), me).wait_recv()
        for cp in first + passed:
            cp.wait_send()
        mine.wait()

    return pl.pallas_call(
        body, name=name, out_shape=jax.ShapeDtypeStruct((N_DEV, r, c_dim), block.dtype),
        in_specs=[pl.BlockSpec(memory_space=pl.ANY)], out_specs=pl.BlockSpec(memory_space=pl.ANY),
        scratch_shapes=[pltpu.SemaphoreType.DMA((7,)), pltpu.SemaphoreType.DMA((7,)), pltpu.SemaphoreType.DMA(())],
    )(block)


def _all_to_all(chunks, *, name):
    def body(in_ref, out_ref, send_sems, recv_sems, local_sem):
        x, y, c = _place()
        me = 4 * x + 2 * y + c
        mine = pltpu.make_async_copy(in_ref.at[me], out_ref.at[me], local_sem)
        mine.start()
        peers = []
        for k in range(1, N_DEV):
            px = 1 - x if k & 4 else x
            py = 1 - y if k & 2 else y
            pc = 1 - c if k & 1 else c
            peers.append((px, py, pc))

        def copy(k, peer):
            pid = 4 * peer[0] + 2 * peer[1] + peer[2]
            return pltpu.make_async_remote_copy(
                src_ref=in_ref.at[pid], dst_ref=out_ref.at[me], send_sem=send_sems.at[k], recv_sem=recv_sems.at[k],
                device_id=peer, device_id_type=MESH)

        def landing(k, peer):
            pid = 4 * peer[0] + 2 * peer[1] + peer[2]
            return pltpu.make_async_remote_copy(
                src_ref=in_ref.at[pid], dst_ref=out_ref.at[pid], send_sem=send_sems.at[k], recv_sem=recv_sems.at[k],
                device_id=peer, device_id_type=MESH)

        sends = [copy(k, peer) for k, peer in enumerate(peers)]
        for cp in sends:
            cp.start()
        for k, peer in enumerate(peers):
            landing(k, peer).wait_recv()
        for cp in sends:
            cp.wait_send()
        mine.wait()

    return pl.pallas_call(
        body, name=name, out_shape=jax.ShapeDtypeStruct(chunks.shape, chunks.dtype),
        in_specs=[pl.BlockSpec(memory_space=pl.ANY)], out_specs=pl.BlockSpec(memory_space=pl.ANY),
        scratch_shapes=[pltpu.SemaphoreType.DMA((7,)), pltpu.SemaphoreType.DMA((7,)), pltpu.SemaphoreType.DMA(())],
    )(chunks)


def _adamw(parts, w, m, v, *, name):
    r, c_dim = w.shape
    tr = r
    for cand in range(ADAM_MAX_ROWS, PACK_ROW_ALIGN - 1, -PACK_ROW_ALIGN):
        if r > ADAM_MAX_ROWS and r % cand == 0:
            tr = cand
            break

    def kern(p_ref, w_ref, m_ref, v_ref, g_ref, d_ref, nm_ref, nv_ref):
        g = p_ref[0].astype(F32)
        for i in range(1, N_DEV):
            g = g + p_ref[i].astype(F32)
        nm = ADAM_B1 * m_ref[...] + (1.0 - ADAM_B1) * g
        nv = ADAM_B2 * v_ref[...] + (1.0 - ADAM_B2) * (g * g)
        m_hat = nm / (1.0 - ADAM_B1 ** ADAM_STEP)
        v_hat = nv / (1.0 - ADAM_B2 ** ADAM_STEP)
        g_ref[...] = g
        d_ref[...] = -ADAM_LR * (m_hat / (jnp.sqrt(v_hat) + ADAM_EPS) + ADAM_WD * w_ref[...])
        nm_ref[...] = nm
        nv_ref[...] = nv

    row = pl.BlockSpec((tr, c_dim), lambda i: (i, 0))
    return pl.pallas_call(
        kern, name=name, out_shape=tuple(jax.ShapeDtypeStruct((r, c_dim), F32) for _ in range(4)), grid=(r // tr,),
        in_specs=[pl.BlockSpec((N_DEV, tr, c_dim), lambda i: (0, i, 0)), row, row, row],
        out_specs=(row, row, row, row), compiler_params=_params(("parallel",)),
    )(parts, w, m, v)


SHARDED = (("w_in", 1), ("w_out", 0), ("w_mlp_in", 1), ("w_mlp_out", 0), ("w_ukv", 1), ("w_uq", 1))
REPLICATED = ("g_mix", "q_norm", "kv_norm", "rpb", "out_norm_a", "out_norm_b", "out_norm_c", "g_mlp", "g_final")
WEIGHTS = ("g_mix", "w_in", "q_norm", "w_uq", "kv_norm", "w_ukv", "rpb", "out_norm_a", "out_norm_b", "out_norm_c",
           "w_out", "g_mlp", "w_mlp_in", "w_mlp_out", "g_final")


def _pack_rows(arrs, dtype):
    parts = [a.reshape(-1, PACK_COLS).astype(dtype) for a in arrs]
    rows = sum(p.shape[0] for p in parts)
    pad = -rows % PACK_ROW_ALIGN
    if pad:
        parts.append(jnp.zeros((pad, PACK_COLS), dtype))
    return jnp.concatenate(parts, axis=0)


def _unpack_rows(packed, shapes):
    out, off = [], 0
    for shp in shapes:
        rows = int(np.prod(shp)) // PACK_COLS
        out.append(packed[..., off:off + rows, :].reshape(packed.shape[:-2] + tuple(shp)))
        off += rows
    return out


def _pack_flat(arrs):
    flat = jnp.concatenate([a.reshape(-1) for a in arrs])
    pad = -flat.shape[0] % (8 * PACK_COLS)
    return jnp.pad(flat, (0, pad)).reshape(-1, PACK_COLS)


def _unpack_flat(packed, shapes):
    flat = packed.reshape(-1)
    out, off = [], 0
    for shp in shapes:
        size = int(np.prod(shp))
        out.append(flat[off:off + size].reshape(shp))
        off += size
    return out


def _gathered_to_full(g, axis):
    n, l, a, b = g.shape
    if axis == 0:
        return jnp.transpose(g, (1, 0, 2, 3)).reshape(l, n * a, b)
    return jnp.transpose(g, (1, 2, 0, 3)).reshape(l, a, n * b)


def _full_to_chunks(w, axis):
    l, a, b = w.shape
    if axis == 0:
        return jnp.transpose(w.reshape(l, N_DEV, a // N_DEV, b), (1, 0, 2, 3))
    return jnp.transpose(w.reshape(l, a, N_DEV, b // N_DEV), (2, 0, 1, 3))


def kernel(x, g_mix, w_in, q_norm, w_uq, kv_norm, w_ukv, rpb, out_norm_a, out_norm_b, out_norm_c, w_out, g_mlp, w_mlp_in, w_mlp_out, g_final, loss_target, m_g_mix, m_w_in, m_q_norm, m_w_uq, m_kv_norm, m_w_ukv, m_rpb, m_out_norm_a, m_out_norm_b, m_out_norm_c, m_w_out, m_g_mlp, m_w_mlp_in, m_w_mlp_out, m_g_final, v_g_mix, v_w_in, v_q_norm, v_w_uq, v_kv_norm, v_w_ukv, v_rpb, v_out_norm_a, v_out_norm_b, v_out_norm_c, v_w_out, v_g_mlp, v_w_mlp_in, v_w_mlp_out, v_g_final):
    given = dict(locals())
    w = {n: given[n] for n in WEIGHTS}
    m = {n: given["m_" + n] for n in WEIGHTS}
    v = {n: given["v_" + n] for n in WEIGHTS}
    s = x.shape[1]
    depth = g_mix.shape[0]
    shard_shapes = [w[n].shape for n, _ in SHARDED]

    gathered = _all_gather(_pack_rows([w[n] for n, _ in SHARDED], BF16), name="gather_weights")
    full = {}
    for (n, axis), g in zip(SHARDED, _unpack_rows(gathered, shard_shapes)):
        full[n] = _gathered_to_full(g, axis)
    stand_in = {n: jnp.broadcast_to(jnp.zeros((), F32), full[n].shape[1:]) for n, _ in SHARDED}
    w_in = full["w_in"]
    full["w_in"] = jnp.concatenate(
        [w_in[..., COLS_A:COLS_A + COLS_B], w_in[..., COLS_A + COLS_B:], w_in[..., :COLS_A]], axis=-1)

    layer_w = [dict(stand_in, **{n: w[n][l] for n in REPLICATED if n != "g_final"}) for l in range(depth)]
    layer_w16 = [{n: full[n][l] for n, _ in SHARDED} for l in range(depth)]
    rope32 = _rope_tables(s, QK_ROPE)
    rope64 = _rope_tables(s, HEAD_DIM)

    def trunk(x0, lws):
        h = x0
        for lw, lw16 in zip(lws, layer_w16):
            h = _layer(h, lw, lw16, rope32, rope64)
        return h

    x_out, pullback = jax.vjp(trunk, x[0], layer_w)
    loss_local, dx_out, d_g_final = _loss_head(x_out, g_final, loss_target[0], name="loss_head")
    dx0, d_layers = pullback(dx_out)
    loss = lax.psum(loss_local, AXES)

    grads_local = {n: jnp.stack([d[n] for d in d_layers]) for n in layer_w[0]}
    grads_local["g_final"] = d_g_final

    chunks = [_full_to_chunks(grads_local[n], axis).reshape(N_DEV, -1, PACK_COLS).astype(BF16)
              for n, axis in SHARDED]
    rows = sum(c.shape[1] for c in chunks)
    pad = -rows % PACK_ROW_ALIGN
    if pad:
        chunks.append(jnp.zeros((N_DEV, pad, PACK_COLS), BF16))
    parts = _all_to_all(jnp.concatenate(chunks, axis=1), name="scatter_grads")
    big = _adamw(parts, _pack_rows([w[n] for n, _ in SHARDED], F32), _pack_rows([m[n] for n, _ in SHARDED], F32),
                 _pack_rows([v[n] for n, _ in SHARDED], F32), name="adamw_sharded")
    big = [_unpack_rows(o, shard_shapes) for o in big]

    rep_shapes = [w[n].shape for n in REPLICATED]
    rep_parts = _all_gather(_pack_flat([grads_local[n] for n in REPLICATED]), name="gather_small_grads")
    small = _adamw(rep_parts, _pack_flat([w[n] for n in REPLICATED]), _pack_flat([m[n] for n in REPLICATED]),
                   _pack_flat([v[n] for n in REPLICATED]), name="adamw_replicated")
    small = [_unpack_flat(o, rep_shapes) for o in small]

    result = {}
    for kind, idx in (("grad", 0), ("delta", 1), ("new_m", 2), ("new_v", 3)):
        for j, (n, _) in enumerate(SHARDED):
            result[kind + "_" + n] = big[idx][j]
        for j, n in enumerate(REPLICATED):
            result[kind + "_" + n] = small[idx][j]
    outs = [loss, dx0[None]]
    for kind in ("grad", "delta", "new_m", "new_v"):
        outs += [result[kind + "_" + n] for n in WEIGHTS]
    return tuple(outs)
```

```python
import functools

import numpy as np
import jax
import jax.numpy as jnp
from jax import lax
from jax.experimental import pallas as pl
from jax.experimental.pallas import tpu as pltpu

F32 = jnp.float32
BF16 = jnp.bfloat16

D_MODEL = 1024
DEPTH = 4
HEAD_DIM = 64
HEADS_A = 6
HEADS_B = 6
HEADS_C = 4
Q_LORA = 256
KV_LORA = 128
QK_NOPE = 64
QK_ROPE = 32
V_DIM_A = 64
DILATED_PAIRS = ((128, 1), (512, 4), (2048, 16))
BAND_HALF = 64
GRID_W = 64
NA_ROWS = 8
NA_COLS = 16
D_FF = 4096
ROPE_THETA = 10000.0
NORM_EPS = 1e-6
NEG_INF = -1e30
LOG2_E = 1.4426950408889634

COLS_A = Q_LORA + KV_LORA + QK_ROPE
COLS_B = 3 * HEADS_B * HEAD_DIM
COLS_C = 3 * HEADS_C * HEAD_DIM
WIDTH_A = HEADS_A * V_DIM_A
WIDTH_B = HEADS_B * HEAD_DIM
WIDTH_C = HEADS_C * HEAD_DIM

ADAM_LR = 0.001
ADAM_B1 = 0.9
ADAM_B2 = 0.999
ADAM_EPS = 1e-08
ADAM_WD = 0.01
ADAM_STEP = 10

N_DEV = 8
AXES = ("x", "y", "c")
MESH = pl.DeviceIdType.MESH

V7X_VMEM_LIMIT = 48 * 1024 * 1024
MM_VMEM_BUDGET = 32 * 1024 * 1024
LANES = 128
PACK_COLS = 1024
PACK_ROW_ALIGN = 16
ADAM_MAX_ROWS = 160
DENSE_TQ_FWD = 512
DENSE_TK_FWD = 1024
DENSE_TQ = 256
DENSE_TK_BWD = 1024
WIN_TILES_PER_STEP = 4
BAND_TILE = 256
NA_TILE_ROWS = 4


def _params(sem=None):
    return pltpu.CompilerParams(dimension_semantics=sem, vmem_limit_bytes=V7X_VMEM_LIMIT)


def _tile(dim, pref):
    if dim <= pref:
        return dim
    for t in range(pref - pref % LANES, LANES - 1, -LANES):
        if dim % t == 0:
            return t
    return dim


def _mm(a, b, *, ta=False, tb=False, out_dtype=F32, epilogue=None, extra=None, splits=None, name):
    if ta:
        k_dim, m_dim = a.shape
    else:
        m_dim, k_dim = a.shape
    n_dim = b.shape[0] if tb else b.shape[1]
    tn = _tile(n_dim, 1024) if n_dim % LANES == 0 else n_dim
    tk = _tile(k_dim, 1024 if not ta else 512) if k_dim % LANES == 0 else k_dim
    if epilogue == "split":
        tn = n_dim
        assert sum(splits) == n_dim and all(off % LANES == 0 for off in np.cumsum(splits)[:-1])
    n_out = 2 if epilogue == "relu2" else 1
    for tm in (_tile(m_dim, 1024), _tile(m_dim, 512), _tile(m_dim, 256)):
        blocks = (tm * tk * a.dtype.itemsize + tk * tn * b.dtype.itemsize
                  + (tm * tn * 4 if extra is not None else 0) + n_out * tm * tn * 4)
        if 2 * blocks + tm * tn * 4 <= MM_VMEM_BUDGET:
            break
    nk = k_dim // tk
    a_spec = (pl.BlockSpec((tk, tm), lambda i, j, k: (k, i)) if ta
              else pl.BlockSpec((tm, tk), lambda i, j, k: (i, k)))
    b_spec = (pl.BlockSpec((tn, tk), lambda i, j, k: (j, k)) if tb
              else pl.BlockSpec((tk, tn), lambda i, j, k: (k, j)))
    o_spec = pl.BlockSpec((tm, tn), lambda i, j, k: (i, j))
    dims = (((0 if ta else 1,), (1 if tb else 0,)), ((), ()))
    in_specs = [a_spec, b_spec]
    operands = [a, b]
    if epilogue in ("drelu2", "add"):
        in_specs.append(o_spec)
        operands.append(extra)
    if epilogue == "relu2":
        out_shape = (jax.ShapeDtypeStruct((m_dim, n_dim), F32), jax.ShapeDtypeStruct((m_dim, n_dim), BF16))
        out_specs = (o_spec, o_spec)
    elif epilogue == "split":
        out_shape = tuple(jax.ShapeDtypeStruct((m_dim, w), out_dtype) for w in splits)
        out_specs = tuple(pl.BlockSpec((tm, w), lambda i, j, k: (i, 0)) for w in splits)
    else:
        out_shape = jax.ShapeDtypeStruct((m_dim, n_dim), out_dtype)
        out_specs = o_spec

    def kern(*refs):
        acc_ref = refs[-1]
        a_ref, b_ref = refs[0], refs[1]
        k = pl.program_id(2)

        @pl.when(k == 0)
        def _():
            acc_ref[...] = jnp.zeros_like(acc_ref)

        acc_ref[...] += lax.dot_general(a_ref[...].astype(BF16), b_ref[...].astype(BF16), dims,
                                        preferred_element_type=F32)

        @pl.when(k == nk - 1)
        def _():
            acc = acc_ref[...]
            if epilogue == "relu2":
                refs[2][...] = acc
                r = jnp.maximum(acc, 0.0)
                refs[3][...] = (r * r).astype(BF16)
            elif epilogue == "drelu2":
                refs[3][...] = (acc * (2.0 * jnp.maximum(refs[2][...], 0.0))).astype(out_dtype)
            elif epilogue == "add":
                refs[3][...] = (acc + refs[2][...]).astype(out_dtype)
            elif epilogue == "split":
                off = 0
                for part, w in enumerate(splits):
                    refs[2 + part][...] = acc[:, off:off + w].astype(out_dtype)
                    off += w
            else:
                refs[2][...] = acc.astype(out_dtype)

    return pl.pallas_call(
        kern, name=name, out_shape=out_shape, grid=(m_dim // tm, n_dim // tn, nk),
        in_specs=in_specs, out_specs=out_specs, scratch_shapes=[pltpu.VMEM((tm, tn), F32)],
        compiler_params=_params(("parallel", "parallel", "arbitrary")),
    )(*operands)


def _norm_fwd(x, g, *, out_dtype, name):
    s, w = x.shape
    ts = _tile(s, 512)

    def kern(x_ref, g_ref, y_ref):
        xv = x_ref[...]
        r = lax.rsqrt(jnp.mean(xv * xv, axis=-1, keepdims=True) + NORM_EPS)
        y_ref[...] = (xv * r * g_ref[...]).astype(out_dtype)

    return pl.pallas_call(
        kern, name=name, out_shape=jax.ShapeDtypeStruct((s, w), out_dtype), grid=(s // ts,),
        in_specs=[pl.BlockSpec((ts, w), lambda i: (i, 0)), pl.BlockSpec((1, w), lambda i: (0, 0))],
        out_specs=pl.BlockSpec((ts, w), lambda i: (i, 0)),
        compiler_params=_params(("parallel",)),
    )(x, g.reshape(1, w))


def _norm_bwd(x, g, dy, *, name):
    s, w = x.shape
    ts = _tile(s, 512)

    def kern(x_ref, g_ref, dy_ref, dx_ref, dg_ref):
        @pl.when(pl.program_id(0) == 0)
        def _():
            dg_ref[...] = jnp.zeros_like(dg_ref)

        xv = x_ref[...]
        dyv = dy_ref[...]
        r = lax.rsqrt(jnp.mean(xv * xv, axis=-1, keepdims=True) + NORM_EPS)
        xhat = xv * r
        dg_ref[...] += jnp.sum(dyv * xhat, axis=0, keepdims=True)
        dxhat = dyv * g_ref[...]
        dx_ref[...] = r * (dxhat - xhat * jnp.mean(dxhat * xhat, axis=-1, keepdims=True))

    dx, dg = pl.pallas_call(
        kern, name=name,
        out_shape=(jax.ShapeDtypeStruct((s, w), F32), jax.ShapeDtypeStruct((1, w), F32)), grid=(s // ts,),
        in_specs=[pl.BlockSpec((ts, w), lambda i: (i, 0)), pl.BlockSpec((1, w), lambda i: (0, 0)),
                  pl.BlockSpec((ts, w), lambda i: (i, 0))],
        out_specs=(pl.BlockSpec((ts, w), lambda i: (i, 0)), pl.BlockSpec((1, w), lambda i: (0, 0))),
        compiler_params=_params(("arbitrary",)),
    )(x, g.reshape(1, w), dy)
    return dx, dg.reshape(w)


def _loss_head(x, g, target, *, name):
    s, w = x.shape
    ts = _tile(s, 512)

    def kern(x_ref, g_ref, t_ref, loss_ref, dx_ref, dg_ref):
        @pl.when(pl.program_id(0) == 0)
        def _():
            dg_ref[...] = jnp.zeros_like(dg_ref)
            loss_ref[...] = jnp.zeros_like(loss_ref)

        xv = x_ref[...]
        gv = g_ref[...]
        r = lax.rsqrt(jnp.mean(xv * xv, axis=-1, keepdims=True) + NORM_EPS)
        xhat = xv * r
        err = xhat * gv - t_ref[...]
        loss_ref[...] += 0.5 * jnp.sum(jnp.mean(err * err, axis=-1, keepdims=True))
        dyv = err * (1.0 / w)
        dg_ref[...] += jnp.sum(dyv * xhat, axis=0, keepdims=True)
        dxhat = dyv * gv
        dx_ref[...] = r * (dxhat - xhat * jnp.mean(dxhat * xhat, axis=-1, keepdims=True))

    loss, dx, dg = pl.pallas_call(
        kern, name=name,
        out_shape=(jax.ShapeDtypeStruct((1, LANES), F32), jax.ShapeDtypeStruct((s, w), F32),
                   jax.ShapeDtypeStruct((1, w), F32)),
        grid=(s // ts,),
        in_specs=[pl.BlockSpec((ts, w), lambda i: (i, 0)), pl.BlockSpec((1, w), lambda i: (0, 0)),
                  pl.BlockSpec((ts, w), lambda i: (i, 0))],
        out_specs=(pl.BlockSpec((1, LANES), lambda i: (0, 0)), pl.BlockSpec((ts, w), lambda i: (i, 0)),
                   pl.BlockSpec((1, w), lambda i: (0, 0))),
        compiler_params=_params(("arbitrary",)),
    )(x, g.reshape(1, w), target)
    return loss[0, 0], dx, dg.reshape(w)


def _dense_fwd(q, k, v_ones, *, dv, scale, name):
    g_dim, n, dq = q.shape
    dve = v_ones.shape[-1]
    tq = _tile(n, DENSE_TQ_FWD)
    tk = _tile(n, DENSE_TK_FWD)
    nk = n // tk
    to_log2 = scale * LOG2_E

    def kern(q_ref, k_ref, v_ref, o_ref, lse_ref):
        qb = q_ref[0]

        def body(j, carry):
            m, acc = carry
            start = pl.multiple_of(j * tk, tk)
            kb = k_ref[0, pl.ds(start, tk), :]
            s = lax.dot_general(qb, kb, (((1,), (1,)), ((), ())), preferred_element_type=F32)
            m_new = jnp.maximum(m, jnp.max(s, axis=-1, keepdims=True))
            p = jnp.exp2(((s - m_new) * to_log2).astype(BF16))
            acc = jnp.exp2((m - m_new) * to_log2) * acc + jnp.dot(
                p, v_ref[0, pl.ds(start, tk), :], preferred_element_type=F32)
            return m_new, acc

        m, acc = lax.fori_loop(0, nk, body, (jnp.full((tq, 1), NEG_INF, F32), jnp.zeros((tq, dve), F32)))
        l = acc[:, dv:dv + 1]
        o_ref[0] = acc[:, :dv] / l
        lse_ref[0] = m * scale + jnp.log(l)

    return pl.pallas_call(
        kern, name=name,
        out_shape=(jax.ShapeDtypeStruct((g_dim, n, dv), F32), jax.ShapeDtypeStruct((g_dim, n, 1), F32)),
        grid=(g_dim, n // tq),
        in_specs=[pl.BlockSpec((1, tq, dq), lambda g, i: (g, i, 0)), pl.BlockSpec((1, n, dq), lambda g, i: (g, 0, 0)),
                  pl.BlockSpec((1, n, dve), lambda g, i: (g, 0, 0))],
        out_specs=(pl.BlockSpec((1, tq, dv), lambda g, i: (g, i, 0)), pl.BlockSpec((1, tq, 1), lambda g, i: (g, i, 0))),
        compiler_params=_params(("parallel", "arbitrary")),
    )(q, k, v_ones)


def _dense_bwd(q_t, k, k_t, v, o_t, lse, do_t, *, scale, name):
    g_dim, dq, n = q_t.shape
    dv = v.shape[-1]
    tq = _tile(n, DENSE_TQ)
    tk = _tile(n, DENSE_TK_BWD)
    nk = n // tk
    nt = (((1,), (1,)), ((), ()))

    def kern(q_ref, k_ref, kt_ref, v_ref, o_ref, lse_ref, do_ref, dq_ref, dk_ref, dv_ref):
        @pl.when(pl.program_id(1) == 0)
        def _():
            dk_ref[...] = jnp.zeros_like(dk_ref)
            dv_ref[...] = jnp.zeros_like(dv_ref)

        qb = q_ref[0]
        dob = do_ref[0]
        delta = jnp.sum(dob * o_ref[0], axis=0, keepdims=True)
        dob16 = dob.astype(BF16)
        lse_v = lse_ref[0]

        def body(j, dq_acc):
            start = pl.multiple_of(j * tk, tk)
            s = jnp.dot(k_ref[0, pl.ds(start, tk), :], qb, preferred_element_type=F32) * scale
            p = jnp.exp(s - lse_v)
            dp = jnp.dot(v_ref[0, pl.ds(start, tk), :], dob16, preferred_element_type=F32)
            ds16 = (p * (dp - delta) * scale).astype(BF16)
            dv_ref[0, pl.ds(start, tk), :] += lax.dot_general(p.astype(BF16), dob16, nt, preferred_element_type=F32)
            dk_ref[0, pl.ds(start, tk), :] += lax.dot_general(ds16, qb, nt, preferred_element_type=F32)
            return dq_acc + jnp.dot(kt_ref[0, :, pl.ds(start, tk)], ds16, preferred_element_type=F32)

        dq_ref[0] = lax.fori_loop(0, nk, body, jnp.zeros((dq, tq), F32))

    qt_spec = pl.BlockSpec((1, dq, tq), lambda g, i: (g, 0, i))
    ot_spec = pl.BlockSpec((1, dv, tq), lambda g, i: (g, 0, i))
    l_spec = pl.BlockSpec((1, 1, tq), lambda g, i: (g, 0, i))
    k_spec = pl.BlockSpec((1, n, dq), lambda g, i: (g, 0, 0))
    kt_spec = pl.BlockSpec((1, dq, n), lambda g, i: (g, 0, 0))
    v_spec = pl.BlockSpec((1, n, dv), lambda g, i: (g, 0, 0))
    return pl.pallas_call(
        kern, name=name,
        out_shape=(jax.ShapeDtypeStruct((g_dim, dq, n), F32), jax.ShapeDtypeStruct((g_dim, n, dq), F32),
                   jax.ShapeDtypeStruct((g_dim, n, dv), F32)),
        grid=(g_dim, n // tq), in_specs=[qt_spec, k_spec, kt_spec, v_spec, ot_spec, l_spec, ot_spec],
        out_specs=(qt_spec, k_spec, v_spec), compiler_params=_params(("parallel", "arbitrary")),
    )(q_t, k, k_t, v, o_t, lse, do_t)


def _make_dense_attention(scale, name):
    @jax.custom_vjp
    def att(q, k, v):
        return att_fwd(q, k, v)[0]

    def att_fwd(q, k, v):
        q16 = jnp.transpose(q.astype(BF16), (1, 0, 2))
        k16 = jnp.transpose(k.astype(BF16), (1, 0, 2))
        v16 = jnp.transpose(v.astype(BF16), (1, 0, 2))
        h, n, dv = v16.shape
        v_ones = jnp.concatenate([v16, jnp.ones((h, n, 1), BF16), jnp.zeros((h, n, LANES - dv - 1), BF16)], axis=-1)
        o, lse = _dense_fwd(q16, k16, v_ones, dv=dv, scale=scale, name=name + "_fwd")
        return jnp.transpose(o, (1, 0, 2)), (q16, k16, v16, o, lse)

    def att_bwd(res, do):
        q16, k16, v16, o, lse = res
        dq_t, dk, dv_ = _dense_bwd(jnp.transpose(q16, (0, 2, 1)), k16, jnp.transpose(k16, (0, 2, 1)), v16,
                                   jnp.transpose(o, (0, 2, 1)), jnp.transpose(lse, (0, 2, 1)),
                                   jnp.transpose(do, (1, 2, 0)), scale=scale, name=name + "_bwd")
        return jnp.transpose(dq_t, (2, 0, 1)), jnp.transpose(dk, (1, 0, 2)), jnp.transpose(dv_, (1, 0, 2))

    att.defvjp(att_fwd, att_bwd)
    return att


def _window(mode, n):
    if mode == "band":
        tq = min(BAND_TILE, n)
        return tq, BAND_HALF, min(tq + 2 * BAND_HALF, n)
    tq = NA_TILE_ROWS * GRID_W
    front = (NA_ROWS // 2) * GRID_W
    return tq, front, tq + NA_ROWS * GRID_W


def _span_start(mode, i0, front, span, n):
    if mode == "band":
        start = pl.multiple_of(jnp.clip(i0 - front, 0, n - span), BAND_HALF)
        return start, start
    return i0, i0 - front


def _window_mask(mode, i0, k0, tq, span, n):
    q_pos = i0 + lax.broadcasted_iota(jnp.int32, (tq, 1), 0)
    k_pos = k0 + lax.broadcasted_iota(jnp.int32, (1, span), 1)
    if mode == "band":
        diff = k_pos - q_pos
        return (diff <= BAND_HALF) & (diff >= -BAND_HALF)
    rows = n // GRID_W
    shift = GRID_W.bit_length() - 1
    r_start = jnp.clip((q_pos >> shift) - NA_ROWS // 2, 0, rows - NA_ROWS)
    c_start = jnp.clip((q_pos & (GRID_W - 1)) - NA_COLS // 2, 0, GRID_W - NA_COLS)
    kr = k_pos >> shift
    kc = k_pos & (GRID_W - 1)
    return (kr >= r_start) & (kr < r_start + NA_ROWS) & (kc >= c_start) & (kc < c_start + NA_COLS)


def _win_fwd(q, k, v, bias, *, mode, scale, name):
    g_dim, n, d = q.shape
    tq, front, span = _window(mode, n)
    sub = min(WIN_TILES_PER_STEP, n // tq)
    n_pad = k.shape[1]

    def kern(*refs):
        if mode == "natten":
            q_ref, k_ref, v_ref, b_ref, o_ref, lse_ref = refs
        else:
            q_ref, k_ref, v_ref, o_ref, lse_ref = refs
        for t in range(sub):
            rows = pl.ds(t * tq, tq)
            i0 = pl.multiple_of(pl.program_id(1) * (sub * tq) + t * tq, tq)
            start, k0 = _span_start(mode, i0, front, span, n)
            kb = k_ref[0, pl.ds(start, span), :]
            vb = v_ref[0, pl.ds(start, span), :]
            s = lax.dot_general(q_ref[0, rows, :], kb, (((1,), (1,)), ((), ())), preferred_element_type=F32) * scale
            if mode == "natten":
                s = s + b_ref[0]
            s = jnp.where(_window_mask(mode, i0, k0, tq, span, n), s, NEG_INF)
            m = jnp.max(s, axis=-1, keepdims=True)
            p = jnp.exp(s - m)
            l = jnp.sum(p, axis=-1, keepdims=True)
            o_ref[0, rows, :] = jnp.dot(p.astype(BF16), vb, preferred_element_type=F32) / l
            lse_ref[0, rows, :] = m + jnp.log(l)

    in_specs = [pl.BlockSpec((1, sub * tq, d), lambda g, i: (g, i, 0)),
                pl.BlockSpec((1, n_pad, d), lambda g, i: (g, 0, 0)), pl.BlockSpec((1, n_pad, d), lambda g, i: (g, 0, 0))]
    operands = [q, k, v]
    if mode == "natten":
        in_specs.append(pl.BlockSpec((1, tq, span), lambda g, i: (g, 0, 0)))
        operands.append(bias)
    return pl.pallas_call(
        kern, name=name,
        out_shape=(jax.ShapeDtypeStruct((g_dim, n, d), F32), jax.ShapeDtypeStruct((g_dim, n, 1), F32)),
        grid=(g_dim, n // (sub * tq)), in_specs=in_specs,
        out_specs=(pl.BlockSpec((1, sub * tq, d), lambda g, i: (g, i, 0)),
                   pl.BlockSpec((1, sub * tq, 1), lambda g, i: (g, i, 0))),
        compiler_params=_params(("parallel", "arbitrary")),
    )(*operands)


def _win_bwd(q, k, v, bias, o, lse, do, dlse, *, mode, scale, name):
    g_dim, n, d = q.shape
    tq, front, span = _window(mode, n)
    sub = min(WIN_TILES_PER_STEP, n // tq)
    n_pad = k.shape[1]

    def kern(*refs):
        if mode == "natten":
            q_ref, k_ref, v_ref, b_ref, o_ref, lse_ref, do_ref, dlse_ref, dq_ref, dk_ref, dv_ref, db_ref = refs
        else:
            q_ref, k_ref, v_ref, o_ref, lse_ref, do_ref, dlse_ref, dq_ref, dk_ref, dv_ref = refs

        @pl.when(pl.program_id(1) == 0)
        def _():
            dk_ref[...] = jnp.zeros_like(dk_ref)
            dv_ref[...] = jnp.zeros_like(dv_ref)
            if mode == "natten":
                db_ref[...] = jnp.zeros_like(db_ref)

        for t in range(sub):
            rows = pl.ds(t * tq, tq)
            i0 = pl.multiple_of(pl.program_id(1) * (sub * tq) + t * tq, tq)
            qb = q_ref[0, rows, :]
            start, k0 = _span_start(mode, i0, front, span, n)
            kb = k_ref[0, pl.ds(start, span), :]
            vb = v_ref[0, pl.ds(start, span), :]
            dob = do_ref[0, rows, :]
            delta = jnp.sum(dob * o_ref[0, rows, :], axis=-1, keepdims=True) - dlse_ref[0, rows, :]
            dob16 = dob.astype(BF16)
            s = lax.dot_general(qb, kb, (((1,), (1,)), ((), ())), preferred_element_type=F32) * scale
            if mode == "natten":
                s = s + b_ref[0]
            p = jnp.where(_window_mask(mode, i0, k0, tq, span, n), jnp.exp(s - lse_ref[0, rows, :]), 0.0)
            dp = lax.dot_general(dob16, vb, (((1,), (1,)), ((), ())), preferred_element_type=F32)
            ds = p * (dp - delta)
            if mode == "natten":
                db_ref[0] += ds
            ds16 = (ds * scale).astype(BF16)
            dv_ref[0, pl.ds(start, span), :] += lax.dot_general(
                p.astype(BF16), dob16, (((0,), (0,)), ((), ())), preferred_element_type=F32)
            dk_ref[0, pl.ds(start, span), :] += lax.dot_general(
                ds16, qb, (((0,), (0,)), ((), ())), preferred_element_type=F32)
            dq_ref[0, rows, :] = jnp.dot(ds16, kb, preferred_element_type=F32)

    q_spec = pl.BlockSpec((1, sub * tq, d), lambda g, i: (g, i, 0))
    k_spec = pl.BlockSpec((1, n_pad, d), lambda g, i: (g, 0, 0))
    l_spec = pl.BlockSpec((1, sub * tq, 1), lambda g, i: (g, i, 0))
    in_specs = [q_spec, k_spec, k_spec]
    operands = [q, k, v]
    out_shape = [jax.ShapeDtypeStruct((g_dim, n, d), F32), jax.ShapeDtypeStruct((g_dim, n_pad, d), F32),
                 jax.ShapeDtypeStruct((g_dim, n_pad, d), F32)]
    out_specs = [q_spec, k_spec, k_spec]
    if mode == "natten":
        b_spec = pl.BlockSpec((1, tq, span), lambda g, i: (g, 0, 0))
        in_specs.append(b_spec)
        operands.append(bias)
        out_shape.append(jax.ShapeDtypeStruct(bias.shape, F32))
        out_specs.append(b_spec)
    in_specs += [q_spec, l_spec, q_spec, l_spec]
    operands += [o, lse, do, dlse]
    return pl.pallas_call(
        kern, name=name, out_shape=tuple(out_shape), grid=(g_dim, n // (sub * tq)), in_specs=in_specs,
        out_specs=tuple(out_specs), compiler_params=_params(("parallel", "arbitrary")),
    )(*operands)


def _dilate(t, dil):
    s, h, d = t.shape
    return jnp.transpose(t.reshape(s // dil, dil, h, d), (1, 2, 0, 3)).reshape(dil * h, s // dil, d)


def _undilate(t, dil):
    gh, n, d = t.shape
    return jnp.transpose(t.reshape(dil, gh // dil, n, d), (2, 0, 1, 3)).reshape(n * dil, gh // dil, d)


def _make_window_attention(mode, scale, dil, name):
    @jax.custom_vjp
    def att(q, k, v, bias):
        return att_fwd(q, k, v, bias)[0]

    def att_fwd(q, k, v, bias):
        tq, front, span = _window(mode, q.shape[0] // dil)
        q16, k16, v16 = (_dilate(t.astype(BF16), dil) for t in (q, k, v))
        if mode == "natten":
            pad = ((0, 0), (front, span - tq - front), (0, 0))
            k16, v16 = jnp.pad(k16, pad), jnp.pad(v16, pad)
        o, lse = _win_fwd(q16, k16, v16, bias, mode=mode, scale=scale, name=name + "_fwd")
        return (_undilate(o, dil), _undilate(lse, dil)[..., 0]), (q16, k16, v16, bias, o, lse)

    def att_bwd(res, cts):
        q16, k16, v16, bias, o, lse = res
        do, dlse = cts
        n = q16.shape[1]
        outs = _win_bwd(q16, k16, v16, bias, o, lse, _dilate(do, dil), _dilate(dlse[..., None], dil),
                        mode=mode, scale=scale, name=name + "_bwd")
        dq, dk, dv = outs[:3]
        dbias = None
        if mode == "natten":
            front = _window(mode, n)[1]
            dk, dv, dbias = dk[:, front:front + n], dv[:, front:front + n], outs[3]
        return _undilate(dq, dil), _undilate(dk, dil), _undilate(dv, dil), dbias

    att.defvjp(att_fwd, att_bwd)
    return att


def _na_onehots():
    tq, front, span = _window("natten", 0)
    q_rows, k_rows, front_rows = tq // GRID_W, span // GRID_W, front // GRID_W
    e_r = np.zeros((q_rows, k_rows, 2 * NA_ROWS - 1), np.float32)
    for qr in range(q_rows):
        for kr in range(k_rows):
            a = kr - front_rows - qr + NA_ROWS - 1
            if 0 <= a < 2 * NA_ROWS - 1:
                e_r[qr, kr, a] = 1.0
    e_c = np.zeros((GRID_W, GRID_W, 2 * NA_COLS - 1), np.float32)
    for qc in range(GRID_W):
        for kc in range(GRID_W):
            b = kc - qc + NA_COLS - 1
            if 0 <= b < 2 * NA_COLS - 1:
                e_c[qc, kc, b] = 1.0
    return e_r, e_c


def _na_bias_tiles(rpb):
    tq, _, span = _window("natten", 0)
    e_r, e_c = _na_onehots()
    t = jnp.einsum("ikA,xyB,hAB->hixky", e_r, e_c, rpb, precision=lax.Precision.HIGHEST)
    return t.reshape(rpb.shape[0], tq, span)


def _make_norm_linear(name):
    @jax.custom_vjp
    def f(x, g, w_grad, w):
        return f_fwd(x, g, w_grad, w)[0]

    def f_fwd(x, g, w_grad, w):
        h = _norm_fwd(x, g, out_dtype=BF16, name=name + "_norm")
        return _mm(h, w, name=name + "_mm"), (x, g, w, h)

    def f_bwd(res, dy):
        x, g, w, h = res
        dh = _mm(dy, w, tb=True, name=name + "_dh")
        dw = _mm(h, dy, ta=True, name=name + "_dw")
        dx, dg = _norm_bwd(x, g, dh, name=name + "_dnorm")
        return dx, dg, dw, jnp.zeros_like(w)

    f.defvjp(f_fwd, f_bwd)
    return f


def _make_in_proj(name):
    splits = (COLS_B, COLS_C, COLS_A)
    offs = (0, COLS_B, COLS_B + COLS_C)

    @jax.custom_vjp
    def f(x, g, w_grad, w):
        return f_fwd(x, g, w_grad, w)[0]

    def f_fwd(x, g, w_grad, w):
        h = _norm_fwd(x, g, out_dtype=BF16, name=name + "_norm")
        return _mm(h, w, epilogue="split", splits=splits, name=name + "_mm"), (x, g, w, h)

    def f_bwd(res, dys):
        x, g, w, h = res
        dh = None
        dws = []
        for dy, off, width, tag in zip(dys, offs, splits, "bca"):
            w_part = w[:, off:off + width]
            dh = _mm(dy, w_part, tb=True, epilogue=None if dh is None else "add", extra=dh, name=name + "_dh_" + tag)
            dws.append(_mm(h, dy, ta=True, name=name + "_dw_" + tag))
        dx, dg = _norm_bwd(x, g, dh, name=name + "_dnorm")
        return dx, dg, jnp.concatenate([dws[2], dws[0], dws[1]], axis=1), jnp.zeros_like(w)

    f.defvjp(f_fwd, f_bwd)
    return f


def _make_mix_out(name):
    @jax.custom_vjp
    def f(x, oa, ob, oc, ga, gb, gc, w_grad, w):
        return f_fwd(x, oa, ob, oc, ga, gb, gc, w_grad, w)[0]

    def f_fwd(x, oa, ob, oc, ga, gb, gc, w_grad, w):
        mixed = jnp.concatenate([
            _norm_fwd(oa, ga, out_dtype=BF16, name=name + "_norm_a"),
            _norm_fwd(ob, gb, out_dtype=BF16, name=name + "_norm_b"),
            _norm_fwd(oc, gc, out_dtype=BF16, name=name + "_norm_c")], axis=-1)
        y = _mm(mixed, w, epilogue="add", extra=x, name=name + "_mm")
        return y, (oa, ob, oc, ga, gb, gc, w, mixed)

    def f_bwd(res, dy):
        oa, ob, oc, ga, gb, gc, w, mixed = res
        dmixed = _mm(dy, w, tb=True, name=name + "_dmixed")
        dw = _mm(mixed, dy, ta=True, name=name + "_dw")
        doa, dga = _norm_bwd(oa, ga, dmixed[:, :WIDTH_A], name=name + "_dnorm_a")
        dob, dgb = _norm_bwd(ob, gb, dmixed[:, WIDTH_A:WIDTH_A + WIDTH_B], name=name + "_dnorm_b")
        doc, dgc = _norm_bwd(oc, gc, dmixed[:, WIDTH_A + WIDTH_B:], name=name + "_dnorm_c")
        return dy, doa, dob, doc, dga, dgb, dgc, dw, jnp.zeros_like(w)

    f.defvjp(f_fwd, f_bwd)
    return f


def _make_mlp(name):
    @jax.custom_vjp
    def f(x, g, w1_grad, w2_grad, w1, w2):
        return f_fwd(x, g, w1_grad, w2_grad, w1, w2)[0]

    def f_fwd(x, g, w1_grad, w2_grad, w1, w2):
        h = _norm_fwd(x, g, out_dtype=BF16, name=name + "_norm")
        u, a = _mm(h, w1, epilogue="relu2", name=name + "_up")
        y = _mm(a, w2, epilogue="add", extra=x, name=name + "_down")
        return y, (x, g, w1, w2, h, u, a)

    def f_bwd(res, dy):
        x, g, w1, w2, h, u, a = res
        du = _mm(dy, w2, tb=True, epilogue="drelu2", extra=u, out_dtype=BF16, name=name + "_du")
        dw2 = _mm(a, dy, ta=True, name=name + "_dw2")
        dw1 = _mm(h, du, ta=True, name=name + "_dw1")
        dh = _mm(du, w1, tb=True, name=name + "_dh")
        dx, dg = _norm_bwd(x, g, dh, name=name + "_dnorm")
        return dx + dy, dg, dw1, dw2, jnp.zeros_like(w1), jnp.zeros_like(w2)

    f.defvjp(f_fwd, f_bwd)
    return f


def _rope_tables(s, dim):
    half = dim // 2
    inv_freq = ROPE_THETA ** (-jnp.arange(half, dtype=F32) / half)
    ang = jnp.arange(s, dtype=F32)[:, None] * inv_freq[None, :]
    return jnp.cos(ang)[:, None, :], jnp.sin(ang)[:, None, :]


def _rope(x, cos, sin):
    half = x.shape[-1] // 2
    x1, x2 = x[..., :half], x[..., half:]
    return jnp.concatenate([x1 * cos - x2 * sin, x1 * sin + x2 * cos], axis=-1)


def _layer(x, lw, lw16, rope32, rope64):
    s = x.shape[0]
    p_b, p_c, p_a = _make_in_proj("in")(x, lw["g_mix"], lw["w_in"], lw16["w_in"])
    c_q = p_a[:, :Q_LORA]
    c_kv = p_a[:, Q_LORA:Q_LORA + KV_LORA]
    k_pe = p_a[:, Q_LORA + KV_LORA:]
    p_b = p_b.reshape(s, 3, HEADS_B, HEAD_DIM)
    p_c = p_c.reshape(s, 3, HEADS_C, HEAD_DIM)

    qa = _make_norm_linear("uq")(c_q, lw["q_norm"], lw["w_uq"], lw16["w_uq"]).reshape(
        s, HEADS_A, QK_NOPE + QK_ROPE)
    kva = _make_norm_linear("ukv")(c_kv, lw["kv_norm"], lw["w_ukv"], lw16["w_ukv"]).reshape(
        s, HEADS_A, QK_NOPE + V_DIM_A)
    k_pe = jnp.broadcast_to(_rope(k_pe[:, None, :], *rope32), (s, HEADS_A, QK_ROPE))
    qa = jnp.concatenate([qa[..., :QK_NOPE], _rope(qa[..., QK_NOPE:], *rope32)], axis=-1)
    ka = jnp.concatenate([kva[..., :QK_NOPE], k_pe], axis=-1)
    att_a = _make_dense_attention((QK_NOPE + QK_ROPE) ** -0.5, "att_a")
    o_a = att_a(qa, ka, kva[..., QK_NOPE:]).reshape(s, WIDTH_A)

    qb = _rope(p_b[:, 0], *rope64)
    kb = _rope(p_b[:, 1], *rope64)
    vb = p_b[:, 2]
    outs, lses = [], []
    for _, dil in DILATED_PAIRS:
        o, lse = _make_window_attention("band", HEAD_DIM ** -0.5, dil, "att_b%d" % dil)(qb, kb, vb, None)
        outs.append(o)
        lses.append(lse)
    top = functools.reduce(jnp.maximum, lses)
    wgts = [jnp.exp(lse - top) for lse in lses]
    den = functools.reduce(jnp.add, wgts)
    o_b = functools.reduce(jnp.add, [o * (wgt / den)[..., None] for o, wgt in zip(outs, wgts)]).reshape(s, WIDTH_B)

    att_c = _make_window_attention("natten", HEAD_DIM ** -0.5, 1, "att_c")
    o_c, _ = att_c(p_c[:, 0], p_c[:, 1], p_c[:, 2], _na_bias_tiles(lw["rpb"]))
    o_c = o_c.reshape(s, WIDTH_C)

    x = _make_mix_out("out")(x, o_a, o_b, o_c, lw["out_norm_a"], lw["out_norm_b"], lw["out_norm_c"],
                             lw["w_out"], lw16["w_out"])
    return _make_mlp("mlp")(x, lw["g_mlp"], lw["w_mlp_in"], lw["w_mlp_out"], lw16["w_mlp_in"], lw16["w_mlp_out"])


def _place():
    return lax.axis_index("x"), lax.axis_index("y"), lax.axis_index("c")


def _all_gather(block, *, name):
    r, c_dim = block.shape

    def body(x_ref, out_ref, send_sems, recv_sems, local_sem):
        x, y, c = _place()
        me, sibling = (x, y, c), (x, y, 1 - c)
        chips = [(1 - x, y), (x, 1 - y), (1 - x, 1 - y)]

        def slot(px, py, pc):
            return out_ref.at[4 * px + 2 * py + pc]

        def copy(k, blk, to, src=None):
            return pltpu.make_async_remote_copy(
                src_ref=slot(*blk) if src is None else src, dst_ref=slot(*blk),
                send_sem=send_sems.at[k], recv_sem=recv_sems.at[k], device_id=to, device_id_type=MESH)

        mine = pltpu.make_async_copy(x_ref, slot(*me), local_sem)
        mine.start()
        first = [copy(0, me, sibling, src=x_ref)]
        first += [copy(1 + j, me, (*chip, c), src=x_ref) for j, chip in enumerate(chips)]
        for cp in first:
            cp.start()
        passed = [copy(4 + j, (*chip, c), sibling) for j, chip in enumerate(chips)]
        for j, chip in enumerate(chips):
            copy(1 + j, (*chip, c), me).wait_recv()
            passed[j].start()
        copy(0, sibling, me).wait_recv()
        for j, chip in enumerate(chips):
            copy(4 + j, (*chip, 1 - c), me).wait_recv()
        for cp in first + passed:
            cp.wait_send()
        mine.wait()

    return pl.pallas_call(
        body, name=name, out_shape=jax.ShapeDtypeStruct((N_DEV, r, c_dim), block.dtype),
        in_specs=[pl.BlockSpec(memory_space=pl.ANY)], out_specs=pl.BlockSpec(memory_space=pl.ANY),
        scratch_shapes=[pltpu.SemaphoreType.DMA((7,)), pltpu.SemaphoreType.DMA((7,)), pltpu.SemaphoreType.DMA(())],
    )(block)


def _all_to_all(chunks, *, name):
    def body(in_ref, out_ref, send_sems, recv_sems, local_sem):
        x, y, c = _place()
        me = 4 * x + 2 * y + c
        mine = pltpu.make_async_copy(in_ref.at[me], out_ref.at[me], local_sem)
        mine.start()
        peers = []
        for k in range(1, N_DEV):
            px = 1 - x if k & 4 else x
            py = 1 - y if k & 2 else y
            pc = 1 - c if k & 1 else c
            peers.append((px, py, pc))

        def copy(k, peer):
            pid = 4 * peer[0] + 2 * peer[1] + peer[2]
            return pltpu.make_async_remote_copy(
                src_ref=in_ref.at[pid], dst_ref=out_ref.at[me], send_sem=send_sems.at[k], recv_sem=recv_sems.at[k],
                device_id=peer, device_id_type=MESH)

        def landing(k, peer):
            pid = 4 * peer[0] + 2 * peer[1] + peer[2]
            return pltpu.make_async_remote_copy(
                src_ref=in_ref.at[pid], dst_ref=out_ref.at[pid], send_sem=send_sems.at[k], recv_sem=recv_sems.at[k],
                device_id=peer, device_id_type=MESH)

        sends = [copy(k, peer) for k, peer in enumerate(peers)]
        for cp in sends:
            cp.start()
        for k, peer in enumerate(peers):
            landing(k, peer).wait_recv()
        for cp in sends:
            cp.wait_send()
        mine.wait()

    return pl.pallas_call(
        body, name=name, out_shape=jax.ShapeDtypeStruct(chunks.shape, chunks.dtype),
        in_specs=[pl.BlockSpec(memory_space=pl.ANY)], out_specs=pl.BlockSpec(memory_space=pl.ANY),
        scratch_shapes=[pltpu.SemaphoreType.DMA((7,)), pltpu.SemaphoreType.DMA((7,)), pltpu.SemaphoreType.DMA(())],
    )(chunks)


def _adamw(parts, w, m, v, *, name):
    r, c_dim = w.shape
    tr = r
    for cand in range(ADAM_MAX_ROWS, PACK_ROW_ALIGN - 1, -PACK_ROW_ALIGN):
        if r > ADAM_MAX_ROWS and r % cand == 0:
            tr = cand
            break

    def kern(p_ref, w_ref, m_ref, v_ref, g_ref, d_ref, nm_ref, nv_ref):
        g = p_ref[0].astype(F32)
        for i in range(1, N_DEV):
            g = g + p_ref[i].astype(F32)
        nm = ADAM_B1 * m_ref[...] + (1.0 - ADAM_B1) * g
        nv = ADAM_B2 * v_ref[...] + (1.0 - ADAM_B2) * (g * g)
        m_hat = nm / (1.0 - ADAM_B1 ** ADAM_STEP)
        v_hat = nv / (1.0 - ADAM_B2 ** ADAM_STEP)
        g_ref[...] = g
        d_ref[...] = -ADAM_LR * (m_hat / (jnp.sqrt(v_hat) + ADAM_EPS) + ADAM_WD * w_ref[...])
        nm_ref[...] = nm
        nv_ref[...] = nv

    row = pl.BlockSpec((tr, c_dim), lambda i: (i, 0))
    return pl.pallas_call(
        kern, name=name, out_shape=tuple(jax.ShapeDtypeStruct((r, c_dim), F32) for _ in range(4)), grid=(r // tr,),
        in_specs=[pl.BlockSpec((N_DEV, tr, c_dim), lambda i: (0, i, 0)), row, row, row],
        out_specs=(row, row, row, row), compiler_params=_params(("parallel",)),
    )(parts, w, m, v)


SHARDED = (("w_in", 1), ("w_out", 0), ("w_mlp_in", 1), ("w_mlp_out", 0), ("w_ukv", 1), ("w_uq", 1))
REPLICATED = ("g_mix", "q_norm", "kv_norm", "rpb", "out_norm_a", "out_norm_b", "out_norm_c", "g_mlp", "g_final")
WEIGHTS = ("g_mix", "w_in", "q_norm", "w_uq", "kv_norm", "w_ukv", "rpb", "out_norm_a", "out_norm_b", "out_norm_c",
           "w_out", "g_mlp", "w_mlp_in", "w_mlp_out", "g_final")


def _pack_rows(arrs, dtype):
    parts = [a.reshape(-1, PACK_COLS).astype(dtype) for a in arrs]
    rows = sum(p.shape[0] for p in parts)
    pad = -rows % PACK_ROW_ALIGN
    if pad:
        parts.append(jnp.zeros((pad, PACK_COLS), dtype))
    return jnp.concatenate(parts, axis=0)


def _unpack_rows(packed, shapes):
    out, off = [], 0
    for shp in shapes:
        rows = int(np.prod(shp)) // PACK_COLS
        out.append(packed[..., off:off + rows, :].reshape(packed.shape[:-2] + tuple(shp)))
        off += rows
    return out


def _pack_flat(arrs):
    flat = jnp.concatenate([a.reshape(-1) for a in arrs])
    pad = -flat.shape[0] % (8 * PACK_COLS)
    return jnp.pad(flat, (0, pad)).reshape(-1, PACK_COLS)


def _unpack_flat(packed, shapes):
    flat = packed.reshape(-1)
    out, off = [], 0
    for shp in shapes:
        size = int(np.prod(shp))
        out.append(flat[off:off + size].reshape(shp))
        off += size
    return out


def _gathered_to_full(g, axis):
    n, l, a, b = g.shape
    if axis == 0:
        return jnp.transpose(g, (1, 0, 2, 3)).reshape(l, n * a, b)
    return jnp.transpose(g, (1, 2, 0, 3)).reshape(l, a, n * b)


def _full_to_chunks(w, axis):
    l, a, b = w.shape
    if axis == 0:
        return jnp.transpose(w.reshape(l, N_DEV, a // N_DEV, b), (1, 0, 2, 3))
    return jnp.transpose(w.reshape(l, a, N_DEV, b // N_DEV), (2, 0, 1, 3))


def kernel(x, g_mix, w_in, q_norm, w_uq, kv_norm, w_ukv, rpb, out_norm_a, out_norm_b, out_norm_c, w_out, g_mlp, w_mlp_in, w_mlp_out, g_final, loss_target, m_g_mix, m_w_in, m_q_norm, m_w_uq, m_kv_norm, m_w_ukv, m_rpb, m_out_norm_a, m_out_norm_b, m_out_norm_c, m_w_out, m_g_mlp, m_w_mlp_in, m_w_mlp_out, m_g_final, v_g_mix, v_w_in, v_q_norm, v_w_uq, v_kv_norm, v_w_ukv, v_rpb, v_out_norm_a, v_out_norm_b, v_out_norm_c, v_w_out, v_g_mlp, v_w_mlp_in, v_w_mlp_out, v_g_final):
    given = dict(locals())
    w = {n: given[n] for n in WEIGHTS}
    m = {n: given["m_" + n] for n in WEIGHTS}
    v = {n: given["v_" + n] for n in WEIGHTS}
    s = x.shape[1]
    depth = g_mix.shape[0]
    shard_shapes = [w[n].shape for n, _ in SHARDED]

    gathered = _all_gather(_pack_rows([w[n] for n, _ in SHARDED], BF16), name="gather_weights")
    full = {}
    for (n, axis), g in zip(SHARDED, _unpack_rows(gathered, shard_shapes)):
        full[n] = _gathered_to_full(g, axis)
    stand_in = {n: jnp.broadcast_to(jnp.zeros((), F32), full[n].shape[1:]) for n, _ in SHARDED}
    w_in = full["w_in"]
    full["w_in"] = jnp.concatenate(
        [w_in[..., COLS_A:COLS_A + COLS_B], w_in[..., COLS_A + COLS_B:], w_in[..., :COLS_A]], axis=-1)

    layer_w = [dict(stand_in, **{n: w[n][l] for n in REPLICATED if n != "g_final"}) for l in range(depth)]
    layer_w16 = [{n: full[n][l] for n, _ in SHARDED} for l in range(depth)]
    rope32 = _rope_tables(s, QK_ROPE)
    rope64 = _rope_tables(s, HEAD_DIM)

    def trunk(x0, lws):
        h = x0
        for lw, lw16 in zip(lws, layer_w16):
            h = _layer(h, lw, lw16, rope32, rope64)
        return h

    x_out, pullback = jax.vjp(trunk, x[0], layer_w)
    loss_local, dx_out, d_g_final = _loss_head(x_out, g_final, loss_target[0], name="loss_head")
    dx0, d_layers = pullback(dx_out)
    loss = lax.psum(loss_local, AXES)

    grads_local = {n: jnp.stack([d[n] for d in d_layers]) for n in layer_w[0]}
    grads_local["g_final"] = d_g_final

    chunks = [_full_to_chunks(grads_local[n], axis).reshape(N_DEV, -1, PACK_COLS).astype(BF16)
              for n, axis in SHARDED]
    rows = sum(c.shape[1] for c in chunks)
    pad = -rows % PACK_ROW_ALIGN
    if pad:
        chunks.append(jnp.zeros((N_DEV, pad, PACK_COLS), BF16))
    parts = _all_to_all(jnp.concatenate(chunks, axis=1), name="scatter_grads")
    big = _adamw(parts, _pack_rows([w[n] for n, _ in SHARDED], F32), _pack_rows([m[n] for n, _ in SHARDED], F32),
                 _pack_rows([v[n] for n, _ in SHARDED], F32), name="adamw_sharded")
    big = [_unpack_rows(o, shard_shapes) for o in big]

    rep_shapes = [w[n].shape for n in REPLICATED]
    rep_parts = _all_gather(_pack_flat([grads_local[n] for n in REPLICATED]), name="gather_small_grads")
    small = _adamw(rep_parts, _pack_flat([w[n] for n in REPLICATED]), _pack_flat([m[n] for n in REPLICATED]),
                   _pack_flat([v[n] for n in REPLICATED]), name="adamw_replicated")
    small = [_unpack_flat(o, rep_shapes) for o in small]

    result = {}
    for kind, idx in (("grad", 0), ("delta", 1), ("new_m", 2), ("new_v", 3)):
        for j, (n, _) in enumerate(SHARDED):
            result[kind + "_" + n] = big[idx][j]
        for j, n in enumerate(REPLICATED):
            result[kind + "_" + n] = small[idx][j]
    outs = [loss, dx0[None]]
    for kind in ("grad", "delta", "new_m", "new_v"):
        outs += [result[kind + "_" + n] for n in WEIGHTS]
    return tuple(outs)
```

```python
import functools

import numpy as np
import jax
import jax.numpy as jnp
from jax import lax
from jax.experimental import pallas as pl
from jax.experimental.pallas import tpu as pltpu

F32 = jnp.float32
BF16 = jnp.bfloat16

D_MODEL = 1024
DEPTH = 4
HEAD_DIM = 64
HEADS_A = 6
HEADS_B = 6
HEADS_C = 4
Q_LORA = 256
KV_LORA = 128
QK_NOPE = 64
QK_ROPE = 32
V_DIM_A = 64
DILATED_PAIRS = ((128, 1), (512, 4), (2048, 16))
BAND_HALF = 64
GRID_W = 64
NA_ROWS = 8
NA_COLS = 16
D_FF = 4096
ROPE_THETA = 10000.0
NORM_EPS = 1e-6
NEG_INF = -1e30
LOG2_E = 1.4426950408889634

COLS_A = Q_LORA + KV_LORA + QK_ROPE
COLS_B = 3 * HEADS_B * HEAD_DIM
COLS_C = 3 * HEADS_C * HEAD_DIM
WIDTH_A = HEADS_A * V_DIM_A
WIDTH_B = HEADS_B * HEAD_DIM
WIDTH_C = HEADS_C * HEAD_DIM

ADAM_LR = 0.001
ADAM_B1 = 0.9
ADAM_B2 = 0.999
ADAM_EPS = 1e-08
ADAM_WD = 0.01
ADAM_STEP = 10

N_DEV = 8
AXES = ("x", "y", "c")
MESH = pl.DeviceIdType.MESH

V7X_VMEM_LIMIT = 48 * 1024 * 1024
MM_VMEM_BUDGET = 32 * 1024 * 1024
LANES = 128
PACK_COLS = 1024
PACK_ROW_ALIGN = 16
ADAM_MAX_ROWS = 160
DENSE_TQ_FWD = 512
DENSE_TK_FWD = 1024
DENSE_TQ = 256
DENSE_TK_BWD = 1024
WIN_TILES_PER_STEP = 4
BAND_TILE = 256
NA_TILE_ROWS = 4


def _params(sem=None):
    return pltpu.CompilerParams(dimension_semantics=sem, vmem_limit_bytes=V7X_VMEM_LIMIT)


def _tile(dim, pref):
    if dim <= pref:
        return dim
    for t in range(pref - pref % LANES, LANES - 1, -LANES):
        if dim % t == 0:
            return t
    return dim


def _mm(a, b, *, ta=False, tb=False, out_dtype=F32, epilogue=None, extra=None, splits=None, name):
    if ta:
        k_dim, m_dim = a.shape
    else:
        m_dim, k_dim = a.shape
    n_dim = b.shape[0] if tb else b.shape[1]
    tn = _tile(n_dim, 1024) if n_dim % LANES == 0 else n_dim
    tk = _tile(k_dim, 1024 if not ta else 512) if k_dim % LANES == 0 else k_dim
    if epilogue == "split":
        tn = n_dim
        assert sum(splits) == n_dim and all(off % LANES == 0 for off in np.cumsum(splits)[:-1])
    n_out = 2 if epilogue == "relu2" else 1
    for tm in (_tile(m_dim, 1024), _tile(m_dim, 512), _tile(m_dim, 256)):
        blocks = (tm * tk * a.dtype.itemsize + tk * tn * b.dtype.itemsize
                  + (tm * tn * 4 if extra is not None else 0) + n_out * tm * tn * 4)
        if 2 * blocks + tm * tn * 4 <= MM_VMEM_BUDGET:
            break
    nk = k_dim // tk
    a_spec = (pl.BlockSpec((tk, tm), lambda i, j, k: (k, i)) if ta
              else pl.BlockSpec((tm, tk), lambda i, j, k: (i, k)))
    b_spec = (pl.BlockSpec((tn, tk), lambda i, j, k: (j, k)) if tb
              else pl.BlockSpec((tk, tn), lambda i, j, k: (k, j)))
    o_spec = pl.BlockSpec((tm, tn), lambda i, j, k: (i, j))
    dims = (((0 if ta else 1,), (1 if tb else 0,)), ((), ()))
    in_specs = [a_spec, b_spec]
    operands = [a, b]
    if epilogue in ("drelu2", "add"):
        in_specs.append(o_spec)
        operands.append(extra)
    if epilogue == "relu2":
        out_shape = (jax.ShapeDtypeStruct((m_dim, n_dim), F32), jax.ShapeDtypeStruct((m_dim, n_dim), BF16))
        out_specs = (o_spec, o_spec)
    elif epilogue == "split":
        out_shape = tuple(jax.ShapeDtypeStruct((m_dim, w), out_dtype) for w in splits)
        out_specs = tuple(pl.BlockSpec((tm, w), lambda i, j, k: (i, 0)) for w in splits)
    else:
        out_shape = jax.ShapeDtypeStruct((m_dim, n_dim), out_dtype)
        out_specs = o_spec

    def kern(*refs):
        acc_ref = refs[-1]
        a_ref, b_ref = refs[0], refs[1]
        k = pl.program_id(2)

        @pl.when(k == 0)
        def _():
            acc_ref[...] = jnp.zeros_like(acc_ref)

        acc_ref[...] += lax.dot_general(a_ref[...].astype(BF16), b_ref[...].astype(BF16), dims,
                                        preferred_element_type=F32)

        @pl.when(k == nk - 1)
        def _():
            acc = acc_ref[...]
            if epilogue == "relu2":
                refs[2][...] = acc
                r = jnp.maximum(acc, 0.0)
                refs[3][...] = (r * r).astype(BF16)
            elif epilogue == "drelu2":
                refs[3][...] = (acc * (2.0 * jnp.maximum(refs[2][...], 0.0))).astype(out_dtype)
            elif epilogue == "add":
                refs[3][...] = (acc + refs[2][...]).astype(out_dtype)
            elif epilogue == "split":
                off = 0
                for part, w in enumerate(splits):
                    refs[2 + part][...] = acc[:, off:off + w].astype(out_dtype)
                    off += w
            else:
                refs[2][...] = acc.astype(out_dtype)

    return pl.pallas_call(
        kern, name=name, out_shape=out_shape, grid=(m_dim // tm, n_dim // tn, nk),
        in_specs=in_specs, out_specs=out_specs, scratch_shapes=[pltpu.VMEM((tm, tn), F32)],
        compiler_params=_params(("parallel", "parallel", "arbitrary")),
    )(*operands)


def _norm_fwd(x, g, *, out_dtype, name):
    s, w = x.shape
    ts = _tile(s, 512)

    def kern(x_ref, g_ref, y_ref):
        xv = x_ref[...]
        r = lax.rsqrt(jnp.mean(xv * xv, axis=-1, keepdims=True) + NORM_EPS)
        y_ref[...] = (xv * r * g_ref[...]).astype(out_dtype)

    return pl.pallas_call(
        kern, name=name, out_shape=jax.ShapeDtypeStruct((s, w), out_dtype), grid=(s // ts,),
        in_specs=[pl.BlockSpec((ts, w), lambda i: (i, 0)), pl.BlockSpec((1, w), lambda i: (0, 0))],
        out_specs=pl.BlockSpec((ts, w), lambda i: (i, 0)),
        compiler_params=_params(("parallel",)),
    )(x, g.reshape(1, w))


def _norm_bwd(x, g, dy, *, name):
    s, w = x.shape
    ts = _tile(s, 512)

    def kern(x_ref, g_ref, dy_ref, dx_ref, dg_ref):
        @pl.when(pl.program_id(0) == 0)
        def _():
            dg_ref[...] = jnp.zeros_like(dg_ref)

        xv = x_ref[...]
        dyv = dy_ref[...]
        r = lax.rsqrt(jnp.mean(xv * xv, axis=-1, keepdims=True) + NORM_EPS)
        xhat = xv * r
        dg_ref[...] += jnp.sum(dyv * xhat, axis=0, keepdims=True)
        dxhat = dyv * g_ref[...]
        dx_ref[...] = r * (dxhat - xhat * jnp.mean(dxhat * xhat, axis=-1, keepdims=True))

    dx, dg = pl.pallas_call(
        kern, name=name,
        out_shape=(jax.ShapeDtypeStruct((s, w), F32), jax.ShapeDtypeStruct((1, w), F32)), grid=(s // ts,),
        in_specs=[pl.BlockSpec((ts, w), lambda i: (i, 0)), pl.BlockSpec((1, w), lambda i: (0, 0)),
                  pl.BlockSpec((ts, w), lambda i: (i, 0))],
        out_specs=(pl.BlockSpec((ts, w), lambda i: (i, 0)), pl.BlockSpec((1, w), lambda i: (0, 0))),
        compiler_params=_params(("arbitrary",)),
    )(x, g.reshape(1, w), dy)
    return dx, dg.reshape(w)


def _loss_head(x, g, target, *, name):
    s, w = x.shape
    ts = _tile(s, 512)

    def kern(x_ref, g_ref, t_ref, loss_ref, dx_ref, dg_ref):
        @pl.when(pl.program_id(0) == 0)
        def _():
            dg_ref[...] = jnp.zeros_like(dg_ref)
            loss_ref[...] = jnp.zeros_like(loss_ref)

        xv = x_ref[...]
        gv = g_ref[...]
        r = lax.rsqrt(jnp.mean(xv * xv, axis=-1, keepdims=True) + NORM_EPS)
        xhat = xv * r
        err = xhat * gv - t_ref[...]
        loss_ref[...] += 0.5 * jnp.sum(jnp.mean(err * err, axis=-1, keepdims=True))
        dyv = err * (1.0 / w)
        dg_ref[...] += jnp.sum(dyv * xhat, axis=0, keepdims=True)
        dxhat = dyv * gv
        dx_ref[...] = r * (dxhat - xhat * jnp.mean(dxhat * xhat, axis=-1, keepdims=True))

    loss, dx, dg = pl.pallas_call(
        kern, name=name,
        out_shape=(jax.ShapeDtypeStruct((1, LANES), F32), jax.ShapeDtypeStruct((s, w), F32),
                   jax.ShapeDtypeStruct((1, w), F32)),
        grid=(s // ts,),
        in_specs=[pl.BlockSpec((ts, w), lambda i: (i, 0)), pl.BlockSpec((1, w), lambda i: (0, 0)),
                  pl.BlockSpec((ts, w), lambda i: (i, 0))],
        out_specs=(pl.BlockSpec((1, LANES), lambda i: (0, 0)), pl.BlockSpec((ts, w), lambda i: (i, 0)),
                   pl.BlockSpec((1, w), lambda i: (0, 0))),
        compiler_params=_params(("arbitrary",)),
    )(x, g.reshape(1, w), target)
    return loss[0, 0], dx, dg.reshape(w)


def _dense_fwd(q, k, v_ones, *, dv, scale, name):
    g_dim, n, dq = q.shape
    dve = v_ones.shape[-1]
    tq = _tile(n, DENSE_TQ_FWD)
    tk = _tile(n, DENSE_TK_FWD)
    nk = n // tk
    to_log2 = scale * LOG2_E

    def kern(q_ref, k_ref, v_ref, o_ref, lse_ref):
        qb = q_ref[0]

        def body(j, carry):
            m, acc = carry
            start = pl.multiple_of(j * tk, tk)
            kb = k_ref[0, pl.ds(start, tk), :]
            s = lax.dot_general(qb, kb, (((1,), (1,)), ((), ())), preferred_element_type=F32)
            m_new = jnp.maximum(m, jnp.max(s, axis=-1, keepdims=True))
            p = jnp.exp2(((s - m_new) * to_log2).astype(BF16))
            acc = jnp.exp2((m - m_new) * to_log2) * acc + jnp.dot(
                p, v_ref[0, pl.ds(start, tk), :], preferred_element_type=F32)
            return m_new, acc

        m, acc = lax.fori_loop(0, nk, body, (jnp.full((tq, 1), NEG_INF, F32), jnp.zeros((tq, dve), F32)))
        l = acc[:, dv:dv + 1]
        o_ref[0] = acc[:, :dv] / l
        lse_ref[0] = m * scale + jnp.log(l)

    return pl.pallas_call(
        kern, name=name,
        out_shape=(jax.ShapeDtypeStruct((g_dim, n, dv), F32), jax.ShapeDtypeStruct((g_dim, n, 1), F32)),
        grid=(g_dim, n // tq),
        in_specs=[pl.BlockSpec((1, tq, dq), lambda g, i: (g, i, 0)), pl.BlockSpec((1, n, dq), lambda g, i: (g, 0, 0)),
                  pl.BlockSpec((1, n, dve), lambda g, i: (g, 0, 0))],
        out_specs=(pl.BlockSpec((1, tq, dv), lambda g, i: (g, i, 0)), pl.BlockSpec((1, tq, 1), lambda g, i: (g, i, 0))),
        compiler_params=_params(("parallel", "arbitrary")),
    )(q, k, v_ones)


def _dense_bwd(q_t, k, k_t, v, o_t, lse, do_t, *, scale, name):
    g_dim, dq, n = q_t.shape
    dv = v.shape[-1]
    tq = _tile(n, DENSE_TQ)
    tk = _tile(n, DENSE_TK_BWD)
    nk = n // tk
    nt = (((1,), (1,)), ((), ()))

    def kern(q_ref, k_ref, kt_ref, v_ref, o_ref, lse_ref, do_ref, dq_ref, dk_ref, dv_ref):
        @pl.when(pl.program_id(1) == 0)
        def _():
            dk_ref[...] = jnp.zeros_like(dk_ref)
            dv_ref[...] = jnp.zeros_like(dv_ref)

        qb = q_ref[0]
        dob = do_ref[0]
        delta = jnp.sum(dob * o_ref[0], axis=0, keepdims=True)
        dob16 = dob.astype(BF16)
        lse_v = lse_ref[0]

        def body(j, dq_acc):
            start = pl.multiple_of(j * tk, tk)
            s = jnp.dot(k_ref[0, pl.ds(start, tk), :], qb, preferred_element_type=F32) * scale
            p = jnp.exp(s - lse_v)
            dp = jnp.dot(v_ref[0, pl.ds(start, tk), :], dob16, preferred_element_type=F32)
            ds16 = (p * (dp - delta) * scale).astype(BF16)
            dv_ref[0, pl.ds(start, tk), :] += lax.dot_general(p.astype(BF16), dob16, nt, preferred_element_type=F32)
            dk_ref[0, pl.ds(start, tk), :] += lax.dot_general(ds16, qb, nt, preferred_element_type=F32)
            return dq_acc + jnp.dot(kt_ref[0, :, pl.ds(start, tk)], ds16, preferred_element_type=F32)

        dq_ref[0] = lax.fori_loop(0, nk, body, jnp.zeros((dq, tq), F32))

    qt_spec = pl.BlockSpec((1, dq, tq), lambda g, i: (g, 0, i))
    ot_spec = pl.BlockSpec((1, dv, tq), lambda g, i: (g, 0, i))
    l_spec = pl.BlockSpec((1, 1, tq), lambda g, i: (g, 0, i))
    k_spec = pl.BlockSpec((1, n, dq), lambda g, i: (g, 0, 0))
    kt_spec = pl.BlockSpec((1, dq, n), lambda g, i: (g, 0, 0))
    v_spec = pl.BlockSpec((1, n, dv), lambda g, i: (g, 0, 0))
    return pl.pallas_call(
        kern, name=name,
        out_shape=(jax.ShapeDtypeStruct((g_dim, dq, n), F32), jax.ShapeDtypeStruct((g_dim, n, dq), F32),
                   jax.ShapeDtypeStruct((g_dim, n, dv), F32)),
        grid=(g_dim, n // tq), in_specs=[qt_spec, k_spec, kt_spec, v_spec, ot_spec, l_spec, ot_spec],
        out_specs=(qt_spec, k_spec, v_spec), compiler_params=_params(("parallel", "arbitrary")),
    )(q_t, k, k_t, v, o_t, lse, do_t)


def _make_dense_attention(scale, name):
    @jax.custom_vjp
    def att(q, k, v):
        return att_fwd(q, k, v)[0]

    def att_fwd(q, k, v):
        q16 = jnp.transpose(q.astype(BF16), (1, 0, 2))
        k16 = jnp.transpose(k.astype(BF16), (1, 0, 2))
        v16 = jnp.transpose(v.astype(BF16), (1, 0, 2))
        h, n, dv = v16.shape
        v_ones = jnp.concatenate([v16, jnp.ones((h, n, 1), BF16), jnp.zeros((h, n, LANES - dv - 1), BF16)], axis=-1)
        o, lse = _dense_fwd(q16, k16, v_ones, dv=dv, scale=scale, name=name + "_fwd")
        return jnp.transpose(o, (1, 0, 2)), (q16, k16, v16, o, lse)

    def att_bwd(res, do):
        q16, k16, v16, o, lse = res
        dq_t, dk, dv_ = _dense_bwd(jnp.transpose(q16, (0, 2, 1)), k16, jnp.transpose(k16, (0, 2, 1)), v16,
                                   jnp.transpose(o, (0, 2, 1)), jnp.transpose(lse, (0, 2, 1)),
                                   jnp.transpose(do, (1, 2, 0)), scale=scale, name=name + "_bwd")
        return jnp.transpose(dq_t, (2, 0, 1)), jnp.transpose(dk, (1, 0, 2)), jnp.transpose(dv_, (1, 0, 2))

    att.defvjp(att_fwd, att_bwd)
    return att


def _window(mode, n):
    if mode == "band":
        tq = min(BAND_TILE, n)
        return tq, BAND_HALF, tq + 2 * BAND_HALF
    tq = NA_TILE_ROWS * GRID_W
    front = (NA_ROWS // 2) * GRID_W
    return tq, front, tq + NA_ROWS * GRID_W


def _span_start(mode, i0, front, span, n):
    return i0, i0 - front


def _window_mask(mode, i0, k0, tq, span, n):
    q_pos = i0 + lax.broadcasted_iota(jnp.int32, (tq, 1), 0)
    k_pos = k0 + lax.broadcasted_iota(jnp.int32, (1, span), 1)
    if mode == "band":
        diff = k_pos - q_pos
        return (diff <= BAND_HALF) & (diff >= -BAND_HALF) & (k_pos >= 0) & (k_pos < n)
    rows = n // GRID_W
    shift = GRID_W.bit_length() - 1
    r_start = jnp.clip((q_pos >> shift) - NA_ROWS // 2, 0, rows - NA_ROWS)
    c_start = jnp.clip((q_pos & (GRID_W - 1)) - NA_COLS // 2, 0, GRID_W - NA_COLS)
    kr = k_pos >> shift
    kc = k_pos & (GRID_W - 1)
    return (kr >= r_start) & (kr < r_start + NA_ROWS) & (kc >= c_start) & (kc < c_start + NA_COLS)


def _win_fwd(q, k, v, bias, *, mode, scale, name):
    g_dim, n, d = q.shape
    tq, front, span = _window(mode, n)
    sub = min(WIN_TILES_PER_STEP, n // tq)
    n_pad = k.shape[1]

    def kern(*refs):
        if mode == "natten":
            q_ref, k_ref, v_ref, b_ref, o_ref, lse_ref = refs
        else:
            q_ref, k_ref, v_ref, o_ref, lse_ref = refs
        for t in range(sub):
            rows = pl.ds(t * tq, tq)
            i0 = pl.multiple_of(pl.program_id(1) * (sub * tq) + t * tq, tq)
            start, k0 = _span_start(mode, i0, front, span, n)
            kb = k_ref[0, pl.ds(start, span), :]
            vb = v_ref[0, pl.ds(start, span), :]
            s = lax.dot_general(q_ref[0, rows, :], kb, (((1,), (1,)), ((), ())), preferred_element_type=F32) * scale
            if mode == "natten":
                s = s + b_ref[0]
            s = jnp.where(_window_mask(mode, i0, k0, tq, span, n), s, NEG_INF)
            m = jnp.max(s, axis=-1, keepdims=True)
            p = jnp.exp(s - m)
            l = jnp.sum(p, axis=-1, keepdims=True)
            o_ref[0, rows, :] = jnp.dot(p.astype(BF16), vb, preferred_element_type=F32) / l
            lse_ref[0, rows, :] = m + jnp.log(l)

    in_specs = [pl.BlockSpec((1, sub * tq, d), lambda g, i: (g, i, 0)),
                pl.BlockSpec((1, n_pad, d), lambda g, i: (g, 0, 0)), pl.BlockSpec((1, n_pad, d), lambda g, i: (g, 0, 0))]
    operands = [q, k, v]
    if mode == "natten":
        in_specs.append(pl.BlockSpec((1, tq, span), lambda g, i: (g, 0, 0)))
        operands.append(bias)
    return pl.pallas_call(
        kern, name=name,
        out_shape=(jax.ShapeDtypeStruct((g_dim, n, d), F32), jax.ShapeDtypeStruct((g_dim, n, 1), F32)),
        grid=(g_dim, n // (sub * tq)), in_specs=in_specs,
        out_specs=(pl.BlockSpec((1, sub * tq, d), lambda g, i: (g, i, 0)),
                   pl.BlockSpec((1, sub * tq, 1), lambda g, i: (g, i, 0))),
        compiler_params=_params(("parallel", "arbitrary")),
    )(*operands)


def _win_bwd(q, k, v, bias, o, lse, do, dlse, *, mode, scale, name):
    g_dim, n, d = q.shape
    tq, front, span = _window(mode, n)
    sub = min(WIN_TILES_PER_STEP, n // tq)
    n_pad = k.shape[1]

    def kern(*refs):
        if mode == "natten":
            q_ref, k_ref, v_ref, b_ref, o_ref, lse_ref, do_ref, dlse_ref, dq_ref, dk_ref, dv_ref, db_ref = refs
        else:
            q_ref, k_ref, v_ref, o_ref, lse_ref, do_ref, dlse_ref, dq_ref, dk_ref, dv_ref = refs

        @pl.when(pl.program_id(1) == 0)
        def _():
            dk_ref[...] = jnp.zeros_like(dk_ref)
            dv_ref[...] = jnp.zeros_like(dv_ref)
            if mode == "natten":
                db_ref[...] = jnp.zeros_like(db_ref)

        for t in range(sub):
            rows = pl.ds(t * tq, tq)
            i0 = pl.multiple_of(pl.program_id(1) * (sub * tq) + t * tq, tq)
            qb = q_ref[0, rows, :]
            start, k0 = _span_start(mode, i0, front, span, n)
            kb = k_ref[0, pl.ds(start, span), :]
            vb = v_ref[0, pl.ds(start, span), :]
            dob = do_ref[0, rows, :]
            delta = jnp.sum(dob * o_ref[0, rows, :], axis=-1, keepdims=True) - dlse_ref[0, rows, :]
            dob16 = dob.astype(BF16)
            s = lax.dot_general(qb, kb, (((1,), (1,)), ((), ())), preferred_element_type=F32) * scale
            if mode == "natten":
                s = s + b_ref[0]
            p = jnp.where(_window_mask(mode, i0, k0, tq, span, n), jnp.exp(s - lse_ref[0, rows, :]), 0.0)
            dp = lax.dot_general(dob16, vb, (((1,), (1,)), ((), ())), preferred_element_type=F32)
            ds = p * (dp - delta)
            if mode == "natten":
                db_ref[0] += ds
            ds16 = (ds * scale).astype(BF16)
            dv_ref[0, pl.ds(start, span), :] += lax.dot_general(
                p.astype(BF16), dob16, (((0,), (0,)), ((), ())), preferred_element_type=F32)
            dk_ref[0, pl.ds(start, span), :] += lax.dot_general(
                ds16, qb, (((0,), (0,)), ((), ())), preferred_element_type=F32)
            dq_ref[0, rows, :] = jnp.dot(ds16, kb, preferred_element_type=F32)

    q_spec = pl.BlockSpec((1, sub * tq, d), lambda g, i: (g, i, 0))
    k_spec = pl.BlockSpec((1, n_pad, d), lambda g, i: (g, 0, 0))
    l_spec = pl.BlockSpec((1, sub * tq, 1), lambda g, i: (g, i, 0))
    in_specs = [q_spec, k_spec, k_spec]
    operands = [q, k, v]
    out_shape = [jax.ShapeDtypeStruct((g_dim, n, d), F32), jax.ShapeDtypeStruct((g_dim, n_pad, d), F32),
                 jax.ShapeDtypeStruct((g_dim, n_pad, d), F32)]
    out_specs = [q_spec, k_spec, k_spec]
    if mode == "natten":
        b_spec = pl.BlockSpec((1, tq, span), lambda g, i: (g, 0, 0))
        in_specs.append(b_spec)
        operands.append(bias)
        out_shape.append(jax.ShapeDtypeStruct(bias.shape, F32))
        out_specs.append(b_spec)
    in_specs += [q_spec, l_spec, q_spec, l_spec]
    operands += [o, lse, do, dlse]
    return pl.pallas_call(
        kern, name=name, out_shape=tuple(out_shape), grid=(g_dim, n // (sub * tq)), in_specs=in_specs,
        out_specs=tuple(out_specs), compiler_params=_params(("parallel", "arbitrary")),
    )(*operands)


def _dilate(t, dil):
    s, h, d = t.shape
    return jnp.transpose(t.reshape(s // dil, dil, h, d), (1, 2, 0, 3)).reshape(dil * h, s // dil, d)


def _undilate(t, dil):
    gh, n, d = t.shape
    return jnp.transpose(t.reshape(dil, gh // dil, n, d), (2, 0, 1, 3)).reshape(n * dil, gh // dil, d)


def _make_window_attention(mode, scale, dil, name):
    @jax.custom_vjp
    def att(q, k, v, bias):
        return att_fwd(q, k, v, bias)[0]

    def att_fwd(q, k, v, bias):
        tq, front, span = _window(mode, q.shape[0] // dil)
        q16, k16, v16 = (_dilate(t.astype(BF16), dil) for t in (q, k, v))
        pad = ((0, 0), (front, span - tq - front), (0, 0))
        k16, v16 = jnp.pad(k16, pad), jnp.pad(v16, pad)
        o, lse = _win_fwd(q16, k16, v16, bias, mode=mode, scale=scale, name=name + "_fwd")
        return (_undilate(o, dil), _undilate(lse, dil)[..., 0]), (q16, k16, v16, bias, o, lse)

    def att_bwd(res, cts):
        q16, k16, v16, bias, o, lse = res
        do, dlse = cts
        n = q16.shape[1]
        outs = _win_bwd(q16, k16, v16, bias, o, lse, _dilate(do, dil), _dilate(dlse[..., None], dil),
                        mode=mode, scale=scale, name=name + "_bwd")
        front = _window(mode, n)[1]
        dq, dk, dv = outs[0], outs[1][:, front:front + n], outs[2][:, front:front + n]
        dbias = outs[3] if mode == "natten" else None
        return _undilate(dq, dil), _undilate(dk, dil), _undilate(dv, dil), dbias

    att.defvjp(att_fwd, att_bwd)
    return att


def _na_onehots():
    tq, front, span = _window("natten", 0)
    q_rows, k_rows, front_rows = tq // GRID_W, span // GRID_W, front // GRID_W
    e_r = np.zeros((q_rows, k_rows, 2 * NA_ROWS - 1), np.float32)
    for qr in range(q_rows):
        for kr in range(k_rows):
            a = kr - front_rows - qr + NA_ROWS - 1
            if 0 <= a < 2 * NA_ROWS - 1:
                e_r[qr, kr, a] = 1.0
    e_c = np.zeros((GRID_W, GRID_W, 2 * NA_COLS - 1), np.float32)
    for qc in range(GRID_W):
        for kc in range(GRID_W):
            b = kc - qc + NA_COLS - 1
            if 0 <= b < 2 * NA_COLS - 1:
                e_c[qc, kc, b] = 1.0
    return e_r, e_c


def _na_bias_tiles(rpb):
    tq, _, span = _window("natten", 0)
    e_r, e_c = _na_onehots()
    t = jnp.einsum("ikA,xyB,hAB->hixky", e_r, e_c, rpb, precision=lax.Precision.HIGHEST)
    return t.reshape(rpb.shape[0], tq, span)


def _make_norm_linear(name):
    @jax.custom_vjp
    def f(x, g, w_grad, w):
        return f_fwd(x, g, w_grad, w)[0]

    def f_fwd(x, g, w_grad, w):
        h = _norm_fwd(x, g, out_dtype=BF16, name=name + "_norm")
        return _mm(h, w, name=name + "_mm"), (x, g, w, h)

    def f_bwd(res, dy):
        x, g, w, h = res
        dh = _mm(dy, w, tb=True, name=name + "_dh")
        dw = _mm(h, dy, ta=True, name=name + "_dw")
        dx, dg = _norm_bwd(x, g, dh, name=name + "_dnorm")
        return dx, dg, dw, jnp.zeros_like(w)

    f.defvjp(f_fwd, f_bwd)
    return f


def _make_in_proj(name):
    splits = (COLS_B, COLS_C, COLS_A)
    offs = (0, COLS_B, COLS_B + COLS_C)

    @jax.custom_vjp
    def f(x, g, w_grad, w):
        return f_fwd(x, g, w_grad, w)[0]

    def f_fwd(x, g, w_grad, w):
        h = _norm_fwd(x, g, out_dtype=BF16, name=name + "_norm")
        return _mm(h, w, epilogue="split", splits=splits, name=name + "_mm"), (x, g, w, h)

    def f_bwd(res, dys):
        x, g, w, h = res
        dh = None
        dws = []
        for dy, off, width, tag in zip(dys, offs, splits, "bca"):
            w_part = w[:, off:off + width]
            dh = _mm(dy, w_part, tb=True, epilogue=None if dh is None else "add", extra=dh, name=name + "_dh_" + tag)
            dws.append(_mm(h, dy, ta=True, name=name + "_dw_" + tag))
        dx, dg = _norm_bwd(x, g, dh, name=name + "_dnorm")
        return dx, dg, jnp.concatenate([dws[2], dws[0], dws[1]], axis=1), jnp.zeros_like(w)

    f.defvjp(f_fwd, f_bwd)
    return f


def _make_mix_out(name):
    @jax.custom_vjp
    def f(x, oa, ob, oc, ga, gb, gc, w_grad, w):
        return f_fwd(x, oa, ob, oc, ga, gb, gc, w_grad, w)[0]

    def f_fwd(x, oa, ob, oc, ga, gb, gc, w_grad, w):
        mixed = jnp.concatenate([
            _norm_fwd(oa, ga, out_dtype=BF16, name=name + "_norm_a"),
            _norm_fwd(ob, gb, out_dtype=BF16, name=name + "_norm_b"),
            _norm_fwd(oc, gc, out_dtype=BF16, name=name + "_norm_c")], axis=-1)
        y = _mm(mixed, w, epilogue="add", extra=x, name=name + "_mm")
        return y, (oa, ob, oc, ga, gb, gc, w, mixed)

    def f_bwd(res, dy):
        oa, ob, oc, ga, gb, gc, w, mixed = res
        dmixed = _mm(dy, w, tb=True, name=name + "_dmixed")
        dw = _mm(mixed, dy, ta=True, name=name + "_dw")
        doa, dga = _norm_bwd(oa, ga, dmixed[:, :WIDTH_A], name=name + "_dnorm_a")
        dob, dgb = _norm_bwd(ob, gb, dmixed[:, WIDTH_A:WIDTH_A + WIDTH_B], name=name + "_dnorm_b")
        doc, dgc = _norm_bwd(oc, gc, dmixed[:, WIDTH_A + WIDTH_B:], name=name + "_dnorm_c")
        return dy, doa, dob, doc, dga, dgb, dgc, dw, jnp.zeros_like(w)

    f.defvjp(f_fwd, f_bwd)
    return f


def _make_mlp(name):
    @jax.custom_vjp
    def f(x, g, w1_grad, w2_grad, w1, w2):
        return f_fwd(x, g, w1_grad, w2_grad, w1, w2)[0]

    def f_fwd(x, g, w1_grad, w2_grad, w1, w2):
        h = _norm_fwd(x, g, out_dtype=BF16, name=name + "_norm")
        u, a = _mm(h, w1, epilogue="relu2", name=name + "_up")
        y = _mm(a, w2, epilogue="add", extra=x, name=name + "_down")
        return y, (x, g, w1, w2, h, u, a)

    def f_bwd(res, dy):
        x, g, w1, w2, h, u, a = res
        du = _mm(dy, w2, tb=True, epilogue="drelu2", extra=u, out_dtype=BF16, name=name + "_du")
        dw2 = _mm(a, dy, ta=True, name=name + "_dw2")
        dw1 = _mm(h, du, ta=True, name=name + "_dw1")
        dh = _mm(du, w1, tb=True, name=name + "_dh")
        dx, dg = _norm_bwd(x, g, dh, name=name + "_dnorm")
        return dx + dy, dg, dw1, dw2, jnp.zeros_like(w1), jnp.zeros_like(w2)

    f.defvjp(f_fwd, f_bwd)
    return f


def _rope_tables(s, dim):
    half = dim // 2
    inv_freq = ROPE_THETA ** (-jnp.arange(half, dtype=F32) / half)
    ang = jnp.arange(s, dtype=F32)[:, None] * inv_freq[None, :]
    return jnp.cos(ang)[:, None, :], jnp.sin(ang)[:, None, :]


def _rope(x, cos, sin):
    half = x.shape[-1] // 2
    x1, x2 = x[..., :half], x[..., half:]
    return jnp.concatenate([x1 * cos - x2 * sin, x1 * sin + x2 * cos], axis=-1)


def _layer(x, lw, lw16, rope32, rope64):
    s = x.shape[0]
    p_b, p_c, p_a = _make_in_proj("in")(x, lw["g_mix"], lw["w_in"], lw16["w_in"])
    c_q = p_a[:, :Q_LORA]
    c_kv = p_a[:, Q_LORA:Q_LORA + KV_LORA]
    k_pe = p_a[:, Q_LORA + KV_LORA:]
    p_b = p_b.reshape(s, 3, HEADS_B, HEAD_DIM)
    p_c = p_c.reshape(s, 3, HEADS_C, HEAD_DIM)

    qa = _make_norm_linear("uq")(c_q, lw["q_norm"], lw["w_uq"], lw16["w_uq"]).reshape(
        s, HEADS_A, QK_NOPE + QK_ROPE)
    kva = _make_norm_linear("ukv")(c_kv, lw["kv_norm"], lw["w_ukv"], lw16["w_ukv"]).reshape(
        s, HEADS_A, QK_NOPE + V_DIM_A)
    k_pe = jnp.broadcast_to(_rope(k_pe[:, None, :], *rope32), (s, HEADS_A, QK_ROPE))
    qa = jnp.concatenate([qa[..., :QK_NOPE], _rope(qa[..., QK_NOPE:], *rope32)], axis=-1)
    ka = jnp.concatenate([kva[..., :QK_NOPE], k_pe], axis=-1)
    att_a = _make_dense_attention((QK_NOPE + QK_ROPE) ** -0.5, "att_a")
    o_a = att_a(qa, ka, kva[..., QK_NOPE:]).reshape(s, WIDTH_A)

    qb = _rope(p_b[:, 0], *rope64)
    kb = _rope(p_b[:, 1], *rope64)
    vb = p_b[:, 2]
    outs, lses = [], []
    for _, dil in DILATED_PAIRS:
        o, lse = _make_window_attention("band", HEAD_DIM ** -0.5, dil, "att_b%d" % dil)(qb, kb, vb, None)
        outs.append(o)
        lses.append(lse)
    wgt = jax.nn.softmax(jnp.stack(lses, axis=-1), axis=-1)
    o_b = jnp.sum(jnp.stack(outs, axis=-1) * wgt[:, :, None, :], axis=-1).reshape(s, WIDTH_B)

    att_c = _make_window_attention("natten", HEAD_DIM ** -0.5, 1, "att_c")
    o_c, _ = att_c(p_c[:, 0], p_c[:, 1], p_c[:, 2], _na_bias_tiles(lw["rpb"]))
    o_c = o_c.reshape(s, WIDTH_C)

    x = _make_mix_out("out")(x, o_a, o_b, o_c, lw["out_norm_a"], lw["out_norm_b"], lw["out_norm_c"],
                             lw["w_out"], lw16["w_out"])
    return _make_mlp("mlp")(x, lw["g_mlp"], lw["w_mlp_in"], lw["w_mlp_out"], lw16["w_mlp_in"], lw16["w_mlp_out"])


def _place():
    return lax.axis_index("x"), lax.axis_index("y"), lax.axis_index("c")


def _all_gather(block, *, name):
    r, c_dim = block.shape

    def body(x_ref, out_ref, send_sems, recv_sems, local_sem):
        x, y, c = _place()
        me, sibling = (x, y, c), (x, y, 1 - c)
        chips = [(1 - x, y), (x, 1 - y), (1 - x, 1 - y)]

        def slot(px, py, pc):
            return out_ref.at[4 * px + 2 * py + pc]

        def copy(k, blk, to, src=None):
            return pltpu.make_async_remote_copy(
                src_ref=slot(*blk) if src is None else src, dst_ref=slot(*blk),
                send_sem=send_sems.at[k], recv_sem=recv_sems.at[k], device_id=to, device_id_type=MESH)

        mine = pltpu.make_async_copy(x_ref, slot(*me), local_sem)
        mine.start()
        first = [copy(0, me, sibling, src=x_ref)]
        first += [copy(1 + j, me, (*chip, c), src=x_ref) for j, chip in enumerate(chips)]
        for cp in first:
            cp.start()
        passed = [copy(4 + j, (*chip, c), sibling) for j, chip in enumerate(chips)]
        for j, chip in enumerate(chips):
            copy(1 + j, (*chip, c), me).wait_recv()
            passed[j].start()
        copy(0, sibling, me).wait_recv()
        for j, chip in enumerate(chips):
            copy(4 + j, (*chip, 1 - c), me).wait_recv()
        for cp in first + passed:
            cp.wait_send()
        mine.wait()

    return pl.pallas_call(
        body, name=name, out_shape=jax.ShapeDtypeStruct((N_DEV, r, c_dim), block.dtype),
        in_specs=[pl.BlockSpec(memory_space=pl.ANY)], out_specs=pl.BlockSpec(memory_space=pl.ANY),
        scratch_shapes=[pltpu.SemaphoreType.DMA((7,)), pltpu.SemaphoreType.DMA((7,)), pltpu.SemaphoreType.DMA(())],
    )(block)


def _all_to_all(chunks, *, name):
    def body(in_ref, out_ref, send_sems, recv_sems, local_sem):
        x, y, c = _place()
        me = 4 * x + 2 * y + c
        mine = pltpu.make_async_copy(in_ref.at[me], out_ref.at[me], local_sem)
        mine.start()
        peers = []
        for k in range(1, N_DEV):
            px = 1 - x if k & 4 else x
            py = 1 - y if k & 2 else y
            pc = 1 - c if k & 1 else c
            peers.append((px, py, pc))

        def copy(k, peer):
            pid = 4 * peer[0] + 2 * peer[1] + peer[2]
            return pltpu.make_async_remote_copy(
                src_ref=in_ref.at[pid], dst_ref=out_ref.at[me], send_sem=send_sems.at[k], recv_sem=recv_sems.at[k],
                device_id=peer, device_id_type=MESH)

        def landing(k, peer):
            pid = 4 * peer[0] + 2 * peer[1] + peer[2]
            return pltpu.make_async_remote_copy(
                src_ref=in_ref.at[pid], dst_ref=out_ref.at[pid], send_sem=send_sems.at[k], recv_sem=recv_sems.at[k],
                device_id=peer, device_id_type=MESH)

        sends = [copy(k, peer) for k, peer in enumerate(peers)]
        for cp in sends:
            cp.start()
        for k, peer in enumerate(peers):
            landing(k, peer).wait_recv()
        for cp in sends:
            cp.wait_send()
        mine.wait()

    return pl.pallas_call(
        body, name=name, out_shape=jax.ShapeDtypeStruct(chunks.shape, chunks.dtype),
        in_specs=[pl.BlockSpec(memory_space=pl.ANY)], out_specs=pl.BlockSpec(memory_space=pl.ANY),
        scratch_shapes=[pltpu.SemaphoreType.DMA((7,)), pltpu.SemaphoreType.DMA((7,)), pltpu.SemaphoreType.DMA(())],
    )(chunks)


def _adamw(parts, w, m, v, *, name):
    r, c_dim = w.shape
    tr = r
    for cand in range(ADAM_MAX_ROWS, PACK_ROW_ALIGN - 1, -PACK_ROW_ALIGN):
        if r > ADAM_MAX_ROWS and r % cand == 0:
            tr = cand
            break

    def kern(p_ref, w_ref, m_ref, v_ref, g_ref, d_ref, nm_ref, nv_ref):
        g = p_ref[0].astype(F32)
        for i in range(1, N_DEV):
            g = g + p_ref[i].astype(F32)
        nm = ADAM_B1 * m_ref[...] + (1.0 - ADAM_B1) * g
        nv = ADAM_B2 * v_ref[...] + (1.0 - ADAM_B2) * (g * g)
        m_hat = nm / (1.0 - ADAM_B1 ** ADAM_STEP)
        v_hat = nv / (1.0 - ADAM_B2 ** ADAM_STEP)
        g_ref[...] = g
        d_ref[...] = -ADAM_LR * (m_hat / (jnp.sqrt(v_hat) + ADAM_EPS) + ADAM_WD * w_ref[...])
        nm_ref[...] = nm
        nv_ref[...] = nv

    row = pl.BlockSpec((tr, c_dim), lambda i: (i, 0))
    return pl.pallas_call(
        kern, name=name, out_shape=tuple(jax.ShapeDtypeStruct((r, c_dim), F32) for _ in range(4)), grid=(r // tr,),
        in_specs=[pl.BlockSpec((N_DEV, tr, c_dim), lambda i: (0, i, 0)), row, row, row],
        out_specs=(row, row, row, row), compiler_params=_params(("parallel",)),
    )(parts, w, m, v)


SHARDED = (("w_in", 1), ("w_out", 0), ("w_mlp_in", 1), ("w_mlp_out", 0), ("w_ukv", 1), ("w_uq", 1))
REPLICATED = ("g_mix", "q_norm", "kv_norm", "rpb", "out_norm_a", "out_norm_b", "out_norm_c", "g_mlp", "g_final")
WEIGHTS = ("g_mix", "w_in", "q_norm", "w_uq", "kv_norm", "w_ukv", "rpb", "out_norm_a", "out_norm_b", "out_norm_c",
           "w_out", "g_mlp", "w_mlp_in", "w_mlp_out", "g_final")


def _pack_rows(arrs, dtype):
    parts = [a.reshape(-1, PACK_COLS).astype(dtype) for a in arrs]
    rows = sum(p.shape[0] for p in parts)
    pad = -rows % PACK_ROW_ALIGN
    if pad:
        parts.append(jnp.zeros((pad, PACK_COLS), dtype))
    return jnp.concatenate(parts, axis=0)


def _unpack_rows(packed, shapes):
    out, off = [], 0
    for shp in shapes:
        rows = int(np.prod(shp)) // PACK_COLS
        out.append(packed[..., off:off + rows, :].reshape(packed.shape[:-2] + tuple(shp)))
        off += rows
    return out


def _pack_flat(arrs):
    flat = jnp.concatenate([a.reshape(-1) for a in arrs])
    pad = -flat.shape[0] % (8 * PACK_COLS)
    return jnp.pad(flat, (0, pad)).reshape(-1, PACK_COLS)


def _unpack_flat(packed, shapes):
    flat = packed.reshape(-1)
    out, off = [], 0
    for shp in shapes:
        size = int(np.prod(shp))
        out.append(flat[off:off + size].reshape(shp))
        off += size
    return out


def _gathered_to_full(g, axis):
    n, l, a, b = g.shape
    if axis == 0:
        return jnp.transpose(g, (1, 0, 2, 3)).reshape(l, n * a, b)
    return jnp.transpose(g, (1, 2, 0, 3)).reshape(l, a, n * b)


def _full_to_chunks(w, axis):
    l, a, b = w.shape
    if axis == 0:
        return jnp.transpose(w.reshape(l, N_DEV, a // N_DEV, b), (1, 0, 2, 3))
    return jnp.transpose(w.reshape(l, a, N_DEV, b // N_DEV), (2, 0, 1, 3))


def kernel(x, g_mix, w_in, q_norm, w_uq, kv_norm, w_ukv, rpb, out_norm_a, out_norm_b, out_norm_c, w_out, g_mlp, w_mlp_in, w_mlp_out, g_final, loss_target, m_g_mix, m_w_in, m_q_norm, m_w_uq, m_kv_norm, m_w_ukv, m_rpb, m_out_norm_a, m_out_norm_b, m_out_norm_c, m_w_out, m_g_mlp, m_w_mlp_in, m_w_mlp_out, m_g_final, v_g_mix, v_w_in, v_q_norm, v_w_uq, v_kv_norm, v_w_ukv, v_rpb, v_out_norm_a, v_out_norm_b, v_out_norm_c, v_w_out, v_g_mlp, v_w_mlp_in, v_w_mlp_out, v_g_final):
    given = dict(locals())
    w = {n: given[n] for n in WEIGHTS}
    m = {n: given["m_" + n] for n in WEIGHTS}
    v = {n: given["v_" + n] for n in WEIGHTS}
    s = x.shape[1]
    depth = g_mix.shape[0]
    shard_shapes = [w[n].shape for n, _ in SHARDED]

    gathered = _all_gather(_pack_rows([w[n] for n, _ in SHARDED], BF16), name="gather_weights")
    full = {}
    for (n, axis), g in zip(SHARDED, _unpack_rows(gathered, shard_shapes)):
        full[n] = _gathered_to_full(g, axis)
    stand_in = {n: jnp.broadcast_to(jnp.zeros((), F32), full[n].shape[1:]) for n, _ in SHARDED}
    w_in = full["w_in"]
    full["w_in"] = jnp.concatenate(
        [w_in[..., COLS_A:COLS_A + COLS_B], w_in[..., COLS_A + COLS_B:], w_in[..., :COLS_A]], axis=-1)

    layer_w = [dict(stand_in, **{n: w[n][l] for n in REPLICATED if n != "g_final"}) for l in range(depth)]
    layer_w16 = [{n: full[n][l] for n, _ in SHARDED} for l in range(depth)]
    rope32 = _rope_tables(s, QK_ROPE)
    rope64 = _rope_tables(s, HEAD_DIM)

    def trunk(x0, lws):
        h = x0
        for lw, lw16 in zip(lws, layer_w16):
            h = _layer(h, lw, lw16, rope32, rope64)
        return h

    x_out, pullback = jax.vjp(trunk, x[0], layer_w)
    loss_local, dx_out, d_g_final = _loss_head(x_out, g_final, loss_target[0], name="loss_head")
    dx0, d_layers = pullback(dx_out)
    loss = lax.psum(loss_local, AXES)

    grads_local = {n: jnp.stack([d[n] for d in d_layers]) for n in layer_w[0]}
    grads_local["g_final"] = d_g_final

    chunks = [_full_to_chunks(grads_local[n], axis).reshape(N_DEV, -1, PACK_COLS).astype(BF16)
              for n, axis in SHARDED]
    rows = sum(c.shape[1] for c in chunks)
    pad = -rows % PACK_ROW_ALIGN
    if pad:
        chunks.append(jnp.zeros((N_DEV, pad, PACK_COLS), BF16))
    parts = _all_to_all(jnp.concatenate(chunks, axis=1), name="scatter_grads")
    big = _adamw(parts, _pack_rows([w[n] for n, _ in SHARDED], F32), _pack_rows([m[n] for n, _ in SHARDED], F32),
                 _pack_rows([v[n] for n, _ in SHARDED], F32), name="adamw_sharded")
    big = [_unpack_rows(o, shard_shapes) for o in big]

    rep_shapes = [w[n].shape for n in REPLICATED]
    rep_parts = _all_gather(_pack_flat([grads_local[n] for n in REPLICATED]), name="gather_small_grads")
    small = _adamw(rep_parts, _pack_flat([w[n] for n in REPLICATED]), _pack_flat([m[n] for n in REPLICATED]),
                   _pack_flat([v[n] for n in REPLICATED]), name="adamw_replicated")
    small = [_unpack_flat(o, rep_shapes) for o in small]

    result = {}
    for kind, idx in (("grad", 0), ("delta", 1), ("new_m", 2), ("new_v", 3)):
        for j, (n, _) in enumerate(SHARDED):
            result[kind + "_" + n] = big[idx][j]
        for j, n in enumerate(REPLICATED):
            result[kind + "_" + n] = small[idx][j]
    outs = [loss, dx0[None]]
    for kind in ("grad", "delta", "new_m", "new_v"):
        outs += [result[kind + "_" + n] for n in WEIGHTS]
    return tuple(outs)
```

```python
import functools

import numpy as np
import jax
import jax.numpy as jnp
from jax import lax
from jax.experimental import pallas as pl
from jax.experimental.pallas import tpu as pltpu

F32 = jnp.float32
BF16 = jnp.bfloat16

D_MODEL = 1024
DEPTH = 4
HEAD_DIM = 64
HEADS_A = 6
HEADS_B = 6
HEADS_C = 4
Q_LORA = 256
KV_LORA = 128
QK_NOPE = 64
QK_ROPE = 32
V_DIM_A = 64
DILATED_PAIRS = ((128, 1), (512, 4), (2048, 16))
BAND_HALF = 64
GRID_W = 64
NA_ROWS = 8
NA_COLS = 16
D_FF = 4096
ROPE_THETA = 10000.0
NORM_EPS = 1e-6
NEG_INF = -1e30
LOG2_E = 1.4426950408889634

COLS_A = Q_LORA + KV_LORA + QK_ROPE
COLS_B = 3 * HEADS_B * HEAD_DIM
COLS_C = 3 * HEADS_C * HEAD_DIM
WIDTH_A = HEADS_A * V_DIM_A
WIDTH_B = HEADS_B * HEAD_DIM
WIDTH_C = HEADS_C * HEAD_DIM

ADAM_LR = 0.001
ADAM_B1 = 0.9
ADAM_B2 = 0.999
ADAM_EPS = 1e-08
ADAM_WD = 0.01
ADAM_STEP = 10

N_DEV = 8
AXES = ("x", "y", "c")
MESH = pl.DeviceIdType.MESH

V7X_VMEM_LIMIT = 48 * 1024 * 1024
MM_VMEM_BUDGET = 32 * 1024 * 1024
LANES = 128
PACK_COLS = 1024
PACK_ROW_ALIGN = 16
ADAM_MAX_ROWS = 160
DENSE_TQ_FWD = 512
DENSE_TK_FWD = 2048
DENSE_TQ = 256
DENSE_TK_BWD = 2048
WIN_TILES_PER_STEP = 4
BAND_TILE = 256
NA_TILE_ROWS = 4


def _params(sem=None):
    return pltpu.CompilerParams(dimension_semantics=sem, vmem_limit_bytes=V7X_VMEM_LIMIT)


def _tile(dim, pref):
    if dim <= pref:
        return dim
    for t in range(pref - pref % LANES, LANES - 1, -LANES):
        if dim % t == 0:
            return t
    return dim


def _mm(a, b, *, ta=False, tb=False, out_dtype=F32, epilogue=None, extra=None, splits=None, name):
    if ta:
        k_dim, m_dim = a.shape
    else:
        m_dim, k_dim = a.shape
    n_dim = b.shape[0] if tb else b.shape[1]
    tn = _tile(n_dim, 1024) if n_dim % LANES == 0 else n_dim
    tk = _tile(k_dim, 1024 if not ta else 512) if k_dim % LANES == 0 else k_dim
    if epilogue == "split":
        tn = n_dim
        assert sum(splits) == n_dim and all(off % LANES == 0 for off in np.cumsum(splits)[:-1])
    n_out = 2 if epilogue == "relu2" else 1
    for tm in (_tile(m_dim, 1024), _tile(m_dim, 512), _tile(m_dim, 256)):
        blocks = (tm * tk * a.dtype.itemsize + tk * tn * b.dtype.itemsize
                  + (tm * tn * 4 if extra is not None else 0) + n_out * tm * tn * 4)
        if 2 * blocks + tm * tn * 4 <= MM_VMEM_BUDGET:
            break
    nk = k_dim // tk
    a_spec = (pl.BlockSpec((tk, tm), lambda i, j, k: (k, i)) if ta
              else pl.BlockSpec((tm, tk), lambda i, j, k: (i, k)))
    b_spec = (pl.BlockSpec((tn, tk), lambda i, j, k: (j, k)) if tb
              else pl.BlockSpec((tk, tn), lambda i, j, k: (k, j)))
    o_spec = pl.BlockSpec((tm, tn), lambda i, j, k: (i, j))
    dims = (((0 if ta else 1,), (1 if tb else 0,)), ((), ()))
    in_specs = [a_spec, b_spec]
    operands = [a, b]
    if epilogue in ("drelu2", "add"):
        in_specs.append(o_spec)
        operands.append(extra)
    if epilogue == "relu2":
        out_shape = (jax.ShapeDtypeStruct((m_dim, n_dim), F32), jax.ShapeDtypeStruct((m_dim, n_dim), BF16))
        out_specs = (o_spec, o_spec)
    elif epilogue == "split":
        out_shape = tuple(jax.ShapeDtypeStruct((m_dim, w), out_dtype) for w in splits)
        out_specs = tuple(pl.BlockSpec((tm, w), lambda i, j, k: (i, 0)) for w in splits)
    else:
        out_shape = jax.ShapeDtypeStruct((m_dim, n_dim), out_dtype)
        out_specs = o_spec

    def kern(*refs):
        acc_ref = refs[-1]
        a_ref, b_ref = refs[0], refs[1]
        k = pl.program_id(2)

        @pl.when(k == 0)
        def _():
            acc_ref[...] = jnp.zeros_like(acc_ref)

        acc_ref[...] += lax.dot_general(a_ref[...].astype(BF16), b_ref[...].astype(BF16), dims,
                                        preferred_element_type=F32)

        @pl.when(k == nk - 1)
        def _():
            acc = acc_ref[...]
            if epilogue == "relu2":
                refs[2][...] = acc
                r = jnp.maximum(acc, 0.0)
                refs[3][...] = (r * r).astype(BF16)
            elif epilogue == "drelu2":
                refs[3][...] = (acc * (2.0 * jnp.maximum(refs[2][...], 0.0))).astype(out_dtype)
            elif epilogue == "add":
                refs[3][...] = (acc + refs[2][...]).astype(out_dtype)
            elif epilogue == "split":
                off = 0
                for part, w in enumerate(splits):
                    refs[2 + part][...] = acc[:, off:off + w].astype(out_dtype)
                    off += w
            else:
                refs[2][...] = acc.astype(out_dtype)

    return pl.pallas_call(
        kern, name=name, out_shape=out_shape, grid=(m_dim // tm, n_dim // tn, nk),
        in_specs=in_specs, out_specs=out_specs, scratch_shapes=[pltpu.VMEM((tm, tn), F32)],
        compiler_params=_params(("parallel", "parallel", "arbitrary")),
    )(*operands)


def _norm_fwd(x, g, *, out_dtype, name):
    s, w = x.shape
    ts = _tile(s, 512)

    def kern(x_ref, g_ref, y_ref):
        xv = x_ref[...]
        r = lax.rsqrt(jnp.mean(xv * xv, axis=-1, keepdims=True) + NORM_EPS)
        y_ref[...] = (xv * r * g_ref[...]).astype(out_dtype)

    return pl.pallas_call(
        kern, name=name, out_shape=jax.ShapeDtypeStruct((s, w), out_dtype), grid=(s // ts,),
        in_specs=[pl.BlockSpec((ts, w), lambda i: (i, 0)), pl.BlockSpec((1, w), lambda i: (0, 0))],
        out_specs=pl.BlockSpec((ts, w), lambda i: (i, 0)),
        compiler_params=_params(("parallel",)),
    )(x, g.reshape(1, w))


def _norm_bwd(x, g, dy, *, name):
    s, w = x.shape
    ts = _tile(s, 512)

    def kern(x_ref, g_ref, dy_ref, dx_ref, dg_ref):
        @pl.when(pl.program_id(0) == 0)
        def _():
            dg_ref[...] = jnp.zeros_like(dg_ref)

        xv = x_ref[...]
        dyv = dy_ref[...]
        r = lax.rsqrt(jnp.mean(xv * xv, axis=-1, keepdims=True) + NORM_EPS)
        xhat = xv * r
        dg_ref[...] += jnp.sum(dyv * xhat, axis=0, keepdims=True)
        dxhat = dyv * g_ref[...]
        dx_ref[...] = r * (dxhat - xhat * jnp.mean(dxhat * xhat, axis=-1, keepdims=True))

    dx, dg = pl.pallas_call(
        kern, name=name,
        out_shape=(jax.ShapeDtypeStruct((s, w), F32), jax.ShapeDtypeStruct((1, w), F32)), grid=(s // ts,),
        in_specs=[pl.BlockSpec((ts, w), lambda i: (i, 0)), pl.BlockSpec((1, w), lambda i: (0, 0)),
                  pl.BlockSpec((ts, w), lambda i: (i, 0))],
        out_specs=(pl.BlockSpec((ts, w), lambda i: (i, 0)), pl.BlockSpec((1, w), lambda i: (0, 0))),
        compiler_params=_params(("arbitrary",)),
    )(x, g.reshape(1, w), dy)
    return dx, dg.reshape(w)


def _loss_head(x, g, target, *, name):
    s, w = x.shape
    ts = _tile(s, 512)

    def kern(x_ref, g_ref, t_ref, loss_ref, dx_ref, dg_ref):
        @pl.when(pl.program_id(0) == 0)
        def _():
            dg_ref[...] = jnp.zeros_like(dg_ref)
            loss_ref[...] = jnp.zeros_like(loss_ref)

        xv = x_ref[...]
        gv = g_ref[...]
        r = lax.rsqrt(jnp.mean(xv * xv, axis=-1, keepdims=True) + NORM_EPS)
        xhat = xv * r
        err = xhat * gv - t_ref[...]
        loss_ref[...] += 0.5 * jnp.sum(jnp.mean(err * err, axis=-1, keepdims=True))
        dyv = err * (1.0 / w)
        dg_ref[...] += jnp.sum(dyv * xhat, axis=0, keepdims=True)
        dxhat = dyv * gv
        dx_ref[...] = r * (dxhat - xhat * jnp.mean(dxhat * xhat, axis=-1, keepdims=True))

    loss, dx, dg = pl.pallas_call(
        kern, name=name,
        out_shape=(jax.ShapeDtypeStruct((1, LANES), F32), jax.ShapeDtypeStruct((s, w), F32),
                   jax.ShapeDtypeStruct((1, w), F32)),
        grid=(s // ts,),
        in_specs=[pl.BlockSpec((ts, w), lambda i: (i, 0)), pl.BlockSpec((1, w), lambda i: (0, 0)),
                  pl.BlockSpec((ts, w), lambda i: (i, 0))],
        out_specs=(pl.BlockSpec((1, LANES), lambda i: (0, 0)), pl.BlockSpec((ts, w), lambda i: (i, 0)),
                   pl.BlockSpec((1, w), lambda i: (0, 0))),
        compiler_params=_params(("arbitrary",)),
    )(x, g.reshape(1, w), target)
    return loss[0, 0], dx, dg.reshape(w)


def _dense_fwd(q, k, v_ones, *, dv, scale, name):
    g_dim, n, dq = q.shape
    dve = v_ones.shape[-1]
    tq = _tile(n, DENSE_TQ_FWD)
    tk = _tile(n, DENSE_TK_FWD)
    nk = n // tk
    to_log2 = scale * LOG2_E

    def kern(q_ref, k_ref, v_ref, o_ref, lse_ref):
        qb = q_ref[0]

        def body(j, carry):
            m, acc = carry
            start = pl.multiple_of(j * tk, tk)
            kb = k_ref[0, pl.ds(start, tk), :]
            s = lax.dot_general(qb, kb, (((1,), (1,)), ((), ())), preferred_element_type=F32)
            m_new = jnp.maximum(m, jnp.max(s, axis=-1, keepdims=True))
            p = jnp.exp2(((s - m_new) * to_log2).astype(BF16))
            acc = jnp.exp2((m - m_new) * to_log2) * acc + jnp.dot(
                p, v_ref[0, pl.ds(start, tk), :], preferred_element_type=F32)
            return m_new, acc

        m, acc = lax.fori_loop(0, nk, body, (jnp.full((tq, 1), NEG_INF, F32), jnp.zeros((tq, dve), F32)))
        l = acc[:, dv:dv + 1]
        o_ref[0] = acc[:, :dv] / l
        lse_ref[0] = m * scale + jnp.log(l)

    return pl.pallas_call(
        kern, name=name,
        out_shape=(jax.ShapeDtypeStruct((g_dim, n, dv), F32), jax.ShapeDtypeStruct((g_dim, n, 1), F32)),
        grid=(g_dim, n // tq),
        in_specs=[pl.BlockSpec((1, tq, dq), lambda g, i: (g, i, 0)), pl.BlockSpec((1, n, dq), lambda g, i: (g, 0, 0)),
                  pl.BlockSpec((1, n, dve), lambda g, i: (g, 0, 0))],
        out_specs=(pl.BlockSpec((1, tq, dv), lambda g, i: (g, i, 0)), pl.BlockSpec((1, tq, 1), lambda g, i: (g, i, 0))),
        compiler_params=_params(("parallel", "arbitrary")),
    )(q, k, v_ones)


def _dense_bwd(q_t, k, k_t, v, o_t, lse, do_t, *, scale, name):
    g_dim, dq, n = q_t.shape
    dv = v.shape[-1]
    tq = _tile(n, DENSE_TQ)
    tk = _tile(n, DENSE_TK_BWD)
    nk = n // tk
    nt = (((1,), (1,)), ((), ()))

    def kern(q_ref, k_ref, kt_ref, v_ref, o_ref, lse_ref, do_ref, dq_ref, dk_ref, dv_ref):
        @pl.when(pl.program_id(1) == 0)
        def _():
            dk_ref[...] = jnp.zeros_like(dk_ref)
            dv_ref[...] = jnp.zeros_like(dv_ref)

        qb = q_ref[0]
        dob = do_ref[0]
        delta = jnp.sum(dob * o_ref[0], axis=0, keepdims=True)
        dob16 = dob.astype(BF16)
        lse_v = lse_ref[0]

        def body(j, dq_acc):
            start = pl.multiple_of(j * tk, tk)
            s = jnp.dot(k_ref[0, pl.ds(start, tk), :], qb, preferred_element_type=F32) * scale
            p = jnp.exp(s - lse_v)
            dp = jnp.dot(v_ref[0, pl.ds(start, tk), :], dob16, preferred_element_type=F32)
            ds16 = (p * (dp - delta) * scale).astype(BF16)
            dv_ref[0, pl.ds(start, tk), :] += lax.dot_general(p.astype(BF16), dob16, nt, preferred_element_type=F32)
            dk_ref[0, pl.ds(start, tk), :] += lax.dot_general(ds16, qb, nt, preferred_element_type=F32)
            return dq_acc + jnp.dot(kt_ref[0, :, pl.ds(start, tk)], ds16, preferred_element_type=F32)

        dq_ref[0] = lax.fori_loop(0, nk, body, jnp.zeros((dq, tq), F32))

    qt_spec = pl.BlockSpec((1, dq, tq), lambda g, i: (g, 0, i))
    ot_spec = pl.BlockSpec((1, dv, tq), lambda g, i: (g, 0, i))
    l_spec = pl.BlockSpec((1, 1, tq), lambda g, i: (g, 0, i))
    k_spec = pl.BlockSpec((1, n, dq), lambda g, i: (g, 0, 0))
    kt_spec = pl.BlockSpec((1, dq, n), lambda g, i: (g, 0, 0))
    v_spec = pl.BlockSpec((1, n, dv), lambda g, i: (g, 0, 0))
    return pl.pallas_call(
        kern, name=name,
        out_shape=(jax.ShapeDtypeStruct((g_dim, dq, n), F32), jax.ShapeDtypeStruct((g_dim, n, dq), F32),
                   jax.ShapeDtypeStruct((g_dim, n, dv), F32)),
        grid=(g_dim, n // tq), in_specs=[qt_spec, k_spec, kt_spec, v_spec, ot_spec, l_spec, ot_spec],
        out_specs=(qt_spec, k_spec, v_spec), compiler_params=_params(("parallel", "arbitrary")),
    )(q_t, k, k_t, v, o_t, lse, do_t)


def _make_dense_attention(scale, name):
    @jax.custom_vjp
    def att(q, k, v):
        return att_fwd(q, k, v)[0]

    def att_fwd(q, k, v):
        q16 = jnp.transpose(q.astype(BF16), (1, 0, 2))
        k16 = jnp.transpose(k.astype(BF16), (1, 0, 2))
        v16 = jnp.transpose(v.astype(BF16), (1, 0, 2))
        h, n, dv = v16.shape
        v_ones = jnp.concatenate([v16, jnp.ones((h, n, 1), BF16), jnp.zeros((h, n, LANES - dv - 1), BF16)], axis=-1)
        o, lse = _dense_fwd(q16, k16, v_ones, dv=dv, scale=scale, name=name + "_fwd")
        return jnp.transpose(o, (1, 0, 2)), (q16, k16, v16, o, lse)

    def att_bwd(res, do):
        q16, k16, v16, o, lse = res
        dq_t, dk, dv_ = _dense_bwd(jnp.transpose(q16, (0, 2, 1)), k16, jnp.transpose(k16, (0, 2, 1)), v16,
                                   jnp.transpose(o, (0, 2, 1)), jnp.transpose(lse, (0, 2, 1)),
                                   jnp.transpose(do, (1, 2, 0)), scale=scale, name=name + "_bwd")
        return jnp.transpose(dq_t, (2, 0, 1)), jnp.transpose(dk, (1, 0, 2)), jnp.transpose(dv_, (1, 0, 2))

    att.defvjp(att_fwd, att_bwd)
    return att


def _window(mode, n):
    if mode == "band":
        tq = min(BAND_TILE, n)
        return tq, BAND_HALF, tq + 2 * BAND_HALF
    tq = NA_TILE_ROWS * GRID_W
    front = (NA_ROWS // 2) * GRID_W
    return tq, front, tq + NA_ROWS * GRID_W


def _span_start(mode, i0, front, span, n):
    return i0, i0 - front


def _window_mask(mode, i0, k0, tq, span, n):
    q_pos = i0 + lax.broadcasted_iota(jnp.int32, (tq, 1), 0)
    k_pos = k0 + lax.broadcasted_iota(jnp.int32, (1, span), 1)
    if mode == "band":
        diff = k_pos - q_pos
        return (diff <= BAND_HALF) & (diff >= -BAND_HALF) & (k_pos >= 0) & (k_pos < n)
    rows = n // GRID_W
    shift = GRID_W.bit_length() - 1
    r_start = jnp.clip((q_pos >> shift) - NA_ROWS // 2, 0, rows - NA_ROWS)
    c_start = jnp.clip((q_pos & (GRID_W - 1)) - NA_COLS // 2, 0, GRID_W - NA_COLS)
    kr = k_pos >> shift
    kc = k_pos & (GRID_W - 1)
    return (kr >= r_start) & (kr < r_start + NA_ROWS) & (kc >= c_start) & (kc < c_start + NA_COLS)


def _win_fwd(q, k, v, bias, *, mode, scale, name):
    g_dim, n, d = q.shape
    tq, front, span = _window(mode, n)
    sub = min(WIN_TILES_PER_STEP, n // tq)
    n_pad = k.shape[1]

    def kern(*refs):
        if mode == "natten":
            q_ref, k_ref, v_ref, b_ref, o_ref, lse_ref = refs
        else:
            q_ref, k_ref, v_ref, o_ref, lse_ref = refs
        for t in range(sub):
            rows = pl.ds(t * tq, tq)
            i0 = pl.multiple_of(pl.program_id(1) * (sub * tq) + t * tq, tq)
            start, k0 = _span_start(mode, i0, front, span, n)
            kb = k_ref[0, pl.ds(start, span), :]
            vb = v_ref[0, pl.ds(start, span), :]
            s = lax.dot_general(q_ref[0, rows, :], kb, (((1,), (1,)), ((), ())), preferred_element_type=F32) * scale
            if mode == "natten":
                s = s + b_ref[0]
            s = jnp.where(_window_mask(mode, i0, k0, tq, span, n), s, NEG_INF)
            m = jnp.max(s, axis=-1, keepdims=True)
            p = jnp.exp(s - m)
            l = jnp.sum(p, axis=-1, keepdims=True)
            o_ref[0, rows, :] = jnp.dot(p.astype(BF16), vb, preferred_element_type=F32) / l
            lse_ref[0, rows, :] = m + jnp.log(l)

    in_specs = [pl.BlockSpec((1, sub * tq, d), lambda g, i: (g, i, 0)),
                pl.BlockSpec((1, n_pad, d), lambda g, i: (g, 0, 0)), pl.BlockSpec((1, n_pad, d), lambda g, i: (g, 0, 0))]
    operands = [q, k, v]
    if mode == "natten":
        in_specs.append(pl.BlockSpec((1, tq, span), lambda g, i: (g, 0, 0)))
        operands.append(bias)
    return pl.pallas_call(
        kern, name=name,
        out_shape=(jax.ShapeDtypeStruct((g_dim, n, d), F32), jax.ShapeDtypeStruct((g_dim, n, 1), F32)),
        grid=(g_dim, n // (sub * tq)), in_specs=in_specs,
        out_specs=(pl.BlockSpec((1, sub * tq, d), lambda g, i: (g, i, 0)),
                   pl.BlockSpec((1, sub * tq, 1), lambda g, i: (g, i, 0))),
        compiler_params=_params(("parallel", "arbitrary")),
    )(*operands)


def _win_bwd(q, k, v, bias, o, lse, do, dlse, *, mode, scale, name):
    g_dim, n, d = q.shape
    tq, front, span = _window(mode, n)
    sub = min(WIN_TILES_PER_STEP, n // tq)
    n_pad = k.shape[1]

    def kern(*refs):
        if mode == "natten":
            q_ref, k_ref, v_ref, b_ref, o_ref, lse_ref, do_ref, dlse_ref, dq_ref, dk_ref, dv_ref, db_ref = refs
        else:
            q_ref, k_ref, v_ref, o_ref, lse_ref, do_ref, dlse_ref, dq_ref, dk_ref, dv_ref = refs

        @pl.when(pl.program_id(1) == 0)
        def _():
            dk_ref[...] = jnp.zeros_like(dk_ref)
            dv_ref[...] = jnp.zeros_like(dv_ref)
            if mode == "natten":
                db_ref[...] = jnp.zeros_like(db_ref)

        for t in range(sub):
            rows = pl.ds(t * tq, tq)
            i0 = pl.multiple_of(pl.program_id(1) * (sub * tq) + t * tq, tq)
            qb = q_ref[0, rows, :]
            start, k0 = _span_start(mode, i0, front, span, n)
            kb = k_ref[0, pl.ds(start, span), :]
            vb = v_ref[0, pl.ds(start, span), :]
            dob = do_ref[0, rows, :]
            delta = jnp.sum(dob * o_ref[0, rows, :], axis=-1, keepdims=True) - dlse_ref[0, rows, :]
            dob16 = dob.astype(BF16)
            s = lax.dot_general(qb, kb, (((1,), (1,)), ((), ())), preferred_element_type=F32) * scale
            if mode == "natten":
                s = s + b_ref[0]
            p = jnp.where(_window_mask(mode, i0, k0, tq, span, n), jnp.exp(s - lse_ref[0, rows, :]), 0.0)
            dp = lax.dot_general(dob16, vb, (((1,), (1,)), ((), ())), preferred_element_type=F32)
            ds = p * (dp - delta)
            if mode == "natten":
                db_ref[0] += ds
            ds16 = (ds * scale).astype(BF16)
            dv_ref[0, pl.ds(start, span), :] += lax.dot_general(
                p.astype(BF16), dob16, (((0,), (0,)), ((), ())), preferred_element_type=F32)
            dk_ref[0, pl.ds(start, span), :] += lax.dot_general(
                ds16, qb, (((0,), (0,)), ((), ())), preferred_element_type=F32)
            dq_ref[0, rows, :] = jnp.dot(ds16, kb, preferred_element_type=F32)

    q_spec = pl.BlockSpec((1, sub * tq, d), lambda g, i: (g, i, 0))
    k_spec = pl.BlockSpec((1, n_pad, d), lambda g, i: (g, 0, 0))
    l_spec = pl.BlockSpec((1, sub * tq, 1), lambda g, i: (g, i, 0))
    in_specs = [q_spec, k_spec, k_spec]
    operands = [q, k, v]
    out_shape = [jax.ShapeDtypeStruct((g_dim, n, d), F32), jax.ShapeDtypeStruct((g_dim, n_pad, d), F32),
                 jax.ShapeDtypeStruct((g_dim, n_pad, d), F32)]
    out_specs = [q_spec, k_spec, k_spec]
    if mode == "natten":
        b_spec = pl.BlockSpec((1, tq, span), lambda g, i: (g, 0, 0))
        in_specs.append(b_spec)
        operands.append(bias)
        out_shape.append(jax.ShapeDtypeStruct(bias.shape, F32))
        out_specs.append(b_spec)
    in_specs += [q_spec, l_spec, q_spec, l_spec]
    operands += [o, lse, do, dlse]
    return pl.pallas_call(
        kern, name=name, out_shape=tuple(out_shape), grid=(g_dim, n // (sub * tq)), in_specs=in_specs,
        out_specs=tuple(out_specs), compiler_params=_params(("parallel", "arbitrary")),
    )(*operands)


def _dilate(t, dil):
    s, h, d = t.shape
    return jnp.transpose(t.reshape(s // dil, dil, h, d), (1, 2, 0, 3)).reshape(dil * h, s // dil, d)


def _undilate(t, dil):
    gh, n, d = t.shape
    return jnp.transpose(t.reshape(dil, gh // dil, n, d), (2, 0, 1, 3)).reshape(n * dil, gh // dil, d)


def _make_window_attention(mode, scale, dil, name):
    @jax.custom_vjp
    def att(q, k, v, bias):
        return att_fwd(q, k, v, bias)[0]

    def att_fwd(q, k, v, bias):
        tq, front, span = _window(mode, q.shape[0] // dil)
        q16, k16, v16 = (_dilate(t.astype(BF16), dil) for t in (q, k, v))
        pad = ((0, 0), (front, span - tq - front), (0, 0))
        k16, v16 = jnp.pad(k16, pad), jnp.pad(v16, pad)
        o, lse = _win_fwd(q16, k16, v16, bias, mode=mode, scale=scale, name=name + "_fwd")
        return (_undilate(o, dil), _undilate(lse, dil)[..., 0]), (q16, k16, v16, bias, o, lse)

    def att_bwd(res, cts):
        q16, k16, v16, bias, o, lse = res
        do, dlse = cts
        n = q16.shape[1]
        outs = _win_bwd(q16, k16, v16, bias, o, lse, _dilate(do, dil), _dilate(dlse[..., None], dil),
                        mode=mode, scale=scale, name=name + "_bwd")
        front = _window(mode, n)[1]
        dq, dk, dv = outs[0], outs[1][:, front:front + n], outs[2][:, front:front + n]
        dbias = outs[3] if mode == "natten" else None
        return _undilate(dq, dil), _undilate(dk, dil), _undilate(dv, dil), dbias

    att.defvjp(att_fwd, att_bwd)
    return att


def _na_onehots():
    tq, front, span = _window("natten", 0)
    q_rows, k_rows, front_rows = tq // GRID_W, span // GRID_W, front // GRID_W
    e_r = np.zeros((q_rows, k_rows, 2 * NA_ROWS - 1), np.float32)
    for qr in range(q_rows):
        for kr in range(k_rows):
            a = kr - front_rows - qr + NA_ROWS - 1
            if 0 <= a < 2 * NA_ROWS - 1:
                e_r[qr, kr, a] = 1.0
    e_c = np.zeros((GRID_W, GRID_W, 2 * NA_COLS - 1), np.float32)
    for qc in range(GRID_W):
        for kc in range(GRID_W):
            b = kc - qc + NA_COLS - 1
            if 0 <= b < 2 * NA_COLS - 1:
                e_c[qc, kc, b] = 1.0
    return e_r, e_c


def _na_bias_tiles(rpb):
    tq, _, span = _window("natten", 0)
    e_r, e_c = _na_onehots()
    t = jnp.einsum("ikA,xyB,hAB->hixky", e_r, e_c, rpb, precision=lax.Precision.HIGHEST)
    return t.reshape(rpb.shape[0], tq, span)


def _make_norm_linear(name):
    @jax.custom_vjp
    def f(x, g, w_grad, w):
        return f_fwd(x, g, w_grad, w)[0]

    def f_fwd(x, g, w_grad, w):
        h = _norm_fwd(x, g, out_dtype=BF16, name=name + "_norm")
        return _mm(h, w, name=name + "_mm"), (x, g, w, h)

    def f_bwd(res, dy):
        x, g, w, h = res
        dh = _mm(dy, w, tb=True, name=name + "_dh")
        dw = _mm(h, dy, ta=True, name=name + "_dw")
        dx, dg = _norm_bwd(x, g, dh, name=name + "_dnorm")
        return dx, dg, dw, jnp.zeros_like(w)

    f.defvjp(f_fwd, f_bwd)
    return f


def _make_in_proj(name):
    splits = (COLS_B, COLS_C, COLS_A)
    offs = (0, COLS_B, COLS_B + COLS_C)

    @jax.custom_vjp
    def f(x, g, w_grad, w):
        return f_fwd(x, g, w_grad, w)[0]

    def f_fwd(x, g, w_grad, w):
        h = _norm_fwd(x, g, out_dtype=BF16, name=name + "_norm")
        return _mm(h, w, epilogue="split", splits=splits, name=name + "_mm"), (x, g, w, h)

    def f_bwd(res, dys):
        x, g, w, h = res
        dh = None
        dws = []
        for dy, off, width, tag in zip(dys, offs, splits, "bca"):
            w_part = w[:, off:off + width]
            dh = _mm(dy, w_part, tb=True, epilogue=None if dh is None else "add", extra=dh, name=name + "_dh_" + tag)
            dws.append(_mm(h, dy, ta=True, name=name + "_dw_" + tag))
        dx, dg = _norm_bwd(x, g, dh, name=name + "_dnorm")
        return dx, dg, jnp.concatenate([dws[2], dws[0], dws[1]], axis=1), jnp.zeros_like(w)

    f.defvjp(f_fwd, f_bwd)
    return f


def _make_mix_out(name):
    @jax.custom_vjp
    def f(x, oa, ob, oc, ga, gb, gc, w_grad, w):
        return f_fwd(x, oa, ob, oc, ga, gb, gc, w_grad, w)[0]

    def f_fwd(x, oa, ob, oc, ga, gb, gc, w_grad, w):
        mixed = jnp.concatenate([
            _norm_fwd(oa, ga, out_dtype=BF16, name=name + "_norm_a"),
            _norm_fwd(ob, gb, out_dtype=BF16, name=name + "_norm_b"),
            _norm_fwd(oc, gc, out_dtype=BF16, name=name + "_norm_c")], axis=-1)
        y = _mm(mixed, w, epilogue="add", extra=x, name=name + "_mm")
        return y, (oa, ob, oc, ga, gb, gc, w, mixed)

    def f_bwd(res, dy):
        oa, ob, oc, ga, gb, gc, w, mixed = res
        dmixed = _mm(dy, w, tb=True, name=name + "_dmixed")
        dw = _mm(mixed, dy, ta=True, name=name + "_dw")
        doa, dga = _norm_bwd(oa, ga, dmixed[:, :WIDTH_A], name=name + "_dnorm_a")
        dob, dgb = _norm_bwd(ob, gb, dmixed[:, WIDTH_A:WIDTH_A + WIDTH_B], name=name + "_dnorm_b")
        doc, dgc = _norm_bwd(oc, gc, dmixed[:, WIDTH_A + WIDTH_B:], name=name + "_dnorm_c")
        return dy, doa, dob, doc, dga, dgb, dgc, dw, jnp.zeros_like(w)

    f.defvjp(f_fwd, f_bwd)
    return f


def _make_mlp(name):
    @jax.custom_vjp
    def f(x, g, w1_grad, w2_grad, w1, w2):
        return f_fwd(x, g, w1_grad, w2_grad, w1, w2)[0]

    def f_fwd(x, g, w1_grad, w2_grad, w1, w2):
        h = _norm_fwd(x, g, out_dtype=BF16, name=name + "_norm")
        u, a = _mm(h, w1, epilogue="relu2", name=name + "_up")
        y = _mm(a, w2, epilogue="add", extra=x, name=name + "_down")
        return y, (x, g, w1, w2, h, u, a)

    def f_bwd(res, dy):
        x, g, w1, w2, h, u, a = res
        du = _mm(dy, w2, tb=True, epilogue="drelu2", extra=u, out_dtype=BF16, name=name + "_du")
        dw2 = _mm(a, dy, ta=True, name=name + "_dw2")
        dw1 = _mm(h, du, ta=True, name=name + "_dw1")
        dh = _mm(du, w1, tb=True, name=name + "_dh")
        dx, dg = _norm_bwd(x, g, dh, name=name + "_dnorm")
        return dx + dy, dg, dw1, dw2, jnp.zeros_like(w1), jnp.zeros_like(w2)

    f.defvjp(f_fwd, f_bwd)
    return f


def _rope_tables(s, dim):
    half = dim // 2
    inv_freq = ROPE_THETA ** (-jnp.arange(half, dtype=F32) / half)
    ang = jnp.arange(s, dtype=F32)[:, None] * inv_freq[None, :]
    return jnp.cos(ang)[:, None, :], jnp.sin(ang)[:, None, :]


def _rope(x, cos, sin):
    half = x.shape[-1] // 2
    x1, x2 = x[..., :half], x[..., half:]
    return jnp.concatenate([x1 * cos - x2 * sin, x1 * sin + x2 * cos], axis=-1)


def _layer(x, lw, lw16, rope32, rope64):
    s = x.shape[0]
    p_b, p_c, p_a = _make_in_proj("in")(x, lw["g_mix"], lw["w_in"], lw16["w_in"])
    c_q = p_a[:, :Q_LORA]
    c_kv = p_a[:, Q_LORA:Q_LORA + KV_LORA]
    k_pe = p_a[:, Q_LORA + KV_LORA:]
    p_b = p_b.reshape(s, 3, HEADS_B, HEAD_DIM)
    p_c = p_c.reshape(s, 3, HEADS_C, HEAD_DIM)

    qa = _make_norm_linear("uq")(c_q, lw["q_norm"], lw["w_uq"], lw16["w_uq"]).reshape(
        s, HEADS_A, QK_NOPE + QK_ROPE)
    kva = _make_norm_linear("ukv")(c_kv, lw["kv_norm"], lw["w_ukv"], lw16["w_ukv"]).reshape(
        s, HEADS_A, QK_NOPE + V_DIM_A)
    k_pe = jnp.broadcast_to(_rope(k_pe[:, None, :], *rope32), (s, HEADS_A, QK_ROPE))
    qa = jnp.concatenate([qa[..., :QK_NOPE], _rope(qa[..., QK_NOPE:], *rope32)], axis=-1)
    ka = jnp.concatenate([kva[..., :QK_NOPE], k_pe], axis=-1)
    att_a = _make_dense_attention((QK_NOPE + QK_ROPE) ** -0.5, "att_a")
    o_a = att_a(qa, ka, kva[..., QK_NOPE:]).reshape(s, WIDTH_A)

    qb = _rope(p_b[:, 0], *rope64)
    kb = _rope(p_b[:, 1], *rope64)
    vb = p_b[:, 2]
    outs, lses = [], []
    for _, dil in DILATED_PAIRS:
        o, lse = _make_window_attention("band", HEAD_DIM ** -0.5, dil, "att_b%d" % dil)(qb, kb, vb, None)
        outs.append(o)
        lses.append(lse)
    wgt = jax.nn.softmax(jnp.stack(lses, axis=-1), axis=-1)
    o_b = jnp.sum(jnp.stack(outs, axis=-1) * wgt[:, :, None, :], axis=-1).reshape(s, WIDTH_B)

    att_c = _make_window_attention("natten", HEAD_DIM ** -0.5, 1, "att_c")
    o_c, _ = att_c(p_c[:, 0], p_c[:, 1], p_c[:, 2], _na_bias_tiles(lw["rpb"]))
    o_c = o_c.reshape(s, WIDTH_C)

    x = _make_mix_out("out")(x, o_a, o_b, o_c, lw["out_norm_a"], lw["out_norm_b"], lw["out_norm_c"],
                             lw["w_out"], lw16["w_out"])
    return _make_mlp("mlp")(x, lw["g_mlp"], lw["w_mlp_in"], lw["w_mlp_out"], lw16["w_mlp_in"], lw16["w_mlp_out"])


def _place():
    return lax.axis_index("x"), lax.axis_index("y"), lax.axis_index("c")


def _all_gather(block, *, name):
    r, c_dim = block.shape

    def body(x_ref, out_ref, send_sems, recv_sems, local_sem):
        x, y, c = _place()
        me, sibling = (x, y, c), (x, y, 1 - c)
        chips = [(1 - x, y), (x, 1 - y), (1 - x, 1 - y)]

        def slot(px, py, pc):
            return out_ref.at[4 * px + 2 * py + pc]

        def copy(k, blk, to, src=None):
            return pltpu.make_async_remote_copy(
                src_ref=slot(*blk) if src is None else src, dst_ref=slot(*blk),
                send_sem=send_sems.at[k], recv_sem=recv_sems.at[k], device_id=to, device_id_type=MESH)

        mine = pltpu.make_async_copy(x_ref, slot(*me), local_sem)
        mine.start()
        first = [copy(0, me, sibling, src=x_ref)]
        first += [copy(1 + j, me, (*chip, c), src=x_ref) for j, chip in enumerate(chips)]
        for cp in first:
            cp.start()
        passed = [copy(4 + j, (*chip, c), sibling) for j, chip in enumerate(chips)]
        for j, chip in enumerate(chips):
            copy(1 + j, (*chip, c), me).wait_recv()
            passed[j].start()
        copy(0, sibling, me).wait_recv()
        for j, chip in enumerate(chips):
            copy(4 + j, (*chip, 1 - c), me).wait_recv()
        for cp in first + passed:
            cp.wait_send()
        mine.wait()

    return pl.pallas_call(
        body, name=name, out_shape=jax.ShapeDtypeStruct((N_DEV, r, c_dim), block.dtype),
        in_specs=[pl.BlockSpec(memory_space=pl.ANY)], out_specs=pl.BlockSpec(memory_space=pl.ANY),
        scratch_shapes=[pltpu.SemaphoreType.DMA((7,)), pltpu.SemaphoreType.DMA((7,)), pltpu.SemaphoreType.DMA(())],
    )(block)


def _all_to_all(chunks, *, name):
    def body(in_ref, out_ref, send_sems, recv_sems, local_sem):
        x, y, c = _place()
        me = 4 * x + 2 * y + c
        mine = pltpu.make_async_copy(in_ref.at[me], out_ref.at[me], local_sem)
        mine.start()
        peers = []
        for k in range(1, N_DEV):
            px = 1 - x if k & 4 else x
            py = 1 - y if k & 2 else y
            pc = 1 - c if k & 1 else c
            peers.append((px, py, pc))

        def copy(k, peer):
            pid = 4 * peer[0] + 2 * peer[1] + peer[2]
            return pltpu.make_async_remote_copy(
                src_ref=in_ref.at[pid], dst_ref=out_ref.at[me], send_sem=send_sems.at[k], recv_sem=recv_sems.at[k],
                device_id=peer, device_id_type=MESH)

        def landing(k, peer):
            pid = 4 * peer[0] + 2 * peer[1] + peer[2]
            return pltpu.make_async_remote_copy(
                src_ref=in_ref.at[pid], dst_ref=out_ref.at[pid], send_sem=send_sems.at[k], recv_sem=recv_sems.at[k],
                device_id=peer, device_id_type=MESH)

        sends = [copy(k, peer) for k, peer in enumerate(peers)]
        for cp in sends:
            cp.start()
        for k, peer in enumerate(peers):
            landing(k, peer).wait_recv()
        for cp in sends:
            cp.wait_send()
        mine.wait()

    return pl.pallas_call(
        body, name=name, out_shape=jax.ShapeDtypeStruct(chunks.shape, chunks.dtype),
        in_specs=[pl.BlockSpec(memory_space=pl.ANY)], out_specs=pl.BlockSpec(memory_space=pl.ANY),
        scratch_shapes=[pltpu.SemaphoreType.DMA((7,)), pltpu.SemaphoreType.DMA((7,)), pltpu.SemaphoreType.DMA(())],
    )(chunks)


def _adamw(parts, w, m, v, *, name):
    r, c_dim = w.shape
    tr = r
    for cand in range(ADAM_MAX_ROWS, PACK_ROW_ALIGN - 1, -PACK_ROW_ALIGN):
        if r > ADAM_MAX_ROWS and r % cand == 0:
            tr = cand
            break

    def kern(p_ref, w_ref, m_ref, v_ref, g_ref, d_ref, nm_ref, nv_ref):
        g = p_ref[0].astype(F32)
        for i in range(1, N_DEV):
            g = g + p_ref[i].astype(F32)
        nm = ADAM_B1 * m_ref[...] + (1.0 - ADAM_B1) * g
        nv = ADAM_B2 * v_ref[...] + (1.0 - ADAM_B2) * (g * g)
        m_hat = nm / (1.0 - ADAM_B1 ** ADAM_STEP)
        v_hat = nv / (1.0 - ADAM_B2 ** ADAM_STEP)
        g_ref[...] = g
        d_ref[...] = -ADAM_LR * (m_hat / (jnp.sqrt(v_hat) + ADAM_EPS) + ADAM_WD * w_ref[...])
        nm_ref[...] = nm
        nv_ref[...] = nv

    row = pl.BlockSpec((tr, c_dim), lambda i: (i, 0))
    return pl.pallas_call(
        kern, name=name, out_shape=tuple(jax.ShapeDtypeStruct((r, c_dim), F32) for _ in range(4)), grid=(r // tr,),
        in_specs=[pl.BlockSpec((N_DEV, tr, c_dim), lambda i: (0, i, 0)), row, row, row],
        out_specs=(row, row, row, row), compiler_params=_params(("parallel",)),
    )(parts, w, m, v)


SHARDED = (("w_in", 1), ("w_out", 0), ("w_mlp_in", 1), ("w_mlp_out", 0), ("w_ukv", 1), ("w_uq", 1))
REPLICATED = ("g_mix", "q_norm", "kv_norm", "rpb", "out_norm_a", "out_norm_b", "out_norm_c", "g_mlp", "g_final")
WEIGHTS = ("g_mix", "w_in", "q_norm", "w_uq", "kv_norm", "w_ukv", "rpb", "out_norm_a", "out_norm_b", "out_norm_c",
           "w_out", "g_mlp", "w_mlp_in", "w_mlp_out", "g_final")


def _pack_rows(arrs, dtype):
    parts = [a.reshape(-1, PACK_COLS).astype(dtype) for a in arrs]
    rows = sum(p.shape[0] for p in parts)
    pad = -rows % PACK_ROW_ALIGN
    if pad:
        parts.append(jnp.zeros((pad, PACK_COLS), dtype))
    return jnp.concatenate(parts, axis=0)


def _unpack_rows(packed, shapes):
    out, off = [], 0
    for shp in shapes:
        rows = int(np.prod(shp)) // PACK_COLS
        out.append(packed[..., off:off + rows, :].reshape(packed.shape[:-2] + tuple(shp)))
        off += rows
    return out


def _pack_flat(arrs):
    flat = jnp.concatenate([a.reshape(-1) for a in arrs])
    pad = -flat.shape[0] % (8 * PACK_COLS)
    return jnp.pad(flat, (0, pad)).reshape(-1, PACK_COLS)


def _unpack_flat(packed, shapes):
    flat = packed.reshape(-1)
    out, off = [], 0
    for shp in shapes:
        size = int(np.prod(shp))
        out.append(flat[off:off + size].reshape(shp))
        off += size
    return out


def _gathered_to_full(g, axis):
    n, l, a, b = g.shape
    if axis == 0:
        return jnp.transpose(g, (1, 0, 2, 3)).reshape(l, n * a, b)
    return jnp.transpose(g, (1, 2, 0, 3)).reshape(l, a, n * b)


def _full_to_chunks(w, axis):
    l, a, b = w.shape
    if axis == 0:
        return jnp.transpose(w.reshape(l, N_DEV, a // N_DEV, b), (1, 0, 2, 3))
    return jnp.transpose(w.reshape(l, a, N_DEV, b // N_DEV), (2, 0, 1, 3))


def kernel(x, g_mix, w_in, q_norm, w_uq, kv_norm, w_ukv, rpb, out_norm_a, out_norm_b, out_norm_c, w_out, g_mlp, w_mlp_in, w_mlp_out, g_final, loss_target, m_g_mix, m_w_in, m_q_norm, m_w_uq, m_kv_norm, m_w_ukv, m_rpb, m_out_norm_a, m_out_norm_b, m_out_norm_c, m_w_out, m_g_mlp, m_w_mlp_in, m_w_mlp_out, m_g_final, v_g_mix, v_w_in, v_q_norm, v_w_uq, v_kv_norm, v_w_ukv, v_rpb, v_out_norm_a, v_out_norm_b, v_out_norm_c, v_w_out, v_g_mlp, v_w_mlp_in, v_w_mlp_out, v_g_final):
    given = dict(locals())
    w = {n: given[n] for n in WEIGHTS}
    m = {n: given["m_" + n] for n in WEIGHTS}
    v = {n: given["v_" + n] for n in WEIGHTS}
    s = x.shape[1]
    depth = g_mix.shape[0]
    shard_shapes = [w[n].shape for n, _ in SHARDED]

    gathered = _all_gather(_pack_rows([w[n] for n, _ in SHARDED], BF16), name="gather_weights")
    full = {}
    for (n, axis), g in zip(SHARDED, _unpack_rows(gathered, shard_shapes)):
        full[n] = _gathered_to_full(g, axis)
    stand_in = {n: jnp.broadcast_to(jnp.zeros((), F32), full[n].shape[1:]) for n, _ in SHARDED}
    w_in = full["w_in"]
    full["w_in"] = jnp.concatenate(
        [w_in[..., COLS_A:COLS_A + COLS_B], w_in[..., COLS_A + COLS_B:], w_in[..., :COLS_A]], axis=-1)

    layer_w = [dict(stand_in, **{n: w[n][l] for n in REPLICATED if n != "g_final"}) for l in range(depth)]
    layer_w16 = [{n: full[n][l] for n, _ in SHARDED} for l in range(depth)]
    rope32 = _rope_tables(s, QK_ROPE)
    rope64 = _rope_tables(s, HEAD_DIM)

    def trunk(x0, lws):
        h = x0
        for lw, lw16 in zip(lws, layer_w16):
            h = _layer(h, lw, lw16, rope32, rope64)
        return h

    x_out, pullback = jax.vjp(trunk, x[0], layer_w)
    loss_local, dx_out, d_g_final = _loss_head(x_out, g_final, loss_target[0], name="loss_head")
    dx0, d_layers = pullback(dx_out)
    loss = lax.psum(loss_local, AXES)

    grads_local = {n: jnp.stack([d[n] for d in d_layers]) for n in layer_w[0]}
    grads_local["g_final"] = d_g_final

    chunks = [_full_to_chunks(grads_local[n], axis).reshape(N_DEV, -1, PACK_COLS).astype(BF16)
              for n, axis in SHARDED]
    rows = sum(c.shape[1] for c in chunks)
    pad = -rows % PACK_ROW_ALIGN
    if pad:
        chunks.append(jnp.zeros((N_DEV, pad, PACK_COLS), BF16))
    parts = _all_to_all(jnp.concatenate(chunks, axis=1), name="scatter_grads")
    big = _adamw(parts, _pack_rows([w[n] for n, _ in SHARDED], F32), _pack_rows([m[n] for n, _ in SHARDED], F32),
                 _pack_rows([v[n] for n, _ in SHARDED], F32), name="adamw_sharded")
    big = [_unpack_rows(o, shard_shapes) for o in big]

    rep_shapes = [w[n].shape for n in REPLICATED]
    rep_parts = _all_gather(_pack_flat([grads_local[n] for n in REPLICATED]), name="gather_small_grads")
    small = _adamw(rep_parts, _pack_flat([w[n] for n in REPLICATED]), _pack_flat([m[n] for n in REPLICATED]),
                   _pack_flat([v[n] for n in REPLICATED]), name="adamw_replicated")
    small = [_unpack_flat(o, rep_shapes) for o in small]

    result = {}
    for kind, idx in (("grad", 0), ("delta", 1), ("new_m", 2), ("new_v", 3)):
        for j, (n, _) in enumerate(SHARDED):
            result[kind + "_" + n] = big[idx][j]
        for j, n in enumerate(REPLICATED):
            result[kind + "_" + n] = small[idx][j]
    outs = [loss, dx0[None]]
    for kind in ("grad", "delta", "new_m", "new_v"):
        outs += [result[kind + "_" + n] for n in WEIGHTS]
    return tuple(outs)
```
